```python
import jax, jax.numpy as jnp
from jax import lax
import numpy as np

D_MODEL = 2048
BATCH = 4
SEQ = 2048
DEPTH = 1
DEC_BATCH = 128
DEC_SEQ = 1
PAST_LEN = 16384
PAGE_SIZE = 128

MIX_WIDTH = D_MODEL
W_LRU = MIX_WIDTH // 2
LRU_HEADS = 8
LRU_HD = W_LRU // LRU_HEADS
CONV_W = 4
LRU_C = 8.0
W_POOL = MIX_WIDTH - W_LRU
POOL_WINDOWS = (2, 4, 8, 16)
POOL_GROUPS = len(POOL_WINDOWS)
POOL_GD = W_POOL // POOL_GROUPS
POOL_BUF = max(POOL_WINDOWS) - 1
D_FF = 4 * D_MODEL
EPS = 1e-6

kernel_name = "hymba_rglru_pool_decode_step"


def rmsnorm(x, g):
    xf = x.astype(jnp.float32)
    y = xf * lax.rsqrt(jnp.mean(xf * xf, axis=-1, keepdims=True) + EPS)
    return (y * g.astype(jnp.float32)).astype(x.dtype)


def causal_conv(u, buf, w, b):
    T = u.shape[1]
    ext = jnp.concatenate([buf.astype(jnp.float32), u.astype(jnp.float32)], axis=1)
    out = b.astype(jnp.float32)
    for k in range(CONV_W):
        out = out + ext[:, k:k + T] * w[k].astype(jnp.float32)
    return out, ext[:, -(CONV_W - 1):].astype(u.dtype)


def rg_lru(x, h0, pos, w_a, b_a, w_x, b_x, lam):
    B, T, _ = x.shape
    xh = x.reshape(B, T, LRU_HEADS, LRU_HD)
    r = jax.nn.sigmoid(jnp.einsum('bthi,hij->bthj', xh, w_a.astype(jnp.float32)).reshape(B, T, W_LRU) + b_a.astype(jnp.float32))
    i = jax.nn.sigmoid(jnp.einsum('bthi,hij->bthj', xh, w_x.astype(jnp.float32)).reshape(B, T, W_LRU) + b_x.astype(jnp.float32))
    log_a = -LRU_C * r * jax.nn.softplus(-lam.astype(jnp.float32))
    a = jnp.exp(log_a)
    mult = jnp.sqrt(1.0 - jnp.exp(2.0 * log_a))
    mult = jnp.where((pos == 0)[None, :, None], 1.0, mult)
    bx = x * i * mult

    def combine(l, rr):
        a1, b1 = l
        a2, b2 = rr
        return a1 * a2, a2 * b1 + b2

    a_cum, b_cum = lax.associative_scan(combine, (a, bx), axis=1)
    h = a_cum * h0.astype(jnp.float32)[:, None, :] + b_cum
    return h, h[:, -1]


def pool_mix(u, buf, pos, w_pool, scale):
    B, T, _ = u.shape
    ext = jnp.concatenate([buf.astype(jnp.float32), u.astype(jnp.float32)], axis=1)
    cs0 = jnp.concatenate([jnp.zeros((B, 1, W_POOL), jnp.float32), jnp.cumsum(ext, axis=1)], axis=1)
    start = POOL_BUF + 1
    cur = ext[:, POOL_BUF:]
    outs = []
    for g, w in enumerate(POOL_WINDOWS):
        sl = slice(g * POOL_GD, (g + 1) * POOL_GD)
        wsum = cs0[:, start:start + T, sl] - cs0[:, start - w:start - w + T, sl]
        count = jnp.minimum(pos + 1, w).astype(jnp.float32)[None, :, None]
        outs.append(wsum / count - cur[:, :, sl])
    pooled = jnp.stack(outs, axis=2)
    y = jnp.einsum('btgi,gij->btgj', pooled, w_pool.astype(jnp.float32)).reshape(B, T, W_POOL)
    y = y * scale.astype(jnp.float32)
    return y, ext[:, -POOL_BUF:].astype(u.dtype)


def layer(x, lru_h0, conv_buf, pool_buf, pos, norm_mix_g, w_in, conv_w, conv_b, w_rg_a, b_rg_a,
          w_rg_x, b_rg_x, lru_lambda, w_pool, pool_scale, w_out, norm_mlp_g, w_up, w_down):
    h = rmsnorm(x, norm_mix_g)
    proj = jnp.einsum('btd,de->bte', h, w_in)
    u_x = proj[..., :W_LRU]
    u_gate = proj[..., W_LRU:2 * W_LRU]
    u_pool = proj[..., 2 * W_LRU:]
    xc, new_conv = causal_conv(u_x, conv_buf, conv_w, conv_b)
    hs, h_last = rg_lru(xc, lru_h0, pos, w_rg_a, b_rg_a, w_rg_x, b_rg_x, lru_lambda)
    y_lru = hs * jax.nn.gelu(u_gate.astype(jnp.float32), approximate=True)
    y_pool, new_pool = pool_mix(u_pool, pool_buf, pos, w_pool, pool_scale)
    y_mix = jnp.concatenate([y_lru, y_pool], axis=-1).astype(x.dtype)
    x = x + jnp.einsum('bte,ed->btd', y_mix, w_out)
    h2 = rmsnorm(x, norm_mlp_g)
    f = jax.nn.relu(jnp.einsum('btd,df->btf', h2, w_up))
    x = x + jnp.einsum('btf,fd->btd', f * f, w_down)
    return x, h_last.astype(x.dtype), new_conv, new_pool


def setup_inputs(seed: int = 0) -> dict:
    key = jax.random.key(seed)
    ks = jax.random.split(key, 24)
    f32 = jnp.float32
    nrm = lambda k, s, sc: jax.random.normal(k, s, f32) * sc
    a0 = jax.random.uniform(ks[10], (DEPTH, W_LRU), f32, 0.9, 0.999)
    base = a0 ** (1.0 / LRU_C)
    lru_lambda = jnp.log(base) - jnp.log1p(-base)
    return {
        "x_prompt": nrm(ks[0], (BATCH, SEQ, D_MODEL), 1.0),
        "x_sample": nrm(ks[1], (DEC_BATCH, DEC_SEQ, D_MODEL), 1.0),
        "state_lru_h": nrm(ks[2], (DEPTH, DEC_BATCH, W_LRU), 0.5),
        "state_conv": nrm(ks[3], (DEPTH, DEC_BATCH, CONV_W - 1, W_LRU), 1.0),
        "state_pool": nrm(ks[4], (DEPTH, DEC_BATCH, POOL_BUF, W_POOL), 1.0),
        "norm_mix_g": 1.0 + nrm(ks[5], (DEPTH, D_MODEL), 0.02),
        "w_in": nrm(ks[6], (DEPTH, D_MODEL, 2 * W_LRU + W_POOL), D_MODEL ** -0.5),
        "conv_w": nrm(ks[7], (DEPTH, CONV_W, W_LRU), CONV_W ** -0.5),
        "conv_b": nrm(ks[8], (DEPTH, W_LRU), 0.01),
        "w_rg_a": nrm(ks[9], (DEPTH, LRU_HEADS, LRU_HD, LRU_HD), LRU_HD ** -0.5),
        "b_rg_a": nrm(ks[11], (DEPTH, W_LRU), 0.01),
        "w_rg_x": nrm(ks[12], (DEPTH, LRU_HEADS, LRU_HD, LRU_HD), LRU_HD ** -0.5),
        "b_rg_x": nrm(ks[13], (DEPTH, W_LRU), 0.01),
        "lru_lambda": lru_lambda,
        "w_pool": nrm(ks[14], (DEPTH, POOL_GROUPS, POOL_GD, POOL_GD), POOL_GD ** -0.5),
        "pool_scale": 0.5 + nrm(ks[15], (DEPTH, W_POOL), 0.05),
        "w_out": nrm(ks[16], (DEPTH, MIX_WIDTH, D_MODEL), MIX_WIDTH ** -0.5),
        "norm_mlp_g": 1.0 + nrm(ks[17], (DEPTH, D_MODEL), 0.02),
        "w_up": nrm(ks[18], (DEPTH, D_MODEL, D_FF), D_MODEL ** -0.5),
        "w_down": nrm(ks[19], (DEPTH, D_FF, D_MODEL), D_FF ** -0.5),
        "norm_final_g": 1.0 + nrm(ks[20], (D_MODEL,), 0.02),
    }


def reference(x_prompt, x_sample, state_lru_h, state_conv, state_pool, norm_mix_g, w_in, conv_w, conv_b,
              w_rg_a, b_rg_a, w_rg_x, b_rg_x, lru_lambda, w_pool, pool_scale, w_out, norm_mlp_g,
              w_up, w_down, norm_final_g):
    Bp, Tp, _ = x_prompt.shape
    Bs, Ts, _ = x_sample.shape
    pos_p = jnp.arange(Tp, dtype=jnp.int32)
    pos_s = PAST_LEN + jnp.arange(Ts, dtype=jnp.int32)
    xp, xs = x_prompt, x_sample
    hp_l, cp_l, pp_l, hs_l, cs_l, ps_l = [], [], [], [], [], []
    for l in range(DEPTH):
        wts = (norm_mix_g[l], w_in[l], conv_w[l], conv_b[l], w_rg_a[l], b_rg_a[l], w_rg_x[l], b_rg_x[l],
               lru_lambda[l], w_pool[l], pool_scale[l], w_out[l], norm_mlp_g[l], w_up[l], w_down[l])
        h0 = jnp.zeros((Bp, W_LRU), jnp.float32)
        cb0 = jnp.zeros((Bp, CONV_W - 1, W_LRU), xp.dtype)
        pb0 = jnp.zeros((Bp, POOL_BUF, W_POOL), xp.dtype)
        xp, hp, cp, pp = layer(xp, h0, cb0, pb0, pos_p, *wts)
        xs, hs, cs, ps = layer(xs, state_lru_h[l], state_conv[l], state_pool[l], pos_s, *wts)
        hp_l.append(hp); cp_l.append(cp); pp_l.append(pp)
        hs_l.append(hs); cs_l.append(cs); ps_l.append(ps)
    y_prompt = rmsnorm(xp, norm_final_g)
    y_sample = rmsnorm(xs, norm_final_g)
    return (y_prompt, y_sample, jnp.stack(hp_l), jnp.stack(cp_l), jnp.stack(pp_l),
            jnp.stack(hs_l), jnp.stack(cs_l), jnp.stack(ps_l))
```

```python
import functools
import math

import jax
import jax.numpy as jnp
from jax import lax
from jax.experimental import pallas as pl
from jax.experimental.pallas import tpu as pltpu

EPS = 1e-6
LRU_HEADS = 8
LRU_C = 8.0
CONV_W = 4
POOL_WINDOWS = (2, 4, 8, 16)
POOL_BUF = max(POOL_WINDOWS) - 1
PAST_LEN = 16384

SUBLANES = 8
MIB = 1024 * 1024

MIX_TM = 256
MLP_TM = 512
MLP_TF = 1024
MIX_VMEM_LIMIT = 56 * MIB
MLP_VMEM_LIMIT = 48 * MIB

CONV_PAD = SUBLANES
POOL_PAD = 2 * SUBLANES

F32 = jnp.float32
BF16 = jnp.bfloat16


def _rmsnorm(x, g):
    return (x * lax.rsqrt(jnp.mean(x * x, axis=-1, keepdims=True) + EPS)) * g


def _dot(a, b):
    return jnp.dot(a, b, preferred_element_type=F32)


def _in_proj(x, g, w_in_ref, w_lru):
    h = _rmsnorm(x, g).astype(BF16)
    ux = _dot(h, w_in_ref[:, 0:w_lru])
    ug = _dot(h, w_in_ref[:, w_lru:2 * w_lru])
    up = _dot(h, w_in_ref[:, 2 * w_lru:])
    return ux, ug, up


def _lru_gates(xc, wg_ref, ba, bx, cneg):
    hd = xc.shape[1] // LRU_HEADS
    xcb = xc.astype(BF16)
    a_parts, m_parts, xi_parts = [], [], []
    for h in range(LRU_HEADS):
        sl = slice(h * hd, (h + 1) * hd)
        g = _dot(xcb[:, sl], wg_ref[h])
        r = jax.nn.sigmoid(g[:, :hd] + ba[:, sl])
        i = jax.nn.sigmoid(g[:, hd:] + bx[:, sl])
        a = jnp.exp(r * cneg[:, sl])
        a_parts.append(a)
        m_parts.append(jnp.sqrt(1.0 - a * a))
        xi_parts.append(xc[:, sl] * i)
    cat = lambda ps: jnp.concatenate(ps, axis=1)
    return cat(a_parts), cat(m_parts), cat(xi_parts)


def _pool_project(pooled, wp_ref, scale):
    n_g = len(POOL_WINDOWS)
    gd = pooled.shape[1] // n_g
    pb = pooled.astype(BF16)
    outs = [_dot(pb[:, g * gd:(g + 1) * gd], wp_ref[g]) for g in range(n_g)]
    return jnp.concatenate(outs, axis=1) * scale


def _neg_c_softplus(lam):
    return -LRU_C * jax.nn.softplus(-lam)


def _seq_mixer_kernel(x_ref, gmix_ref, w_in_ref, cw_ref, cb_ref, wg_ref, ba_ref, bx_ref,
                      lam_ref, wp_ref, ps_ref, w_out_ref,
                      o_ref, hlast_ref, nconv_ref, npool_ref,
                      extx_ref, extp_ref, a_ref, b_ref, ymix_ref,
                      alast_ref, blast_ref, cin_ref, hc_ref):
    tm = x_ref.shape[0]
    w_lru = a_ref.shape[1]
    n_groups = tm // SUBLANES
    t = pl.program_id(1)

    @pl.when(t == 0)
    def _reset_state():
        hc_ref[...] = jnp.zeros_like(hc_ref)
        extx_ref[0:CONV_PAD, :] = jnp.zeros((CONV_PAD, w_lru), F32)
        extp_ref[0:POOL_PAD, :] = jnp.zeros((POOL_PAD, extp_ref.shape[1]), F32)

    x = x_ref[...]
    ux, ug, up = _in_proj(x, gmix_ref[...], w_in_ref, w_lru)
    extx_ref[CONV_PAD:, :] = ux
    extp_ref[POOL_PAD:, :] = up

    ex = extx_ref[...]
    xc = cb_ref[...] + ex[CONV_PAD:, :] * cw_ref[CONV_W - 1:CONV_W, :]
    for k in range(CONV_W - 1):
        shift = CONV_W - 1 - k
        xc = xc + pltpu.roll(ex, shift, 0)[CONV_PAD:, :] * cw_ref[k:k + 1, :]

    a, mult, xi = _lru_gates(xc, wg_ref, ba_ref[...], bx_ref[...], _neg_c_softplus(lam_ref[...]))
    row = lax.broadcasted_iota(jnp.int32, (tm, 1), 0)
    mult = jnp.where((row == 0) & (t == 0), 1.0, mult)
    a_ref[...] = a
    b_ref[...] = xi * mult

    sub = lax.broadcasted_iota(jnp.int32, (SUBLANES, w_lru), 0)

    def _group_scan(g, carry):
        r0 = pl.multiple_of(g * SUBLANES, SUBLANES)
        av = a_ref[pl.ds(r0, SUBLANES), :]
        bv = b_ref[pl.ds(r0, SUBLANES), :]
        for d in (1, 2, 4):
            keep = sub >= d
            bv = jnp.where(keep, av * pltpu.roll(bv, d, 0) + bv, bv)
            av = jnp.where(keep, av * pltpu.roll(av, d, 0), av)
        a_ref[pl.ds(r0, SUBLANES), :] = av
        b_ref[pl.ds(r0, SUBLANES), :] = bv
        alast_ref[pl.ds(g, 1), :] = av[SUBLANES - 1:SUBLANES, :]
        blast_ref[pl.ds(g, 1), :] = bv[SUBLANES - 1:SUBLANES, :]
        return carry

    lax.fori_loop(0, n_groups, _group_scan, 0, unroll=2)

    def _group_carry(g, c):
        cin_ref[pl.ds(g, 1), :] = c
        return alast_ref[pl.ds(g, 1), :] * c + blast_ref[pl.ds(g, 1), :]

    h_end = lax.fori_loop(0, n_groups, _group_carry, hc_ref[...], unroll=4)
    hc_ref[...] = h_end

    def _group_apply(g, carry):
        r0 = pl.multiple_of(g * SUBLANES, SUBLANES)
        b_ref[pl.ds(r0, SUBLANES), :] = (a_ref[pl.ds(r0, SUBLANES), :] * cin_ref[pl.ds(g, 1), :]
                                         + b_ref[pl.ds(r0, SUBLANES), :])
        return carry

    lax.fori_loop(0, n_groups, _group_apply, 0, unroll=2)

    y_lru = b_ref[...] * jax.nn.gelu(ug, approximate=True)
    ymix_ref[:, 0:w_lru] = y_lru.astype(BF16)

    ep = extp_ref[...]
    w_pool = ep.shape[1]
    gd = w_pool // len(POOL_WINDOWS)
    pos = t * tm + row
    s = ep
    width = 1
    pooled = []
    for g, w in enumerate(POOL_WINDOWS):
        while width < w:
            s = s + pltpu.roll(s, width, 0)
            width *= 2
        inv = 1.0 / jnp.minimum(pos + 1, w).astype(F32)
        pooled.append(s[POOL_PAD:, 0:gd] * inv - ep[POOL_PAD:, g * gd:(g + 1) * gd])
        if g + 1 < len(POOL_WINDOWS):
            s = s[:, gd:]
    y_pool = _pool_project(jnp.concatenate(pooled, axis=1), wp_ref, ps_ref[...])
    ymix_ref[:, w_lru:] = y_pool.astype(BF16)

    o_ref[...] = x + _dot(ymix_ref[...], w_out_ref[...])

    extx_ref[0:CONV_PAD, :] = extx_ref[tm:tm + CONV_PAD, :]
    extp_ref[0:POOL_PAD, :] = extp_ref[tm:tm + POOL_PAD, :]

    @pl.when(t == pl.num_programs(1) - 1)
    def _emit_state():
        hlast_ref[...] = h_end
        nconv_ref[...] = extx_ref[CONV_PAD - (CONV_W - 1):CONV_PAD, :]
        npool_ref[...] = extp_ref[POOL_PAD - POOL_BUF:POOL_PAD, :]


def _const_spec(shape):
    nd = len(shape)
    return pl.BlockSpec(shape, lambda *_: (0,) * nd, pipeline_mode=pl.Buffered(1))


def _seq_mixer(x, p, tm):
    bsz, seq, d = x.shape
    w_lru = p["conv_b"].shape[1]
    w_pool = p["pool_scale"].shape[1]
    weights = (p["norm_mix_g"], p["w_in"], p["conv_w"], p["conv_b"], p["w_gate"], p["b_rg_a"],
               p["b_rg_x"], p["lru_lambda"], p["w_pool"], p["pool_scale"], p["w_out"])
    tok_spec = pl.BlockSpec((None, tm, d), lambda b, t: (b, t, 0))
    state_spec = lambda rows, w: pl.BlockSpec((None, rows, w), lambda b, t: (b, 0, 0))
    n_groups = tm // SUBLANES
    return pl.pallas_call(
        _seq_mixer_kernel,
        grid=(bsz, seq // tm),
        in_specs=[tok_spec] + [_const_spec(w.shape) for w in weights],
        out_specs=[tok_spec, state_spec(1, w_lru), state_spec(CONV_W - 1, w_lru),
                   state_spec(POOL_BUF, w_pool)],
        out_shape=[jax.ShapeDtypeStruct((bsz, seq, d), F32),
                   jax.ShapeDtypeStruct((bsz, 1, w_lru), F32),
                   jax.ShapeDtypeStruct((bsz, CONV_W - 1, w_lru), F32),
                   jax.ShapeDtypeStruct((bsz, POOL_BUF, w_pool), F32)],
        scratch_shapes=[
            pltpu.VMEM((CONV_PAD + tm, w_lru), F32),
            pltpu.VMEM((POOL_PAD + tm, w_pool), F32),
            pltpu.VMEM((tm, w_lru), F32),
            pltpu.VMEM((tm, w_lru), F32),
            pltpu.VMEM((tm, d), BF16),
            pltpu.VMEM((n_groups, w_lru), F32),
            pltpu.VMEM((n_groups, w_lru), F32),
            pltpu.VMEM((n_groups, w_lru), F32),
            pltpu.VMEM((1, w_lru), F32),
        ],
        compiler_params=pltpu.CompilerParams(
            dimension_semantics=("arbitrary", "arbitrary"),
            vmem_limit_bytes=MIX_VMEM_LIMIT),
        name="seq_mixer",
    )(x, *weights)


def _dec_mixer_kernel(x_ref, h0_ref, cbuf_ref, pbuf_ref, gmix_ref, w_in_ref, cw_ref, cb_ref,
                      wg_ref, ba_ref, bx_ref, lam_ref, wp_ref, ps_ref, w_out_ref,
                      o_ref, hnew_ref, nconv_ref, npool_ref):
    w_lru = h0_ref.shape[1]
    w_pool = ps_ref.shape[1]
    x = x_ref[...]
    ux, ug, up = _in_proj(x, gmix_ref[...], w_in_ref, w_lru)

    xc = cb_ref[...] + ux * cw_ref[CONV_W - 1:CONV_W, :]
    for k in range(CONV_W - 1):
        xc = xc + cbuf_ref[:, k * w_lru:(k + 1) * w_lru] * cw_ref[k:k + 1, :]
    nconv_ref[:, 0:(CONV_W - 2) * w_lru] = cbuf_ref[:, w_lru:]
    nconv_ref[:, (CONV_W - 2) * w_lru:] = ux

    a, mult, xi = _lru_gates(xc, wg_ref, ba_ref[...], bx_ref[...], _neg_c_softplus(lam_ref[...]))
    h_new = a * h0_ref[...] + xi * mult
    hnew_ref[...] = h_new
    y_lru = h_new * jax.nn.gelu(ug, approximate=True)

    gd = w_pool // len(POOL_WINDOWS)
    pooled = []
    for g, w in enumerate(POOL_WINDOWS):
        sl = slice(g * gd, (g + 1) * gd)
        cur = up[:, sl]
        s = cur
        for j in range(1, w):
            k = POOL_BUF - j
            s = s + pbuf_ref[:, k * w_pool + g * gd:k * w_pool + (g + 1) * gd]
        count = float(min(PAST_LEN + 1, w))
        pooled.append(s / count - cur)
    y_pool = _pool_project(jnp.concatenate(pooled, axis=1), wp_ref, ps_ref[...])
    npool_ref[:, 0:(POOL_BUF - 1) * w_pool] = pbuf_ref[:, w_pool:]
    npool_ref[:, (POOL_BUF - 1) * w_pool:] = up

    y_mix = jnp.concatenate([y_lru, y_pool], axis=1).astype(BF16)
    o_ref[...] = x + _dot(y_mix, w_out_ref[...])


def _dec_mixer(x, h0, cbuf, pbuf, p):
    rows, d = x.shape
    weights = (p["norm_mix_g"], p["w_in"], p["conv_w"], p["conv_b"], p["w_gate"], p["b_rg_a"],
               p["b_rg_x"], p["lru_lambda"], p["w_pool"], p["pool_scale"], p["w_out"])
    args = (x, h0, cbuf, pbuf) + weights
    return pl.pallas_call(
        _dec_mixer_kernel,
        grid=(1,),
        in_specs=[_const_spec(a.shape) for a in args],
        out_specs=[pl.BlockSpec(s.shape, lambda i: (0, 0)) for s in (x, h0, cbuf, pbuf)],
        out_shape=[jax.ShapeDtypeStruct(s.shape, F32) for s in (x, h0, cbuf, pbuf)],
        compiler_params=pltpu.CompilerParams(
            dimension_semantics=("arbitrary",), vmem_limit_bytes=MIX_VMEM_LIMIT),
        name="dec_mixer",
    )(*args)


def _mlp_kernel(x_ref, g_ref, wup_ref, wdn_ref, gfin_ref, o_ref, h_ref):
    j = pl.program_id(1)

    @pl.when(j == 0)
    def _norm():
        h_ref[...] = _rmsnorm(x_ref[...], g_ref[...]).astype(BF16)

    f = jnp.maximum(_dot(h_ref[...], wup_ref[...]), 0.0)
    part = _dot((f * f).astype(BF16), wdn_ref[...])

    @pl.when(j == 0)
    def _first():
        o_ref[...] = part

    @pl.when(j > 0)
    def _accumulate():
        o_ref[...] += part

    @pl.when(j == pl.num_programs(1) - 1)
    def _finish():
        o_ref[...] = _rmsnorm(x_ref[...] + o_ref[...], gfin_ref[...])


def _mlp(x, p, tm, tf):
    n, d = x.shape
    d_ff = p["w_up"].shape[1]
    vec_spec = pl.BlockSpec((1, d), lambda i, j: (0, 0))
    tok_spec = pl.BlockSpec((tm, d), lambda i, j: (i, 0))
    return pl.pallas_call(
        _mlp_kernel,
        grid=(n // tm, d_ff // tf),
        in_specs=[tok_spec, vec_spec,
                  pl.BlockSpec((d, tf), lambda i, j: (0, j)),
                  pl.BlockSpec((tf, d), lambda i, j: (j, 0)),
                  vec_spec],
        out_specs=tok_spec,
        out_shape=jax.ShapeDtypeStruct((n, d), F32),
        scratch_shapes=[pltpu.VMEM((tm, d), BF16)],
        compiler_params=pltpu.CompilerParams(
            dimension_semantics=("arbitrary", "arbitrary"),
            vmem_limit_bytes=MLP_VMEM_LIMIT),
        name="mlp",
    )(x, p["norm_mlp_g"], p["w_up"], p["w_down"], p["norm_final_g"])


def _layer_params(l, norm_mix_g, w_in, conv_w, conv_b, w_rg_a, b_rg_a, w_rg_x, b_rg_x, lru_lambda,
                  w_pool, pool_scale, w_out, norm_mlp_g, w_up, w_down, norm_final_g):
    row = lambda v: v.reshape(1, -1)
    return {
        "norm_mix_g": row(norm_mix_g[l]),
        "w_in": w_in[l].astype(BF16),
        "conv_w": conv_w[l],
        "conv_b": row(conv_b[l]),
        "w_gate": jnp.concatenate([w_rg_a[l], w_rg_x[l]], axis=-1).astype(BF16),
        "b_rg_a": row(b_rg_a[l]),
        "b_rg_x": row(b_rg_x[l]),
        "lru_lambda": row(lru_lambda[l]),
        "w_pool": w_pool[l].astype(BF16),
        "pool_scale": row(pool_scale[l]),
        "w_out": w_out[l].astype(BF16),
        "norm_mlp_g": row(norm_mlp_g[l]),
        "w_up": w_up[l].astype(BF16),
        "w_down": w_down[l].astype(BF16),
        "norm_final_g": row(norm_final_g),
    }


def kernel(x_prompt, x_sample, state_lru_h, state_conv, state_pool, norm_mix_g, w_in, conv_w, conv_b, w_rg_a, b_rg_a, w_rg_x, b_rg_x, lru_lambda, w_pool, pool_scale, w_out, norm_mlp_g, w_up, w_down, norm_final_g):
    depth = w_in.shape[0]
    assert depth == 1, "final RMSNorm is fused into the (single) layer's MLP kernel"
    bp, tp, d = x_prompt.shape
    bs, ts, _ = x_sample.shape
    assert ts == 1 and tp % MIX_TM == 0 and (bp * tp) % MLP_TM == 0
    p = _layer_params(0, norm_mix_g, w_in, conv_w, conv_b, w_rg_a, b_rg_a, w_rg_x, b_rg_x,
                      lru_lambda, w_pool, pool_scale, w_out, norm_mlp_g, w_up, w_down, norm_final_g)

    x1p, hp, cp, pp = _seq_mixer(x_prompt, p, MIX_TM)
    yp = _mlp(x1p.reshape(bp * tp, d), p, MLP_TM, MLP_TF).reshape(bp, tp, d)

    x1s, hs, cs, ps = _dec_mixer(x_sample.reshape(bs, d), state_lru_h[0],
                                 state_conv[0].reshape(bs, -1), state_pool[0].reshape(bs, -1), p)
    ys = _mlp(x1s, p, bs, MLP_TF).reshape(bs, ts, d)

    w_lru = hs.shape[1]
    w_pl = pool_scale.shape[1]
    return (yp, ys,
            hp.reshape(1, bp, w_lru), cp[None], pp[None],
            hs[None], cs.reshape(1, bs, CONV_W - 1, w_lru), ps.reshape(1, bs, POOL_BUF, w_pl))
```

```python
import functools
import math

import jax
import jax.numpy as jnp
from jax import lax
from jax.experimental import pallas as pl
from jax.experimental.pallas import tpu as pltpu

EPS = 1e-6
LRU_HEADS = 8
LRU_C = 8.0
CONV_W = 4
POOL_WINDOWS = (2, 4, 8, 16)
POOL_BUF = max(POOL_WINDOWS) - 1
PAST_LEN = 16384

SUBLANES = 8
MIB = 1024 * 1024

MIX_TM = 256
MLP_TM = 512
MLP_TF = 1024
MIX_VMEM_LIMIT = 56 * MIB
MLP_VMEM_LIMIT = 48 * MIB

CONV_PAD = SUBLANES
POOL_PAD = 2 * SUBLANES

F32 = jnp.float32
BF16 = jnp.bfloat16


def _rmsnorm(x, g):
    return (x * lax.rsqrt(jnp.mean(x * x, axis=-1, keepdims=True) + EPS)) * g


def _dot(a, b):
    return jnp.dot(a, b, preferred_element_type=F32)


def _in_proj(x, g, w_in_ref, w_lru):
    h = _rmsnorm(x, g).astype(BF16)
    ux = _dot(h, w_in_ref[:, 0:w_lru])
    ug = _dot(h, w_in_ref[:, w_lru:2 * w_lru])
    up = _dot(h, w_in_ref[:, 2 * w_lru:])
    return ux, ug, up


def _lru_gates(xc, wg_ref, ba, bx, cneg):
    hd = xc.shape[1] // LRU_HEADS
    xcb = xc.astype(BF16)
    a_parts, m_parts, xi_parts = [], [], []
    for h in range(LRU_HEADS):
        sl = slice(h * hd, (h + 1) * hd)
        g = _dot(xcb[:, sl], wg_ref[h])
        r = jax.nn.sigmoid(g[:, :hd] + ba[:, sl])
        i = jax.nn.sigmoid(g[:, hd:] + bx[:, sl])
        a = jnp.exp(r * cneg[:, sl])
        a_parts.append(a)
        m_parts.append(jnp.sqrt(1.0 - a * a))
        xi_parts.append(xc[:, sl] * i)
    cat = lambda ps: jnp.concatenate(ps, axis=1)
    return cat(a_parts), cat(m_parts), cat(xi_parts)


def _pool_project(pooled, wp_ref, scale):
    n_g = len(POOL_WINDOWS)
    gd = pooled.shape[1] // n_g
    pb = pooled.astype(BF16)
    outs = [_dot(pb[:, g * gd:(g + 1) * gd], wp_ref[g]) for g in range(n_g)]
    return jnp.concatenate(outs, axis=1) * scale


def _neg_c_softplus(lam):
    return -LRU_C * jax.nn.softplus(-lam)


def _seq_mixer_kernel(x_ref, gmix_ref, w_in_ref, cw_ref, cb_ref, wg_ref, ba_ref, bx_ref,
                      lam_ref, wp_ref, ps_ref, w_out_ref,
                      o_ref, hlast_ref, nconv_ref, npool_ref,
                      extx_ref, extp_ref, a_ref, b_ref, ymix_ref,
                      alast_ref, blast_ref, cin_ref, hc_ref):
    tm = x_ref.shape[0]
    w_lru = a_ref.shape[1]
    n_groups = tm // SUBLANES
    t = pl.program_id(1)

    @pl.when(t == 0)
    def _reset_state():
        hc_ref[...] = jnp.zeros_like(hc_ref)
        extx_ref[0:CONV_PAD, :] = jnp.zeros((CONV_PAD, w_lru), F32)
        extp_ref[0:POOL_PAD, :] = jnp.zeros((POOL_PAD, extp_ref.shape[1]), F32)

    x = x_ref[...]
    ux, ug, up = _in_proj(x, gmix_ref[...], w_in_ref, w_lru)
    extx_ref[CONV_PAD:, :] = ux
    extp_ref[POOL_PAD:, :] = up

    ex = extx_ref[...]
    xc = cb_ref[...] + ex[CONV_PAD:, :] * cw_ref[CONV_W - 1:CONV_W, :]
    for k in range(CONV_W - 1):
        shift = CONV_W - 1 - k
        xc = xc + pltpu.roll(ex, shift, 0)[CONV_PAD:, :] * cw_ref[k:k + 1, :]

    a, mult, xi = _lru_gates(xc, wg_ref, ba_ref[...], bx_ref[...], _neg_c_softplus(lam_ref[...]))
    row = lax.broadcasted_iota(jnp.int32, (tm, 1), 0)
    mult = jnp.where((row == 0) & (t == 0), 1.0, mult)
    a_ref[...] = a
    b_ref[...] = xi * mult

    sub = lax.broadcasted_iota(jnp.int32, (SUBLANES, w_lru), 0)

    def _group_scan(g, carry):
        r0 = pl.multiple_of(g * SUBLANES, SUBLANES)
        av = a_ref[pl.ds(r0, SUBLANES), :]
        bv = b_ref[pl.ds(r0, SUBLANES), :]
        for d in (1, 2, 4):
            keep = sub >= d
            bv = jnp.where(keep, av * pltpu.roll(bv, d, 0) + bv, bv)
            av = jnp.where(keep, av * pltpu.roll(av, d, 0), av)
        a_ref[pl.ds(r0, SUBLANES), :] = av
        b_ref[pl.ds(r0, SUBLANES), :] = bv
        alast_ref[pl.ds(g, 1), :] = av[SUBLANES - 1:SUBLANES, :]
        blast_ref[pl.ds(g, 1), :] = bv[SUBLANES - 1:SUBLANES, :]
        return carry

    lax.fori_loop(0, n_groups, _group_scan, 0, unroll=2)

    def _group_carry(g, c):
        cin_ref[pl.ds(g, 1), :] = c
        return alast_ref[pl.ds(g, 1), :] * c + blast_ref[pl.ds(g, 1), :]

    h_end = lax.fori_loop(0, n_groups, _group_carry, hc_ref[...], unroll=4)
    hc_ref[...] = h_end

    def _group_apply(g, carry):
        r0 = pl.multiple_of(g * SUBLANES, SUBLANES)
        b_ref[pl.ds(r0, SUBLANES), :] = (a_ref[pl.ds(r0, SUBLANES), :] * cin_ref[pl.ds(g, 1), :]
                                         + b_ref[pl.ds(r0, SUBLANES), :])
        return carry

    lax.fori_loop(0, n_groups, _group_apply, 0, unroll=2)

    y_lru = b_ref[...] * jax.nn.gelu(ug, approximate=True)
    ymix_ref[:, 0:w_lru] = y_lru.astype(BF16)

    ep = extp_ref[...]
    w_pool = ep.shape[1]
    gd = w_pool // len(POOL_WINDOWS)
    pos = t * tm + row
    s = ep
    width = 1
    pooled = []
    for g, w in enumerate(POOL_WINDOWS):
        while width < w:
            s = s + pltpu.roll(s, width, 0)
            width *= 2
        inv = 1.0 / jnp.minimum(pos + 1, w).astype(F32)
        pooled.append(s[POOL_PAD:, 0:gd] * inv - ep[POOL_PAD:, g * gd:(g + 1) * gd])
        if g + 1 < len(POOL_WINDOWS):
            s = s[:, gd:]
    y_pool = _pool_project(jnp.concatenate(pooled, axis=1), wp_ref, ps_ref[...])
    ymix_ref[:, w_lru:] = y_pool.astype(BF16)

    o_ref[...] = x + _dot(ymix_ref[...], w_out_ref[...])

    extx_ref[0:CONV_PAD, :] = extx_ref[tm:tm + CONV_PAD, :]
    extp_ref[0:POOL_PAD, :] = extp_ref[tm:tm + POOL_PAD, :]

    @pl.when(t == pl.num_programs(1) - 1)
    def _emit_state():
        hlast_ref[...] = h_end
        nconv_ref[...] = extx_ref[CONV_PAD - (CONV_W - 1):CONV_PAD, :]
        npool_ref[...] = extp_ref[POOL_PAD - POOL_BUF:POOL_PAD, :]


def _const_spec(shape):
    nd = len(shape)
    return pl.BlockSpec(shape, lambda *_: (0,) * nd, pipeline_mode=pl.Buffered(1))


def _seq_mixer(x, p, tm):
    bsz, seq, d = x.shape
    w_lru = p["conv_b"].shape[1]
    w_pool = p["pool_scale"].shape[1]
    weights = (p["norm_mix_g"], p["w_in"], p["conv_w"], p["conv_b"], p["w_gate"], p["b_rg_a"],
               p["b_rg_x"], p["lru_lambda"], p["w_pool"], p["pool_scale"], p["w_out"])
    tok_spec = pl.BlockSpec((None, tm, d), lambda b, t: (b, t, 0))
    state_spec = lambda rows, w: pl.BlockSpec((None, rows, w), lambda b, t: (b, 0, 0))
    n_groups = tm // SUBLANES
    return pl.pallas_call(
        _seq_mixer_kernel,
        grid=(bsz, seq // tm),
        in_specs=[tok_spec] + [_const_spec(w.shape) for w in weights],
        out_specs=[tok_spec, state_spec(1, w_lru), state_spec(CONV_W - 1, w_lru),
                   state_spec(POOL_BUF, w_pool)],
        out_shape=[jax.ShapeDtypeStruct((bsz, seq, d), F32),
                   jax.ShapeDtypeStruct((bsz, 1, w_lru), F32),
                   jax.ShapeDtypeStruct((bsz, CONV_W - 1, w_lru), F32),
                   jax.ShapeDtypeStruct((bsz, POOL_BUF, w_pool), F32)],
        scratch_shapes=[
            pltpu.VMEM((CONV_PAD + tm, w_lru), F32),
            pltpu.VMEM((POOL_PAD + tm, w_pool), F32),
            pltpu.VMEM((tm, w_lru), F32),
            pltpu.VMEM((tm, w_lru), F32),
            pltpu.VMEM((tm, d), BF16),
            pltpu.VMEM((n_groups, w_lru), F32),
            pltpu.VMEM((n_groups, w_lru), F32),
            pltpu.VMEM((n_groups, w_lru), F32),
            pltpu.VMEM((1, w_lru), F32),
        ],
        compiler_params=pltpu.CompilerParams(
            dimension_semantics=("arbitrary", "arbitrary"),
            vmem_limit_bytes=MIX_VMEM_LIMIT),
        name="seq_mixer",
    )(x, *weights)


def _dec_mixer_kernel(x_ref, h0_ref, cbuf_ref, pbuf_ref, gmix_ref, w_in_ref, cw_ref, cb_ref,
                      wg_ref, ba_ref, bx_ref, lam_ref, wp_ref, ps_ref, w_out_ref,
                      o_ref, hnew_ref, nconv_ref, npool_ref):
    w_lru = h0_ref.shape[1]
    w_pool = ps_ref.shape[1]
    x = x_ref[...]
    ux, ug, up = _in_proj(x, gmix_ref[...], w_in_ref, w_lru)

    xc = cb_ref[...] + ux * cw_ref[CONV_W - 1:CONV_W, :]
    for k in range(CONV_W - 1):
        xc = xc + cbuf_ref[:, k * w_lru:(k + 1) * w_lru] * cw_ref[k:k + 1, :]
    nconv_ref[:, 0:(CONV_W - 2) * w_lru] = cbuf_ref[:, w_lru:]
    nconv_ref[:, (CONV_W - 2) * w_lru:] = ux

    a, mult, xi = _lru_gates(xc, wg_ref, ba_ref[...], bx_ref[...], _neg_c_softplus(lam_ref[...]))
    h_new = a * h0_ref[...] + xi * mult
    hnew_ref[...] = h_new
    y_lru = h_new * jax.nn.gelu(ug, approximate=True)

    gd = w_pool // len(POOL_WINDOWS)
    pooled = []
    for g, w in enumerate(POOL_WINDOWS):
        sl = slice(g * gd, (g + 1) * gd)
        cur = up[:, sl]
        s = cur
        for j in range(1, w):
            k = POOL_BUF - j
            s = s + pbuf_ref[:, k * w_pool + g * gd:k * w_pool + (g + 1) * gd]
        count = float(min(PAST_LEN + 1, w))
        pooled.append(s / count - cur)
    y_pool = _pool_project(jnp.concatenate(pooled, axis=1), wp_ref, ps_ref[...])
    npool_ref[:, 0:(POOL_BUF - 1) * w_pool] = pbuf_ref[:, w_pool:]
    npool_ref[:, (POOL_BUF - 1) * w_pool:] = up

    y_mix = jnp.concatenate([y_lru, y_pool], axis=1).astype(BF16)
    o_ref[...] = x + _dot(y_mix, w_out_ref[...])


def _dec_mixer(x, h0, cbuf, pbuf, p):
    rows, d = x.shape
    weights = (p["norm_mix_g"], p["w_in"], p["conv_w"], p["conv_b"], p["w_gate"], p["b_rg_a"],
               p["b_rg_x"], p["lru_lambda"], p["w_pool"], p["pool_scale"], p["w_out"])
    args = (x, h0, cbuf, pbuf) + weights
    return pl.pallas_call(
        _dec_mixer_kernel,
        grid=(1,),
        in_specs=[_const_spec(a.shape) for a in args],
        out_specs=[pl.BlockSpec(s.shape, lambda i: (0, 0)) for s in (x, h0, cbuf, pbuf)],
        out_shape=[jax.ShapeDtypeStruct(s.shape, F32) for s in (x, h0, cbuf, pbuf)],
        compiler_params=pltpu.CompilerParams(
            dimension_semantics=("arbitrary",), vmem_limit_bytes=MIX_VMEM_LIMIT),
        name="dec_mixer",
    )(*args)


def _mlp_kernel(x_ref, g_ref, wup_ref, wdn_ref, gfin_ref, o_ref, h_ref):
    j = pl.program_id(1)

    @pl.when(j == 0)
    def _norm():
        x = x_ref[...]
        h_ref[...] = _rmsnorm(x, g_ref[...]).astype(BF16)
        o_ref[...] = x

    f = jnp.maximum(_dot(h_ref[...], wup_ref[...]), 0.0)
    o_ref[...] += _dot((f * f).astype(BF16), wdn_ref[...])

    @pl.when(j == pl.num_programs(1) - 1)
    def _finish():
        o_ref[...] = _rmsnorm(o_ref[...], gfin_ref[...])


def _mlp(x, p, tm, tf):
    n, d = x.shape
    d_ff = p["w_up"].shape[1]
    vec_spec = pl.BlockSpec((1, d), lambda i, j: (0, 0))
    tok_spec = pl.BlockSpec((tm, d), lambda i, j: (i, 0))
    return pl.pallas_call(
        _mlp_kernel,
        grid=(n // tm, d_ff // tf),
        in_specs=[tok_spec, vec_spec,
                  pl.BlockSpec((d, tf), lambda i, j: (0, j)),
                  pl.BlockSpec((tf, d), lambda i, j: (j, 0)),
                  vec_spec],
        out_specs=tok_spec,
        out_shape=jax.ShapeDtypeStruct((n, d), F32),
        scratch_shapes=[pltpu.VMEM((tm, d), BF16)],
        compiler_params=pltpu.CompilerParams(
            dimension_semantics=("arbitrary", "arbitrary"),
            vmem_limit_bytes=MLP_VMEM_LIMIT),
        name="mlp",
    )(x, p["norm_mlp_g"], p["w_up"], p["w_down"], p["norm_final_g"])


def _layer_params(l, norm_mix_g, w_in, conv_w, conv_b, w_rg_a, b_rg_a, w_rg_x, b_rg_x, lru_lambda,
                  w_pool, pool_scale, w_out, norm_mlp_g, w_up, w_down, norm_final_g):
    row = lambda v: v.reshape(1, -1)
    return {
        "norm_mix_g": row(norm_mix_g[l]),
        "w_in": w_in[l].astype(BF16),
        "conv_w": conv_w[l],
        "conv_b": row(conv_b[l]),
        "w_gate": jnp.concatenate([w_rg_a[l], w_rg_x[l]], axis=-1).astype(BF16),
        "b_rg_a": row(b_rg_a[l]),
        "b_rg_x": row(b_rg_x[l]),
        "lru_lambda": row(lru_lambda[l]),
        "w_pool": w_pool[l].astype(BF16),
        "pool_scale": row(pool_scale[l]),
        "w_out": w_out[l].astype(BF16),
        "norm_mlp_g": row(norm_mlp_g[l]),
        "w_up": w_up[l].astype(BF16),
        "w_down": w_down[l].astype(BF16),
        "norm_final_g": row(norm_final_g),
    }


def kernel(x_prompt, x_sample, state_lru_h, state_conv, state_pool, norm_mix_g, w_in, conv_w, conv_b, w_rg_a, b_rg_a, w_rg_x, b_rg_x, lru_lambda, w_pool, pool_scale, w_out, norm_mlp_g, w_up, w_down, norm_final_g):
    depth = w_in.shape[0]
    assert depth == 1, "final RMSNorm is fused into the (single) layer's MLP kernel"
    bp, tp, d = x_prompt.shape
    bs, ts, _ = x_sample.shape
    assert ts == 1 and tp % MIX_TM == 0 and (bp * tp) % MLP_TM == 0
    p = _layer_params(0, norm_mix_g, w_in, conv_w, conv_b, w_rg_a, b_rg_a, w_rg_x, b_rg_x,
                      lru_lambda, w_pool, pool_scale, w_out, norm_mlp_g, w_up, w_down, norm_final_g)

    x1p, hp, cp, pp = _seq_mixer(x_prompt, p, MIX_TM)
    yp = _mlp(x1p.reshape(bp * tp, d), p, MLP_TM, MLP_TF).reshape(bp, tp, d)

    x1s, hs, cs, ps = _dec_mixer(x_sample.reshape(bs, d), state_lru_h[0],
                                 state_conv[0].reshape(bs, -1), state_pool[0].reshape(bs, -1), p)
    ys = _mlp(x1s, p, bs, MLP_TF).reshape(bs, ts, d)

    w_lru = hs.shape[1]
    w_pl = pool_scale.shape[1]
    return (yp, ys,
            hp.reshape(1, bp, w_lru), cp[None], pp[None],
            hs[None], cs.reshape(1, bs, CONV_W - 1, w_lru), ps.reshape(1, bs, POOL_BUF, w_pl))
```

```python
import functools
import math

import jax
import jax.numpy as jnp
from jax import lax
from jax.experimental import pallas as pl
from jax.experimental.pallas import tpu as pltpu

EPS = 1e-6
LRU_HEADS = 8
LRU_C = 8.0
CONV_W = 4
POOL_WINDOWS = (2, 4, 8, 16)
POOL_BUF = max(POOL_WINDOWS) - 1
PAST_LEN = 16384

SUBLANES = 8
MIB = 1024 * 1024

MIX_TM = 256
MLP_TM = 512
MLP_TF = 1024
MIX_VMEM_LIMIT = 56 * MIB
MLP_VMEM_LIMIT = 48 * MIB

CONV_PAD = SUBLANES
POOL_PAD = 2 * SUBLANES

F32 = jnp.float32
BF16 = jnp.bfloat16


def _rmsnorm(x, g):
    return (x * lax.rsqrt(jnp.mean(x * x, axis=-1, keepdims=True) + EPS)) * g


def _dot(a, b):
    return jnp.dot(a, b, preferred_element_type=F32)


def _in_proj(x, g, w_in_ref, w_lru):
    h = _rmsnorm(x, g).astype(BF16)
    ux = _dot(h, w_in_ref[:, 0:w_lru])
    ug = _dot(h, w_in_ref[:, w_lru:2 * w_lru])
    up = _dot(h, w_in_ref[:, 2 * w_lru:])
    return ux, ug, up


def _lru_gates(xc, wg_ref, ba, bx, cneg):
    hd = xc.shape[1] // LRU_HEADS
    xcb = xc.astype(BF16)
    a_parts, m_parts, xi_parts = [], [], []
    for h in range(LRU_HEADS):
        sl = slice(h * hd, (h + 1) * hd)
        g = _dot(xcb[:, sl], wg_ref[h])
        r = jax.nn.sigmoid(g[:, :hd] + ba[:, sl])
        i = jax.nn.sigmoid(g[:, hd:] + bx[:, sl])
        a = jnp.exp(r * cneg[:, sl])
        a_parts.append(a)
        y = 1.0 - a * a
        m_parts.append(jnp.where(y > 0.0, y * lax.rsqrt(y), 0.0))
        xi_parts.append(xc[:, sl] * i)
    cat = lambda ps: jnp.concatenate(ps, axis=1)
    return cat(a_parts), cat(m_parts), cat(xi_parts)


def _pool_project(pooled, wp_ref, scale):
    n_g = len(POOL_WINDOWS)
    gd = pooled.shape[1] // n_g
    pb = pooled.astype(BF16)
    outs = [_dot(pb[:, g * gd:(g + 1) * gd], wp_ref[g]) for g in range(n_g)]
    return jnp.concatenate(outs, axis=1) * scale


def _neg_c_softplus(lam):
    return -LRU_C * jax.nn.softplus(-lam)


def _seq_mixer_kernel(x_ref, gmix_ref, w_in_ref, cw_ref, cb_ref, wg_ref, ba_ref, bx_ref,
                      lam_ref, wp_ref, ps_ref, w_out_ref, wup32_ref, wdn32_ref,
                      o_ref, hlast_ref, nconv_ref, npool_ref, wup16_ref, wdn16_ref,
                      extx_ref, extp_ref, a_ref, b_ref, ymix_ref,
                      alast_ref, blast_ref, cin_ref, hc_ref):
    tm = x_ref.shape[0]
    w_lru = a_ref.shape[1]
    n_groups = tm // SUBLANES
    t = pl.program_id(1)

    wup16_ref[...] = wup32_ref[...].astype(BF16)
    wdn16_ref[...] = wdn32_ref[...].astype(BF16)

    @pl.when(t == 0)
    def _reset_state():
        hc_ref[...] = jnp.zeros_like(hc_ref)
        extx_ref[0:CONV_PAD, :] = jnp.zeros((CONV_PAD, w_lru), F32)
        extp_ref[0:POOL_PAD, :] = jnp.zeros((POOL_PAD, extp_ref.shape[1]), F32)

    x = x_ref[...]
    ux, ug, up = _in_proj(x, gmix_ref[...], w_in_ref, w_lru)
    extx_ref[CONV_PAD:, :] = ux
    extp_ref[POOL_PAD:, :] = up

    ex = extx_ref[...]
    xc = cb_ref[...] + ex[CONV_PAD:, :] * cw_ref[CONV_W - 1:CONV_W, :]
    for k in range(CONV_W - 1):
        shift = CONV_W - 1 - k
        xc = xc + pltpu.roll(ex, shift, 0)[CONV_PAD:, :] * cw_ref[k:k + 1, :]

    a, mult, xi = _lru_gates(xc, wg_ref, ba_ref[...], bx_ref[...], _neg_c_softplus(lam_ref[...]))
    row = lax.broadcasted_iota(jnp.int32, (tm, 1), 0)
    mult = jnp.where((row == 0) & (t == 0), 1.0, mult)
    a_ref[...] = a
    b_ref[...] = xi * mult

    sub = lax.broadcasted_iota(jnp.int32, (SUBLANES, w_lru), 0)

    def _group_scan(g, carry):
        r0 = g * SUBLANES
        av = a_ref[pl.ds(r0, SUBLANES), :]
        bv = b_ref[pl.ds(r0, SUBLANES), :]
        for d in (1, 2, 4):
            keep = sub >= d
            bv = jnp.where(keep, av * pltpu.roll(bv, d, 0) + bv, bv)
            av = jnp.where(keep, av * pltpu.roll(av, d, 0), av)
        a_ref[pl.ds(r0, SUBLANES), :] = av
        b_ref[pl.ds(r0, SUBLANES), :] = bv
        alast_ref[pl.ds(g, 1), :] = av[SUBLANES - 1:SUBLANES, :]
        blast_ref[pl.ds(g, 1), :] = bv[SUBLANES - 1:SUBLANES, :]
        return carry

    for g in range(n_groups):
        _group_scan(g, 0)

    def _group_carry(g, c):
        cin_ref[pl.ds(g, 1), :] = c
        return alast_ref[pl.ds(g, 1), :] * c + blast_ref[pl.ds(g, 1), :]

    h_end = hc_ref[...]
    for g in range(n_groups):
        h_end = _group_carry(g, h_end)
    hc_ref[...] = h_end

    def _group_apply(g, carry):
        r0 = g * SUBLANES
        b_ref[pl.ds(r0, SUBLANES), :] = (a_ref[pl.ds(r0, SUBLANES), :] * cin_ref[pl.ds(g, 1), :]
                                         + b_ref[pl.ds(r0, SUBLANES), :])
        return carry

    for g in range(n_groups):
        _group_apply(g, 0)

    y_lru = b_ref[...] * jax.nn.gelu(ug, approximate=True)
    ymix_ref[:, 0:w_lru] = y_lru.astype(BF16)

    ep = extp_ref[...]
    w_pool = ep.shape[1]
    gd = w_pool // len(POOL_WINDOWS)
    pos = t * tm + row
    s = ep
    width = 1
    pooled = []
    for g, w in enumerate(POOL_WINDOWS):
        while width < w:
            s = s + pltpu.roll(s, width, 0)
            width *= 2
        inv = 1.0 / jnp.minimum(pos + 1, w).astype(F32)
        pooled.append(s[POOL_PAD:, 0:gd] * inv - ep[POOL_PAD:, g * gd:(g + 1) * gd])
        if g + 1 < len(POOL_WINDOWS):
            s = s[:, gd:]
    y_pool = _pool_project(jnp.concatenate(pooled, axis=1), wp_ref, ps_ref[...])
    ymix_ref[:, w_lru:] = y_pool.astype(BF16)

    o_ref[...] = x + _dot(ymix_ref[...], w_out_ref[...])

    extx_ref[0:CONV_PAD, :] = extx_ref[tm:tm + CONV_PAD, :]
    extp_ref[0:POOL_PAD, :] = extp_ref[tm:tm + POOL_PAD, :]

    @pl.when(t == pl.num_programs(1) - 1)
    def _emit_state():
        hlast_ref[...] = h_end
        nconv_ref[...] = extx_ref[CONV_PAD - (CONV_W - 1):CONV_PAD, :]
        npool_ref[...] = extp_ref[POOL_PAD - POOL_BUF:POOL_PAD, :]


def _const_spec(shape):
    nd = len(shape)
    return pl.BlockSpec(shape, lambda *_: (0,) * nd, pipeline_mode=pl.Buffered(1))


def _seq_mixer(x, p, w_up, w_down, tm):
    bsz, seq, d = x.shape
    w_lru = p["conv_b"].shape[1]
    w_pool = p["pool_scale"].shape[1]
    weights = (p["norm_mix_g"], p["w_in"], p["conv_w"], p["conv_b"], p["w_gate"], p["b_rg_a"],
               p["b_rg_x"], p["lru_lambda"], p["w_pool"], p["pool_scale"], p["w_out"])
    n_t = seq // tm
    n_steps = bsz * n_t
    tok_spec = pl.BlockSpec((None, tm, d), lambda b, t: (b, t, 0))
    state_spec = lambda rows, w: pl.BlockSpec((None, rows, w), lambda b, t: (b, 0, 0))
    slab_spec = lambda w: pl.BlockSpec((w.shape[0] // n_steps, w.shape[1]),
                                       lambda b, t: (b * n_t + t, 0))
    n_groups = tm // SUBLANES
    return pl.pallas_call(
        _seq_mixer_kernel,
        grid=(bsz, n_t),
        in_specs=([tok_spec] + [_const_spec(w.shape) for w in weights]
                  + [slab_spec(w_up), slab_spec(w_down)]),
        out_specs=[tok_spec, state_spec(1, w_lru), state_spec(CONV_W - 1, w_lru),
                   state_spec(POOL_BUF, w_pool), slab_spec(w_up), slab_spec(w_down)],
        out_shape=[jax.ShapeDtypeStruct((bsz, seq, d), F32),
                   jax.ShapeDtypeStruct((bsz, 1, w_lru), F32),
                   jax.ShapeDtypeStruct((bsz, CONV_W - 1, w_lru), F32),
                   jax.ShapeDtypeStruct((bsz, POOL_BUF, w_pool), F32),
                   jax.ShapeDtypeStruct(w_up.shape, BF16),
                   jax.ShapeDtypeStruct(w_down.shape, BF16)],
        scratch_shapes=[
            pltpu.VMEM((CONV_PAD + tm, w_lru), F32),
            pltpu.VMEM((POOL_PAD + tm, w_pool), F32),
            pltpu.VMEM((tm, w_lru), F32),
            pltpu.VMEM((tm, w_lru), F32),
            pltpu.VMEM((tm, d), BF16),
            pltpu.VMEM((n_groups, w_lru), F32),
            pltpu.VMEM((n_groups, w_lru), F32),
            pltpu.VMEM((n_groups, w_lru), F32),
            pltpu.VMEM((1, w_lru), F32),
        ],
        compiler_params=pltpu.CompilerParams(
            dimension_semantics=("arbitrary", "arbitrary"),
            vmem_limit_bytes=MIX_VMEM_LIMIT),
        name="seq_mixer",
    )(x, *weights, w_up, w_down)


def _dec_mixer_kernel(x_ref, h0_ref, cbuf_ref, pbuf_ref, gmix_ref, w_in_ref, cw_ref, cb_ref,
                      wg_ref, ba_ref, bx_ref, lam_ref, wp_ref, ps_ref, w_out_ref,
                      o_ref, hnew_ref, nconv_ref, npool_ref):
    w_lru = h0_ref.shape[1]
    w_pool = ps_ref.shape[1]
    x = x_ref[...]
    ux, ug, up = _in_proj(x, gmix_ref[...], w_in_ref, w_lru)

    xc = cb_ref[...] + ux * cw_ref[CONV_W - 1:CONV_W, :]
    for k in range(CONV_W - 1):
        tap = cbuf_ref[:, k, :]
        xc = xc + tap * cw_ref[k:k + 1, :]
        if k > 0:
            nconv_ref[:, k - 1, :] = tap
    nconv_ref[:, CONV_W - 2, :] = ux

    a, mult, xi = _lru_gates(xc, wg_ref, ba_ref[...], bx_ref[...], _neg_c_softplus(lam_ref[...]))
    h_new = a * h0_ref[...] + xi * mult
    hnew_ref[...] = h_new
    y_lru = h_new * jax.nn.gelu(ug, approximate=True)

    gd = w_pool // len(POOL_WINDOWS)
    s = up
    width = 1
    pooled = []
    for g, w in enumerate(POOL_WINDOWS):
        while width < w:
            k = POOL_BUF - width
            hist = pbuf_ref[:, k, :]
            if k > 0:
                npool_ref[:, k - 1, :] = hist
            s = s + hist[:, g * gd:]
            width += 1
        count = float(min(PAST_LEN + 1, w))
        pooled.append(s[:, 0:gd] / count - up[:, g * gd:(g + 1) * gd])
        if g + 1 < len(POOL_WINDOWS):
            s = s[:, gd:]
    npool_ref[:, POOL_BUF - 1, :] = up
    y_pool = _pool_project(jnp.concatenate(pooled, axis=1), wp_ref, ps_ref[...])

    y_mix = jnp.concatenate([y_lru, y_pool], axis=1).astype(BF16)
    o_ref[...] = x + _dot(y_mix, w_out_ref[...])


def _dec_mixer(x, h0, cbuf, pbuf, p):
    rows, d = x.shape
    weights = (p["norm_mix_g"], p["w_in"], p["conv_w"], p["conv_b"], p["w_gate"], p["b_rg_a"],
               p["b_rg_x"], p["lru_lambda"], p["w_pool"], p["pool_scale"], p["w_out"])
    args = (x, h0, cbuf, pbuf) + weights
    return pl.pallas_call(
        _dec_mixer_kernel,
        grid=(1,),
        in_specs=[_const_spec(a.shape) for a in args],
        out_specs=[pl.BlockSpec(s.shape, functools.partial(lambda nd, i: (0,) * nd, s.ndim))
                   for s in (x, h0, cbuf, pbuf)],
        out_shape=[jax.ShapeDtypeStruct(s.shape, F32) for s in (x, h0, cbuf, pbuf)],
        compiler_params=pltpu.CompilerParams(
            dimension_semantics=("arbitrary",), vmem_limit_bytes=MIX_VMEM_LIMIT),
        name="dec_mixer",
    )(*args)


def _mlp_kernel(x_ref, g_ref, wup_ref, wdn_ref, gfin_ref, o_ref, h_ref):
    j = pl.program_id(1)

    @pl.when(j == 0)
    def _norm():
        x = x_ref[...]
        h_ref[...] = _rmsnorm(x, g_ref[...]).astype(BF16)
        o_ref[...] = x

    f = jnp.maximum(_dot(h_ref[...], wup_ref[...]), 0.0)
    o_ref[...] += _dot((f * f).astype(BF16), wdn_ref[...])

    @pl.when(j == pl.num_programs(1) - 1)
    def _finish():
        o_ref[...] = _rmsnorm(o_ref[...], gfin_ref[...])


def _mlp(x, p, tm, tf):
    n, d = x.shape
    d_ff = p["w_up"].shape[1]
    vec_spec = pl.BlockSpec((1, d), lambda i, j: (0, 0))
    tok_spec = pl.BlockSpec((tm, d), lambda i, j: (i, 0))
    return pl.pallas_call(
        _mlp_kernel,
        grid=(n // tm, d_ff // tf),
        in_specs=[tok_spec, vec_spec,
                  pl.BlockSpec((d, tf), lambda i, j: (0, j)),
                  pl.BlockSpec((tf, d), lambda i, j: (j, 0)),
                  vec_spec],
        out_specs=tok_spec,
        out_shape=jax.ShapeDtypeStruct((n, d), F32),
        scratch_shapes=[pltpu.VMEM((tm, d), BF16)],
        compiler_params=pltpu.CompilerParams(
            dimension_semantics=("arbitrary", "arbitrary"),
            vmem_limit_bytes=MLP_VMEM_LIMIT),
        name="mlp",
    )(x, p["norm_mlp_g"], p["w_up"], p["w_down"], p["norm_final_g"])


def _layer_params(l, norm_mix_g, w_in, conv_w, conv_b, w_rg_a, b_rg_a, w_rg_x, b_rg_x, lru_lambda,
                  w_pool, pool_scale, w_out, norm_mlp_g, w_up, w_down, norm_final_g):
    row = lambda v: v.reshape(1, -1)
    return {
        "norm_mix_g": row(norm_mix_g[l]),
        "w_in": w_in[l].astype(BF16),
        "conv_w": conv_w[l],
        "conv_b": row(conv_b[l]),
        "w_gate": jnp.concatenate([w_rg_a[l], w_rg_x[l]], axis=-1).astype(BF16),
        "b_rg_a": row(b_rg_a[l]),
        "b_rg_x": row(b_rg_x[l]),
        "lru_lambda": row(lru_lambda[l]),
        "w_pool": w_pool[l].astype(BF16),
        "pool_scale": row(pool_scale[l]),
        "w_out": w_out[l].astype(BF16),
        "norm_mlp_g": row(norm_mlp_g[l]),
        "w_up": w_up[l],
        "w_down": w_down[l],
        "norm_final_g": row(norm_final_g),
    }


def kernel(x_prompt, x_sample, state_lru_h, state_conv, state_pool, norm_mix_g, w_in, conv_w, conv_b, w_rg_a, b_rg_a, w_rg_x, b_rg_x, lru_lambda, w_pool, pool_scale, w_out, norm_mlp_g, w_up, w_down, norm_final_g):
    depth = w_in.shape[0]
    assert depth == 1, "final RMSNorm is fused into the (single) layer's MLP kernel"
    bp, tp, d = x_prompt.shape
    bs, ts, _ = x_sample.shape
    assert ts == 1 and tp % MIX_TM == 0 and (bp * tp) % MLP_TM == 0
    p = _layer_params(0, norm_mix_g, w_in, conv_w, conv_b, w_rg_a, b_rg_a, w_rg_x, b_rg_x,
                      lru_lambda, w_pool, pool_scale, w_out, norm_mlp_g, w_up, w_down, norm_final_g)

    x1p, hp, cp, pp, p["w_up"], p["w_down"] = _seq_mixer(x_prompt, p, p["w_up"], p["w_down"], MIX_TM)
    yp = _mlp(x1p.reshape(bp * tp, d), p, MLP_TM, MLP_TF).reshape(bp, tp, d)

    x1s, hs, cs, ps = _dec_mixer(x_sample.reshape(bs, d), state_lru_h[0], state_conv[0],
                                 state_pool[0], p)
    ys = _mlp(x1s, p, bs, MLP_TF).reshape(bs, ts, d)

    return (yp, ys, hp.reshape(1, bp, -1), cp[None], pp[None], hs[None], cs[None], ps[None])
```

```python
import functools

import jax
import jax.numpy as jnp
from jax import lax
from jax.experimental import pallas as pl
from jax.experimental.pallas import tpu as pltpu

EPS = 1e-6
LRU_HEADS = 8
LRU_C = 8.0
CONV_W = 4
POOL_WINDOWS = (2, 4, 8, 16)
POOL_BUF = max(POOL_WINDOWS) - 1
PAST_LEN = 16384

SUBLANES = 8
MIB = 1024 * 1024

MIX_TM = 256
MLP_TM = 1024
MLP_TF = 1024
MIX_VMEM_LIMIT = 60 * MIB
MLP_VMEM_LIMIT = 56 * MIB

F32 = jnp.float32
BF16 = jnp.bfloat16


def _rmsnorm(x, g):
    return (x * lax.rsqrt(jnp.mean(x * x, axis=-1, keepdims=True) + EPS)) * g


def _dot(a, b):
    return jnp.dot(a, b, preferred_element_type=F32)


def _in_proj(x, g, w_in_ref, w_lru):
    h = _rmsnorm(x, g).astype(BF16)
    ux = _dot(h, w_in_ref[:, 0:w_lru])
    ug = _dot(h, w_in_ref[:, w_lru:2 * w_lru])
    up = _dot(h, w_in_ref[:, 2 * w_lru:])
    return ux, ug, up


def _gate_update(g, xc, ba, bx, cneg):
    hd = xc.shape[1]
    r = jax.nn.sigmoid(g[:, :hd] + ba)
    i = jax.nn.sigmoid(g[:, hd:] + bx)
    a = jnp.exp(r * cneg)
    y = 1.0 - a * a
    mult = jnp.where(y > 0.0, y * lax.rsqrt(y), 0.0)
    return a, xc * i, mult


def _lru_gates(xc, wg_ref, ba, bx, cneg):
    hd = xc.shape[1] // LRU_HEADS
    xcb = xc.astype(BF16)
    a_parts, m_parts, xi_parts = [], [], []
    for h in range(LRU_HEADS):
        sl = slice(h * hd, (h + 1) * hd)
        g = _dot(xcb[:, sl], wg_ref[h])
        a, xi, mult = _gate_update(g, xc[:, sl], ba[:, sl], bx[:, sl], cneg[:, sl])
        a_parts.append(a)
        m_parts.append(mult)
        xi_parts.append(xi)
    cat = lambda ps: jnp.concatenate(ps, axis=1)
    return cat(a_parts), cat(m_parts), cat(xi_parts)


def _pool_project(pooled, wp_ref, scale):
    n_g = len(POOL_WINDOWS)
    gd = pooled.shape[1] // n_g
    pb = pooled.astype(BF16)
    outs = [_dot(pb[:, g * gd:(g + 1) * gd], wp_ref[g]) for g in range(n_g)]
    return jnp.concatenate(outs, axis=1) * scale


def _neg_c_softplus(lam):
    return -LRU_C * jax.nn.softplus(-lam)


CONV_HALO = CONV_W - 1
POOL_HALO = POOL_BUF


def _rows(k, n=1):
    return pl.ds(k * SUBLANES, n * SUBLANES)


def _tile_copies(hbm, buf, sem, b, t, slot, n_grp, to_hbm):
    tm = n_grp * SUBLANES
    copies = []
    for s in range(SUBLANES):
        far = hbm.at[b, pl.ds(t * tm + s * n_grp, n_grp), :]
        near = buf.at[slot, :, s, :]
        src, dst = (near, far) if to_hbm else (far, near)
        copies.append(pltpu.make_async_copy(src, dst, sem.at[slot]))
    return copies


def _fill_halo(ext_ref, carry_ref, halo, n_grp):
    sub = lax.broadcasted_iota(jnp.int32, (SUBLANES, ext_ref.shape[1]), 0)
    for j in range(1, halo + 1):
        cur = ext_ref[_rows(halo + n_grp - j), :]
        prev = carry_ref[_rows(halo - j), :]
        ext_ref[_rows(halo - j), :] = pltpu.roll(jnp.where(sub == SUBLANES - 1, prev, cur), 1, 0)
    carry_ref[...] = ext_ref[_rows(n_grp, halo), :]


def _seq_mixer_kernel(x_hbm, gmix_ref, w_in_ref, cw_ref, cb_ref, wg_ref, ba_ref, bx_ref,
                      lam_ref, wp_ref, ps_ref, w_out_ref, wup32_ref, wdn32_ref,
                      o_hbm, hlast_ref, nconv_ref, npool_ref, wup16_ref, wdn16_ref,
                      xbuf, obuf, in_sem, out_sem, hn_ref, extx_ref, carryx_ref, extp_ref,
                      carryp_ref, xc_ref, xcb_ref, gate_ref, ug_ref, a_ref, b_ref, ymix_ref, hc_ref,
                      *, n_t):
    n_grp = xbuf.shape[1]
    tm = n_grp * SUBLANES
    d = xbuf.shape[3]
    w_lru = a_ref.shape[1]
    w_pool = extp_ref.shape[1]
    hd = w_lru // LRU_HEADS
    half = w_lru // 2
    b = pl.program_id(0)
    t = pl.program_id(1)
    step = b * n_t + t
    n_steps = pl.num_programs(0) * n_t
    slot = step % 2

    @pl.when(step == 0)
    def _first_fetch():
        for c in _tile_copies(x_hbm, xbuf, in_sem, b, t, slot, n_grp, False):
            c.start()

    @pl.when(step + 1 < n_steps)
    def _prefetch():
        nxt = step + 1
        for c in _tile_copies(x_hbm, xbuf, in_sem, nxt // n_t, nxt % n_t, 1 - slot, n_grp, False):
            c.start()

    for c in _tile_copies(x_hbm, xbuf, in_sem, b, t, slot, n_grp, False):
        c.wait()

    @pl.when(t == 0)
    def _reset_state():
        hc_ref[...] = jnp.zeros_like(hc_ref)
        carryx_ref[...] = jnp.zeros_like(carryx_ref)
        carryp_ref[...] = jnp.zeros_like(carryp_ref)

    x_tile = lambda: xbuf[slot].reshape(tm, d)
    hn_ref[...] = _rmsnorm(x_tile(), gmix_ref[...]).astype(BF16)
    in_proj = lambda c0, c1: _dot(hn_ref[...], w_in_ref[:, c0:c1])
    sub = lax.broadcasted_iota(jnp.int32, (SUBLANES, w_lru), 0)
    first_token = (sub == 0) & (t == 0)
    cneg = _neg_c_softplus(lam_ref[...])

    extx_ref[_rows(CONV_HALO, n_grp), :] = in_proj(0, w_lru)
    _fill_halo(extx_ref, carryx_ref, CONV_HALO, n_grp)

    extp_ref[_rows(POOL_HALO, n_grp), 0:w_pool // 2] = in_proj(2 * w_lru, 2 * w_lru + w_pool // 2)
    for gp in range(n_grp // 2):
        g = 2 * gp
        acc = cb_ref[...] + extx_ref[_rows(CONV_HALO + g, 2), :] * cw_ref[CONV_W - 1:CONV_W, :]
        for k in range(CONV_W - 1):
            acc = acc + extx_ref[_rows(g + k, 2), :] * cw_ref[k:k + 1, :]
        xc_ref[_rows(g, 2), :] = acc
        xcb_ref[_rows(g, 2), :] = acc.astype(BF16)
    for hh in range(LRU_HEADS):
        gate_ref[:, 2 * hd * hh:2 * hd * (hh + 1)] = _dot(xcb_ref[:, hd * hh:hd * (hh + 1)], wg_ref[hh])

    extp_ref[_rows(POOL_HALO, n_grp), w_pool // 2:] = in_proj(2 * w_lru + w_pool // 2, 2 * w_lru + w_pool)
    _fill_halo(extp_ref, carryp_ref, POOL_HALO, n_grp)
    ug_ref[:, 0:half] = in_proj(w_lru, w_lru + half)

    def _scan_groups(g0, g1, a_cum, h_loc):
        for g in range(g0, g1):
            gates = gate_ref[_rows(g), :]
            ga = jnp.concatenate([gates[:, 2 * hd * hh:2 * hd * hh + hd] for hh in range(LRU_HEADS)], axis=1)
            gx = jnp.concatenate([gates[:, 2 * hd * hh + hd:2 * hd * (hh + 1)] for hh in range(LRU_HEADS)], axis=1)
            a, xi, mult = _gate_update(jnp.concatenate([ga, gx], axis=1), xc_ref[_rows(g), :],
                                       ba_ref[...], bx_ref[...], cneg)
            if g == 0:
                mult = jnp.where(first_token, 1.0, mult)
            bt = xi * mult
            a_cum = a if a_cum is None else a * a_cum
            h_loc = bt if h_loc is None else a * h_loc + bt
            a_ref[_rows(g), :] = a_cum
            b_ref[_rows(g), :] = h_loc
        return a_cum, h_loc

    a_tot, h_tot = _scan_groups(0, n_grp // 2, None, None)
    ug_ref[:, half:] = in_proj(w_lru + half, 2 * w_lru)
    a_tot, h_tot = _scan_groups(n_grp // 2, n_grp, a_tot, h_tot)

    gd = w_pool // len(POOL_WINDOWS)
    row = lax.broadcasted_iota(jnp.int32, (tm, 1), 0)
    pos = t * tm + (row & (SUBLANES - 1)) * n_grp + (row >> 3)
    pooled = []
    for gi, w in enumerate(POOL_WINDOWS):
        cols = slice(gi * gd, (gi + 1) * gd)
        lo = POOL_HALO - (w - 1)
        s = extp_ref[_rows(lo, n_grp + w - 1), cols]
        width = 1
        while width < w:
            s = s[width * SUBLANES:, :] + s[:-width * SUBLANES, :]
            width *= 2
        inv = 1.0 / jnp.minimum(pos + 1, w).astype(F32)
        pooled.append(s * inv - extp_ref[_rows(POOL_HALO, n_grp), cols])

    y_pool = _pool_project(jnp.concatenate(pooled, axis=1), wp_ref, ps_ref[...])
    ymix_ref[:, w_lru:] = y_pool.astype(BF16)
    obuf[slot] = (x_tile() + _dot(ymix_ref[:, w_lru:], w_out_ref[w_lru:, :])).reshape(n_grp, SUBLANES, d)

    p_cum, q_cum = a_tot, h_tot
    for dd in (1, 2, 4):
        keep = sub >= dd
        q_cum = jnp.where(keep, p_cum * pltpu.roll(q_cum, dd, 0) + q_cum, q_cum)
        p_cum = jnp.where(keep, p_cum * pltpu.roll(p_cum, dd, 0), p_cum)
    h_in = jnp.broadcast_to(hc_ref[...], (SUBLANES, w_lru))
    seg_end = p_cum * h_in + q_cum
    seg_in = jnp.where(sub == 0, h_in, pltpu.roll(seg_end, 1, 0))
    h_end = seg_end[SUBLANES - 1:SUBLANES, :]
    hc_ref[...] = h_end

    seg_in2 = jnp.concatenate([seg_in, seg_in], axis=0)
    for gp in range(n_grp // 2):
        r2 = _rows(2 * gp, 2)
        hs = b_ref[r2, :] + a_ref[r2, :] * seg_in2
        ymix_ref[r2, 0:w_lru] = (hs * jax.nn.gelu(ug_ref[r2, :], approximate=True)).astype(BF16)

    obuf[slot] += _dot(ymix_ref[:, 0:w_lru], w_out_ref[0:w_lru, :]).reshape(n_grp, SUBLANES, d)

    wup16_ref[...] = wup32_ref[...].astype(BF16)
    wdn16_ref[...] = wdn32_ref[...].astype(BF16)

    for c in _tile_copies(o_hbm, obuf, out_sem, b, t, slot, n_grp, True):
        c.start()

    @pl.when(step > 0)
    def _wait_prev_writeback():
        prv = step - 1
        for c in _tile_copies(o_hbm, obuf, out_sem, prv // n_t, prv % n_t, 1 - slot, n_grp, True):
            c.wait()

    @pl.when(step == n_steps - 1)
    def _wait_last_writeback():
        for c in _tile_copies(o_hbm, obuf, out_sem, b, t, slot, n_grp, True):
            c.wait()

    @pl.when(t == n_t - 1)
    def _emit_state():
        last = SUBLANES - 1
        hlast_ref[...] = h_end
        for i in range(CONV_HALO):
            nconv_ref[i:i + 1, :] = carryx_ref[pl.ds(i * SUBLANES + last, 1), :]
        for i in range(POOL_HALO):
            npool_ref[i:i + 1, :] = carryp_ref[pl.ds(i * SUBLANES + last, 1), :]


def _const_spec(shape):
    nd = len(shape)
    return pl.BlockSpec(shape, lambda *_: (0,) * nd, pipeline_mode=pl.Buffered(1))


def _seq_mixer(x, p, w_up, w_down, tm):
    bsz, seq, d = x.shape
    w_lru = p["conv_b"].shape[1]
    w_pool = p["pool_scale"].shape[1]
    weights = (p["norm_mix_g"], p["w_in"], p["conv_w"], p["conv_b"], p["w_gate"], p["b_rg_a"],
               p["b_rg_x"], p["lru_lambda"], p["w_pool"], p["pool_scale"], p["w_out"])
    n_t = seq // tm
    n_steps = bsz * n_t
    n_grp = tm // SUBLANES
    assert n_grp % 2 == 0 and n_grp >= POOL_HALO
    any_spec = pl.BlockSpec(memory_space=pl.ANY)
    state_spec = lambda rows, w: pl.BlockSpec((None, rows, w), lambda b, t: (b, 0, 0))
    slab_spec = lambda w: pl.BlockSpec((w.shape[0] // n_steps, w.shape[1]),
                                       lambda b, t: (b * n_t + t, 0))
    ext_rows = lambda halo: (halo + n_grp) * SUBLANES
    return pl.pallas_call(
        functools.partial(_seq_mixer_kernel, n_t=n_t),
        grid=(bsz, n_t),
        in_specs=([any_spec] + [_const_spec(w.shape) for w in weights]
                  + [slab_spec(w_up), slab_spec(w_down)]),
        out_specs=[any_spec, state_spec(1, w_lru), state_spec(CONV_W - 1, w_lru),
                   state_spec(POOL_BUF, w_pool), slab_spec(w_up), slab_spec(w_down)],
        out_shape=[jax.ShapeDtypeStruct((bsz, seq, d), F32),
                   jax.ShapeDtypeStruct((bsz, 1, w_lru), F32),
                   jax.ShapeDtypeStruct((bsz, CONV_W - 1, w_lru), F32),
                   jax.ShapeDtypeStruct((bsz, POOL_BUF, w_pool), F32),
                   jax.ShapeDtypeStruct(w_up.shape, BF16),
                   jax.ShapeDtypeStruct(w_down.shape, BF16)],
        scratch_shapes=[
            pltpu.VMEM((2, n_grp, SUBLANES, d), F32),
            pltpu.VMEM((2, n_grp, SUBLANES, d), F32),
            pltpu.SemaphoreType.DMA((2,)),
            pltpu.SemaphoreType.DMA((2,)),
            pltpu.VMEM((tm, d), BF16),
            pltpu.VMEM((ext_rows(CONV_HALO), w_lru), F32),
            pltpu.VMEM((CONV_HALO * SUBLANES, w_lru), F32),
            pltpu.VMEM((ext_rows(POOL_HALO), w_pool), F32),
            pltpu.VMEM((POOL_HALO * SUBLANES, w_pool), F32),
            pltpu.VMEM((tm, w_lru), F32),
            pltpu.VMEM((tm, w_lru), BF16),
            pltpu.VMEM((tm, 2 * w_lru), F32),
            pltpu.VMEM((tm, w_lru), F32),
            pltpu.VMEM((tm, w_lru), F32),
            pltpu.VMEM((tm, w_lru), F32),
            pltpu.VMEM((tm, d), BF16),
            pltpu.VMEM((1, w_lru), F32),
        ],
        compiler_params=pltpu.CompilerParams(
            dimension_semantics=("arbitrary", "arbitrary"),
            vmem_limit_bytes=MIX_VMEM_LIMIT),
        name="seq_mixer",
    )(x, *weights, w_up, w_down)


def _dec_mixer_kernel(x_ref, h0_ref, cbuf_ref, pbuf_ref, gmix_ref, w_in_ref, cw_ref, cb_ref,
                      wg_ref, ba_ref, bx_ref, lam_ref, wp_ref, ps_ref, w_out_ref,
                      o_ref, hnew_ref, nconv_ref, npool_ref):
    w_lru = h0_ref.shape[1]
    w_pool = ps_ref.shape[1]
    x = x_ref[...]
    ux, ug, up = _in_proj(x, gmix_ref[...], w_in_ref, w_lru)

    xc = cb_ref[...] + ux * cw_ref[CONV_W - 1:CONV_W, :]
    for k in range(CONV_W - 1):
        tap = cbuf_ref[:, k, :]
        xc = xc + tap * cw_ref[k:k + 1, :]
        if k > 0:
            nconv_ref[:, k - 1, :] = tap
    nconv_ref[:, CONV_W - 2, :] = ux

    a, mult, xi = _lru_gates(xc, wg_ref, ba_ref[...], bx_ref[...], _neg_c_softplus(lam_ref[...]))
    h_new = a * h0_ref[...] + xi * mult
    hnew_ref[...] = h_new
    y_lru = h_new * jax.nn.gelu(ug, approximate=True)

    gd = w_pool // len(POOL_WINDOWS)
    s = up
    width = 1
    pooled = []
    for g, w in enumerate(POOL_WINDOWS):
        while width < w:
            k = POOL_BUF - width
            hist = pbuf_ref[:, k, :]
            if k > 0:
                npool_ref[:, k - 1, :] = hist
            s = s + hist[:, g * gd:]
            width += 1
        count = float(min(PAST_LEN + 1, w))
        pooled.append(s[:, 0:gd] / count - up[:, g * gd:(g + 1) * gd])
        if g + 1 < len(POOL_WINDOWS):
            s = s[:, gd:]
    npool_ref[:, POOL_BUF - 1, :] = up
    y_pool = _pool_project(jnp.concatenate(pooled, axis=1), wp_ref, ps_ref[...])

    y_mix = jnp.concatenate([y_lru, y_pool], axis=1).astype(BF16)
    o_ref[...] = x + _dot(y_mix, w_out_ref[...])


def _dec_mixer(x, h0, cbuf, pbuf, p):
    rows, d = x.shape
    weights = (p["norm_mix_g"], p["w_in"], p["conv_w"], p["conv_b"], p["w_gate"], p["b_rg_a"],
               p["b_rg_x"], p["lru_lambda"], p["w_pool"], p["pool_scale"], p["w_out"])
    args = (x, h0, cbuf, pbuf) + weights
    return pl.pallas_call(
        _dec_mixer_kernel,
        grid=(1,),
        in_specs=[_const_spec(a.shape) for a in args],
        out_specs=[pl.BlockSpec(s.shape, functools.partial(lambda nd, i: (0,) * nd, s.ndim))
                   for s in (x, h0, cbuf, pbuf)],
        out_shape=[jax.ShapeDtypeStruct(s.shape, F32) for s in (x, h0, cbuf, pbuf)],
        compiler_params=pltpu.CompilerParams(
            dimension_semantics=("arbitrary",), vmem_limit_bytes=MIX_VMEM_LIMIT),
        name="dec_mixer",
    )(*args)


def _mlp_kernel(x_ref, g_ref, wup_ref, wdn_ref, gfin_ref, o_ref, h_ref):
    j = pl.program_id(1)

    @pl.when(j == 0)
    def _norm():
        x = x_ref[...]
        h_ref[...] = _rmsnorm(x, g_ref[...]).astype(BF16)
        o_ref[...] = x

    f = jnp.maximum(_dot(h_ref[...], wup_ref[...]), 0.0)
    o_ref[...] += _dot((f * f).astype(BF16), wdn_ref[...])

    @pl.when(j == pl.num_programs(1) - 1)
    def _finish():
        o_ref[...] = _rmsnorm(o_ref[...], gfin_ref[...])


def _mlp(x, p, tm, tf):
    n, d = x.shape
    d_ff = p["w_up"].shape[1]
    vec_spec = pl.BlockSpec((1, d), lambda i, j: (0, 0))
    tok_spec = pl.BlockSpec((tm, d), lambda i, j: (i, 0))
    xin_spec = pl.BlockSpec((tm, d), lambda i, j: (i, 0), pipeline_mode=pl.Buffered(1))
    return pl.pallas_call(
        _mlp_kernel,
        grid=(n // tm, d_ff // tf),
        in_specs=[xin_spec, vec_spec,
                  pl.BlockSpec((d, tf), lambda i, j: (0, j)),
                  pl.BlockSpec((tf, d), lambda i, j: (j, 0)),
                  vec_spec],
        out_specs=tok_spec,
        out_shape=jax.ShapeDtypeStruct((n, d), F32),
        scratch_shapes=[pltpu.VMEM((tm, d), BF16)],
        compiler_params=pltpu.CompilerParams(
            dimension_semantics=("arbitrary", "arbitrary"),
            vmem_limit_bytes=MLP_VMEM_LIMIT),
        name="mlp",
    )(x, p["norm_mlp_g"], p["w_up"], p["w_down"], p["norm_final_g"])


def _layer_params(l, norm_mix_g, w_in, conv_w, conv_b, w_rg_a, b_rg_a, w_rg_x, b_rg_x, lru_lambda,
                  w_pool, pool_scale, w_out, norm_mlp_g, w_up, w_down, norm_final_g):
    row = lambda v: v.reshape(1, -1)
    return {
        "norm_mix_g": row(norm_mix_g[l]),
        "w_in": w_in[l].astype(BF16),
        "conv_w": conv_w[l],
        "conv_b": row(conv_b[l]),
        "w_gate": jnp.concatenate([w_rg_a[l], w_rg_x[l]], axis=-1).astype(BF16),
        "b_rg_a": row(b_rg_a[l]),
        "b_rg_x": row(b_rg_x[l]),
        "lru_lambda": row(lru_lambda[l]),
        "w_pool": w_pool[l].astype(BF16),
        "pool_scale": row(pool_scale[l]),
        "w_out": w_out[l].astype(BF16),
        "norm_mlp_g": row(norm_mlp_g[l]),
        "w_up": w_up[l],
        "w_down": w_down[l],
        "norm_final_g": row(norm_final_g),
    }


def kernel(x_prompt, x_sample, state_lru_h, state_conv, state_pool, norm_mix_g, w_in, conv_w, conv_b, w_rg_a, b_rg_a, w_rg_x, b_rg_x, lru_lambda, w_pool, pool_scale, w_out, norm_mlp_g, w_up, w_down, norm_final_g):
    depth = w_in.shape[0]
    assert depth == 1, "final RMSNorm is fused into the (single) layer's MLP kernel"
    bp, tp, d = x_prompt.shape
    bs, ts, _ = x_sample.shape
    assert ts == 1 and tp % MIX_TM == 0 and (bp * tp) % MLP_TM == 0
    p = _layer_params(0, norm_mix_g, w_in, conv_w, conv_b, w_rg_a, b_rg_a, w_rg_x, b_rg_x,
                      lru_lambda, w_pool, pool_scale, w_out, norm_mlp_g, w_up, w_down, norm_final_g)

    x1p, hp, cp, pp, p["w_up"], p["w_down"] = _seq_mixer(x_prompt, p, p["w_up"], p["w_down"], MIX_TM)
    yp = _mlp(x1p.reshape(bp * tp, d), p, MLP_TM, MLP_TF).reshape(bp, tp, d)

    x1s, hs, cs, ps = _dec_mixer(x_sample.reshape(bs, d), state_lru_h[0], state_conv[0],
                                 state_pool[0], p)
    ys = _mlp(x1s, p, bs, MLP_TF).reshape(bs, ts, d)

    return (yp, ys, hp.reshape(1, bp, -1), cp[None], pp[None], hs[None], cs[None], ps[None])
```

```python
import functools

import jax
import jax.numpy as jnp
from jax import lax
from jax.experimental import pallas as pl
from jax.experimental.pallas import tpu as pltpu

EPS = 1e-6
LRU_HEADS = 8
LRU_C = 8.0
CONV_W = 4
POOL_WINDOWS = (2, 4, 8, 16)
POOL_BUF = max(POOL_WINDOWS) - 1
PAST_LEN = 16384

SUBLANES = 8
MIB = 1024 * 1024

MIX_TM = 256
MLP_TM = 512
MLP_TF = 1024
MIX_VMEM_LIMIT = 60 * MIB
MLP_VMEM_LIMIT = 48 * MIB

F32 = jnp.float32
BF16 = jnp.bfloat16


def _rmsnorm(x, g):
    return (x * lax.rsqrt(jnp.mean(x * x, axis=-1, keepdims=True) + EPS)) * g


def _dot(a, b):
    return jnp.dot(a, b, preferred_element_type=F32)


def _in_proj(x, g, w_in_ref, w_lru):
    h = _rmsnorm(x, g).astype(BF16)
    ux = _dot(h, w_in_ref[:, 0:w_lru])
    ug = _dot(h, w_in_ref[:, w_lru:2 * w_lru])
    up = _dot(h, w_in_ref[:, 2 * w_lru:])
    return ux, ug, up


def _gate_update(g, xc, ba, bx, cneg):
    hd = xc.shape[1]
    r = jax.nn.sigmoid(g[:, :hd] + ba)
    i = jax.nn.sigmoid(g[:, hd:] + bx)
    a = jnp.exp(r * cneg)
    y = 1.0 - a * a
    mult = jnp.where(y > 0.0, y * lax.rsqrt(y), 0.0)
    return a, xc * i, mult


def _lru_gates(xc, wg_ref, ba, bx, cneg):
    hd = xc.shape[1] // LRU_HEADS
    xcb = xc.astype(BF16)
    a_parts, m_parts, xi_parts = [], [], []
    for h in range(LRU_HEADS):
        sl = slice(h * hd, (h + 1) * hd)
        g = _dot(xcb[:, sl], wg_ref[h])
        a, xi, mult = _gate_update(g, xc[:, sl], ba[:, sl], bx[:, sl], cneg[:, sl])
        a_parts.append(a)
        m_parts.append(mult)
        xi_parts.append(xi)
    cat = lambda ps: jnp.concatenate(ps, axis=1)
    return cat(a_parts), cat(m_parts), cat(xi_parts)


def _pool_project(pooled, wp_ref, scale):
    n_g = len(POOL_WINDOWS)
    gd = pooled.shape[1] // n_g
    pb = pooled.astype(BF16)
    outs = [_dot(pb[:, g * gd:(g + 1) * gd], wp_ref[g]) for g in range(n_g)]
    return jnp.concatenate(outs, axis=1) * scale


def _neg_c_softplus(lam):
    return -LRU_C * jax.nn.softplus(-lam)


CONV_HALO = CONV_W - 1
POOL_HALO = POOL_BUF


def _rows(k, n=1):
    return pl.ds(k * SUBLANES, n * SUBLANES)


def _tile_copies(hbm, buf, sem, b, t, slot, n_grp, to_hbm):
    tm = n_grp * SUBLANES
    copies = []
    for s in range(SUBLANES):
        far = hbm.at[b, pl.ds(t * tm + s * n_grp, n_grp), :]
        near = buf.at[slot, :, s, :]
        src, dst = (near, far) if to_hbm else (far, near)
        copies.append(pltpu.make_async_copy(src, dst, sem.at[slot]))
    return copies


def _fill_halo(ext_ref, carry_ref, halo, n_grp):
    sub = lax.broadcasted_iota(jnp.int32, (SUBLANES, ext_ref.shape[1]), 0)
    for j in range(1, halo + 1):
        cur = ext_ref[_rows(halo + n_grp - j), :]
        prev = carry_ref[_rows(halo - j), :]
        ext_ref[_rows(halo - j), :] = pltpu.roll(jnp.where(sub == SUBLANES - 1, prev, cur), 1, 0)
    carry_ref[...] = ext_ref[_rows(n_grp, halo), :]


def _seq_mixer_kernel(x_hbm, gmix_ref, w_in_ref, cw_ref, cb_ref, wg_ref, ba_ref, bx_ref,
                      lam_ref, wp_ref, ps_ref, w_out_ref, wup32_ref, wdn32_ref,
                      o_hbm, hlast_ref, nconv_ref, npool_ref, wup16_ref, wdn16_ref,
                      xbuf, obuf, in_sem, out_sem, hn_ref, extx_ref, carryx_ref, extp_ref,
                      carryp_ref, xc_ref, xcb_ref, gate_ref, ug_ref, a_ref, b_ref, ymix_ref, hc_ref,
                      *, n_t):
    n_grp = xbuf.shape[1]
    tm = n_grp * SUBLANES
    d = xbuf.shape[3]
    w_lru = a_ref.shape[1]
    w_pool = extp_ref.shape[1]
    hd = w_lru // LRU_HEADS
    half = w_lru // 2
    b = pl.program_id(0)
    t = pl.program_id(1)
    step = b * n_t + t
    n_steps = pl.num_programs(0) * n_t
    slot = step % 2

    @pl.when(step == 0)
    def _first_fetch():
        for c in _tile_copies(x_hbm, xbuf, in_sem, b, t, slot, n_grp, False):
            c.start()

    @pl.when(step + 1 < n_steps)
    def _prefetch():
        nxt = step + 1
        for c in _tile_copies(x_hbm, xbuf, in_sem, nxt // n_t, nxt % n_t, 1 - slot, n_grp, False):
            c.start()

    for c in _tile_copies(x_hbm, xbuf, in_sem, b, t, slot, n_grp, False):
        c.wait()

    @pl.when(t == 0)
    def _reset_state():
        hc_ref[...] = jnp.zeros_like(hc_ref)
        carryx_ref[...] = jnp.zeros_like(carryx_ref)
        carryp_ref[...] = jnp.zeros_like(carryp_ref)

    x_tile = lambda: xbuf[slot].reshape(tm, d)
    hn_ref[...] = _rmsnorm(x_tile(), gmix_ref[...]).astype(BF16)
    in_proj = lambda c0, c1: _dot(hn_ref[...], w_in_ref[:, c0:c1])
    sub = lax.broadcasted_iota(jnp.int32, (SUBLANES, w_lru), 0)
    first_token = (sub == 0) & (t == 0)
    cneg = _neg_c_softplus(lam_ref[...])

    extx_ref[_rows(CONV_HALO, n_grp), :] = in_proj(0, w_lru)
    _fill_halo(extx_ref, carryx_ref, CONV_HALO, n_grp)

    extp_ref[_rows(POOL_HALO, n_grp), 0:w_pool // 2] = in_proj(2 * w_lru, 2 * w_lru + w_pool // 2)
    for gp in range(n_grp // 2):
        g = 2 * gp
        acc = cb_ref[...] + extx_ref[_rows(CONV_HALO + g, 2), :] * cw_ref[CONV_W - 1:CONV_W, :]
        for k in range(CONV_W - 1):
            acc = acc + extx_ref[_rows(g + k, 2), :] * cw_ref[k:k + 1, :]
        xc_ref[_rows(g, 2), :] = acc
        xcb_ref[_rows(g, 2), :] = acc.astype(BF16)
    for hh in range(LRU_HEADS):
        gate_ref[:, 2 * hd * hh:2 * hd * (hh + 1)] = _dot(xcb_ref[:, hd * hh:hd * (hh + 1)], wg_ref[hh])

    extp_ref[_rows(POOL_HALO, n_grp), w_pool // 2:] = in_proj(2 * w_lru + w_pool // 2, 2 * w_lru + w_pool)
    _fill_halo(extp_ref, carryp_ref, POOL_HALO, n_grp)
    ug_ref[:, 0:half] = in_proj(w_lru, w_lru + half)

    def _scan_groups(g0, g1, a_cum, h_loc):
        for g in range(g0, g1):
            gates = gate_ref[_rows(g), :]
            ga = jnp.concatenate([gates[:, 2 * hd * hh:2 * hd * hh + hd] for hh in range(LRU_HEADS)], axis=1)
            gx = jnp.concatenate([gates[:, 2 * hd * hh + hd:2 * hd * (hh + 1)] for hh in range(LRU_HEADS)], axis=1)
            a, xi, mult = _gate_update(jnp.concatenate([ga, gx], axis=1), xc_ref[_rows(g), :],
                                       ba_ref[...], bx_ref[...], cneg)
            if g == 0:
                mult = jnp.where(first_token, 1.0, mult)
            bt = xi * mult
            a_cum = a if a_cum is None else a * a_cum
            h_loc = bt if h_loc is None else a * h_loc + bt
            a_ref[_rows(g), :] = a_cum
            b_ref[_rows(g), :] = h_loc
        return a_cum, h_loc

    a_tot, h_tot = _scan_groups(0, n_grp // 2, None, None)
    ug_ref[:, half:] = in_proj(w_lru + half, 2 * w_lru)
    a_tot, h_tot = _scan_groups(n_grp // 2, n_grp, a_tot, h_tot)

    gd = w_pool // len(POOL_WINDOWS)
    row = lax.broadcasted_iota(jnp.int32, (tm, 1), 0)
    pos = t * tm + (row & (SUBLANES - 1)) * n_grp + (row >> 3)
    pooled = []
    for gi, w in enumerate(POOL_WINDOWS):
        cols = slice(gi * gd, (gi + 1) * gd)
        lo = POOL_HALO - (w - 1)
        s = extp_ref[_rows(lo, n_grp + w - 1), cols]
        width = 1
        while width < w:
            s = s[width * SUBLANES:, :] + s[:-width * SUBLANES, :]
            width *= 2
        inv = 1.0 / jnp.minimum(pos + 1, w).astype(F32)
        pooled.append(s * inv - extp_ref[_rows(POOL_HALO, n_grp), cols])

    y_pool = _pool_project(jnp.concatenate(pooled, axis=1), wp_ref, ps_ref[...])
    ymix_ref[:, w_lru:] = y_pool.astype(BF16)
    obuf[slot] = (x_tile() + _dot(ymix_ref[:, w_lru:], w_out_ref[w_lru:, :])).reshape(n_grp, SUBLANES, d)

    p_cum, q_cum = a_tot, h_tot
    for dd in (1, 2, 4):
        keep = sub >= dd
        q_cum = jnp.where(keep, p_cum * pltpu.roll(q_cum, dd, 0) + q_cum, q_cum)
        p_cum = jnp.where(keep, p_cum * pltpu.roll(p_cum, dd, 0), p_cum)
    h_in = jnp.broadcast_to(hc_ref[...], (SUBLANES, w_lru))
    seg_end = p_cum * h_in + q_cum
    seg_in = jnp.where(sub == 0, h_in, pltpu.roll(seg_end, 1, 0))
    h_end = seg_end[SUBLANES - 1:SUBLANES, :]
    hc_ref[...] = h_end

    seg_in2 = jnp.concatenate([seg_in, seg_in], axis=0)
    for gp in range(n_grp // 2):
        r2 = _rows(2 * gp, 2)
        hs = b_ref[r2, :] + a_ref[r2, :] * seg_in2
        ymix_ref[r2, 0:w_lru] = (hs * jax.nn.gelu(ug_ref[r2, :], approximate=True)).astype(BF16)

    obuf[slot] += _dot(ymix_ref[:, 0:w_lru], w_out_ref[0:w_lru, :]).reshape(n_grp, SUBLANES, d)

    wup16_ref[...] = wup32_ref[...].astype(BF16)
    wdn16_ref[...] = wdn32_ref[...].astype(BF16)

    for c in _tile_copies(o_hbm, obuf, out_sem, b, t, slot, n_grp, True):
        c.start()

    @pl.when(step > 0)
    def _wait_prev_writeback():
        prv = step - 1
        for c in _tile_copies(o_hbm, obuf, out_sem, prv // n_t, prv % n_t, 1 - slot, n_grp, True):
            c.wait()

    @pl.when(step == n_steps - 1)
    def _wait_last_writeback():
        for c in _tile_copies(o_hbm, obuf, out_sem, b, t, slot, n_grp, True):
            c.wait()

    @pl.when(t == n_t - 1)
    def _emit_state():
        last = SUBLANES - 1
        hlast_ref[...] = h_end
        for i in range(CONV_HALO):
            nconv_ref[i:i + 1, :] = carryx_ref[pl.ds(i * SUBLANES + last, 1), :]
        for i in range(POOL_HALO):
            npool_ref[i:i + 1, :] = carryp_ref[pl.ds(i * SUBLANES + last, 1), :]


def _const_spec(shape):
    nd = len(shape)
    return pl.BlockSpec(shape, lambda *_: (0,) * nd, pipeline_mode=pl.Buffered(1))


def _seq_mixer(x, p, w_up, w_down, tm):
    bsz, seq, d = x.shape
    w_lru = p["conv_b"].shape[1]
    w_pool = p["pool_scale"].shape[1]
    weights = (p["norm_mix_g"], p["w_in"], p["conv_w"], p["conv_b"], p["w_gate"], p["b_rg_a"],
               p["b_rg_x"], p["lru_lambda"], p["w_pool"], p["pool_scale"], p["w_out"])
    n_t = seq // tm
    n_steps = bsz * n_t
    n_grp = tm // SUBLANES
    assert n_grp % 2 == 0 and n_grp >= POOL_HALO
    any_spec = pl.BlockSpec(memory_space=pl.ANY)
    state_spec = lambda rows, w: pl.BlockSpec((None, rows, w), lambda b, t: (b, 0, 0))
    slab_spec = lambda w: pl.BlockSpec((w.shape[0] // n_steps, w.shape[1]),
                                       lambda b, t: (b * n_t + t, 0))
    ext_rows = lambda halo: (halo + n_grp) * SUBLANES
    return pl.pallas_call(
        functools.partial(_seq_mixer_kernel, n_t=n_t),
        grid=(bsz, n_t),
        in_specs=([any_spec] + [_const_spec(w.shape) for w in weights]
                  + [slab_spec(w_up), slab_spec(w_down)]),
        out_specs=[any_spec, state_spec(1, w_lru), state_spec(CONV_W - 1, w_lru),
                   state_spec(POOL_BUF, w_pool), slab_spec(w_up), slab_spec(w_down)],
        out_shape=[jax.ShapeDtypeStruct((bsz, seq, d), F32),
                   jax.ShapeDtypeStruct((bsz, 1, w_lru), F32),
                   jax.ShapeDtypeStruct((bsz, CONV_W - 1, w_lru), F32),
                   jax.ShapeDtypeStruct((bsz, POOL_BUF, w_pool), F32),
                   jax.ShapeDtypeStruct(w_up.shape, BF16),
                   jax.ShapeDtypeStruct(w_down.shape, BF16)],
        scratch_shapes=[
            pltpu.VMEM((2, n_grp, SUBLANES, d), F32),
            pltpu.VMEM((2, n_grp, SUBLANES, d), F32),
            pltpu.SemaphoreType.DMA((2,)),
            pltpu.SemaphoreType.DMA((2,)),
            pltpu.VMEM((tm, d), BF16),
            pltpu.VMEM((ext_rows(CONV_HALO), w_lru), F32),
            pltpu.VMEM((CONV_HALO * SUBLANES, w_lru), F32),
            pltpu.VMEM((ext_rows(POOL_HALO), w_pool), F32),
            pltpu.VMEM((POOL_HALO * SUBLANES, w_pool), F32),
            pltpu.VMEM((tm, w_lru), F32),
            pltpu.VMEM((tm, w_lru), BF16),
            pltpu.VMEM((tm, 2 * w_lru), F32),
            pltpu.VMEM((tm, w_lru), F32),
            pltpu.VMEM((tm, w_lru), F32),
            pltpu.VMEM((tm, w_lru), F32),
            pltpu.VMEM((tm, d), BF16),
            pltpu.VMEM((1, w_lru), F32),
        ],
        compiler_params=pltpu.CompilerParams(
            dimension_semantics=("arbitrary", "arbitrary"),
            vmem_limit_bytes=MIX_VMEM_LIMIT),
        name="seq_mixer",
    )(x, *weights, w_up, w_down)


def _dec_mixer_kernel(x_ref, h0_ref, cbuf_ref, pbuf_ref, gmix_ref, w_in_ref, cw_ref, cb_ref,
                      wg_ref, ba_ref, bx_ref, lam_ref, wp_ref, ps_ref, w_out_ref,
                      o_ref, hnew_ref, nconv_ref, npool_ref):
    w_lru = h0_ref.shape[1]
    w_pool = ps_ref.shape[1]
    x = x_ref[...]
    ux, ug, up = _in_proj(x, gmix_ref[...], w_in_ref, w_lru)

    xc = cb_ref[...] + ux * cw_ref[CONV_W - 1:CONV_W, :]
    for k in range(CONV_W - 1):
        tap = cbuf_ref[:, k, :]
        xc = xc + tap * cw_ref[k:k + 1, :]
        if k > 0:
            nconv_ref[:, k - 1, :] = tap
    nconv_ref[:, CONV_W - 2, :] = ux

    a, mult, xi = _lru_gates(xc, wg_ref, ba_ref[...], bx_ref[...], _neg_c_softplus(lam_ref[...]))
    h_new = a * h0_ref[...] + xi * mult
    hnew_ref[...] = h_new
    y_lru = h_new * jax.nn.gelu(ug, approximate=True)

    gd = w_pool // len(POOL_WINDOWS)
    s = up
    width = 1
    pooled = []
    for g, w in enumerate(POOL_WINDOWS):
        while width < w:
            k = POOL_BUF - width
            hist = pbuf_ref[:, k, :]
            if k > 0:
                npool_ref[:, k - 1, :] = hist
            s = s + hist[:, g * gd:]
            width += 1
        count = float(min(PAST_LEN + 1, w))
        pooled.append(s[:, 0:gd] / count - up[:, g * gd:(g + 1) * gd])
        if g + 1 < len(POOL_WINDOWS):
            s = s[:, gd:]
    npool_ref[:, POOL_BUF - 1, :] = up
    y_pool = _pool_project(jnp.concatenate(pooled, axis=1), wp_ref, ps_ref[...])

    y_mix = jnp.concatenate([y_lru, y_pool], axis=1).astype(BF16)
    o_ref[...] = x + _dot(y_mix, w_out_ref[...])


def _dec_mixer(x, h0, cbuf, pbuf, p):
    rows, d = x.shape
    weights = (p["norm_mix_g"], p["w_in"], p["conv_w"], p["conv_b"], p["w_gate"], p["b_rg_a"],
               p["b_rg_x"], p["lru_lambda"], p["w_pool"], p["pool_scale"], p["w_out"])
    args = (x, h0, cbuf, pbuf) + weights
    return pl.pallas_call(
        _dec_mixer_kernel,
        grid=(1,),
        in_specs=[_const_spec(a.shape) for a in args],
        out_specs=[pl.BlockSpec(s.shape, functools.partial(lambda nd, i: (0,) * nd, s.ndim))
                   for s in (x, h0, cbuf, pbuf)],
        out_shape=[jax.ShapeDtypeStruct(s.shape, F32) for s in (x, h0, cbuf, pbuf)],
        compiler_params=pltpu.CompilerParams(
            dimension_semantics=("arbitrary",), vmem_limit_bytes=MIX_VMEM_LIMIT),
        name="dec_mixer",
    )(*args)


def _mlp_kernel(x_ref, g_ref, wup_ref, wdn_ref, gfin_ref, o_ref, h_ref):
    j = pl.program_id(1)

    @pl.when(j == 0)
    def _norm():
        x = x_ref[...]
        h_ref[...] = _rmsnorm(x, g_ref[...]).astype(BF16)
        o_ref[...] = x

    f = jnp.maximum(_dot(h_ref[...], wup_ref[...]), 0.0)
    o_ref[...] += _dot((f * f).astype(BF16), wdn_ref[...])

    @pl.when(j == pl.num_programs(1) - 1)
    def _finish():
        o_ref[...] = _rmsnorm(o_ref[...], gfin_ref[...])


def _mlp(x, p, tm, tf):
    n, d = x.shape
    d_ff = p["w_up"].shape[1]
    vec_spec = pl.BlockSpec((1, d), lambda i, j: (0, 0))
    tok_spec = pl.BlockSpec((tm, d), lambda i, j: (i, 0))
    return pl.pallas_call(
        _mlp_kernel,
        grid=(n // tm, d_ff // tf),
        in_specs=[tok_spec, vec_spec,
                  pl.BlockSpec((d, tf), lambda i, j: (0, j)),
                  pl.BlockSpec((tf, d), lambda i, j: (j, 0)),
                  vec_spec],
        out_specs=tok_spec,
        out_shape=jax.ShapeDtypeStruct((n, d), F32),
        scratch_shapes=[pltpu.VMEM((tm, d), BF16)],
        compiler_params=pltpu.CompilerParams(
            dimension_semantics=("arbitrary", "arbitrary"),
            vmem_limit_bytes=MLP_VMEM_LIMIT),
        name="mlp",
    )(x, p["norm_mlp_g"], p["w_up"], p["w_down"], p["norm_final_g"])


def _layer_params(l, norm_mix_g, w_in, conv_w, conv_b, w_rg_a, b_rg_a, w_rg_x, b_rg_x, lru_lambda,
                  w_pool, pool_scale, w_out, norm_mlp_g, w_up, w_down, norm_final_g):
    row = lambda v: v.reshape(1, -1)
    return {
        "norm_mix_g": row(norm_mix_g[l]),
        "w_in": w_in[l].astype(BF16),
        "conv_w": conv_w[l],
        "conv_b": row(conv_b[l]),
        "w_gate": jnp.concatenate([w_rg_a[l], w_rg_x[l]], axis=-1).astype(BF16),
        "b_rg_a": row(b_rg_a[l]),
        "b_rg_x": row(b_rg_x[l]),
        "lru_lambda": row(lru_lambda[l]),
        "w_pool": w_pool[l].astype(BF16),
        "pool_scale": row(pool_scale[l]),
        "w_out": w_out[l].astype(BF16),
        "norm_mlp_g": row(norm_mlp_g[l]),
        "w_up": w_up[l],
        "w_down": w_down[l],
        "norm_final_g": row(norm_final_g),
    }


def kernel(x_prompt, x_sample, state_lru_h, state_conv, state_pool, norm_mix_g, w_in, conv_w, conv_b, w_rg_a, b_rg_a, w_rg_x, b_rg_x, lru_lambda, w_pool, pool_scale, w_out, norm_mlp_g, w_up, w_down, norm_final_g):
    depth = w_in.shape[0]
    assert depth == 1, "final RMSNorm is fused into the (single) layer's MLP kernel"
    bp, tp, d = x_prompt.shape
    bs, ts, _ = x_sample.shape
    assert ts == 1 and tp % MIX_TM == 0 and (bp * tp) % MLP_TM == 0
    p = _layer_params(0, norm_mix_g, w_in, conv_w, conv_b, w_rg_a, b_rg_a, w_rg_x, b_rg_x,
                      lru_lambda, w_pool, pool_scale, w_out, norm_mlp_g, w_up, w_down, norm_final_g)

    x1p, hp, cp, pp, p["w_up"], p["w_down"] = _seq_mixer(x_prompt, p, p["w_up"], p["w_down"], MIX_TM)
    yp = _mlp(x1p.reshape(bp * tp, d), p, MLP_TM, MLP_TF).reshape(bp, tp, d)

    x1s, hs, cs, ps = _dec_mixer(x_sample.reshape(bs, d), state_lru_h[0], state_conv[0],
                                 state_pool[0], p)
    ys = _mlp(x1s, p, bs, MLP_TF).reshape(bs, ts, d)

    return (yp, ys, hp.reshape(1, bp, -1), cp[None], pp[None], hs[None], cs[None], ps[None])
```

```python
import functools

import jax
import jax.numpy as jnp
from jax import lax
from jax.experimental import pallas as pl
from jax.experimental.pallas import tpu as pltpu

EPS = 1e-6
LRU_HEADS = 8
LRU_C = 8.0
CONV_W = 4
POOL_WINDOWS = (2, 4, 8, 16)
POOL_BUF = max(POOL_WINDOWS) - 1
PAST_LEN = 16384

SUBLANES = 8
MIB = 1024 * 1024

MIX_TM = 256
MLP_TM = 512
MLP_TF = 2048
MIX_VMEM_LIMIT = 60 * MIB
DEC_VMEM_LIMIT = 56 * MIB
MLP_VMEM_LIMIT = 60 * MIB

F32 = jnp.float32
BF16 = jnp.bfloat16


def _rmsnorm(x, g):
    return (x * lax.rsqrt(jnp.mean(x * x, axis=-1, keepdims=True) + EPS)) * g


def _dot(a, b):
    return jnp.dot(a, b, preferred_element_type=F32)


def _gate_update(g, xc, ba, bx, cneg):
    hd = xc.shape[1]
    r = jax.nn.sigmoid(g[:, :hd] + ba)
    i = jax.nn.sigmoid(g[:, hd:] + bx)
    a = jnp.exp(r * cneg)
    y = 1.0 - a * a
    mult = jnp.where(y > 0.0, y * lax.rsqrt(y), 0.0)
    return a, xc * i, mult


def _lru_gates(xc, wg_ref, ba, bx, cneg):
    hd = xc.shape[1] // LRU_HEADS
    xcb = xc.astype(BF16)
    a_parts, m_parts, xi_parts = [], [], []
    for h in range(LRU_HEADS):
        sl = slice(h * hd, (h + 1) * hd)
        g = _dot(xcb[:, sl], wg_ref[h])
        a, xi, mult = _gate_update(g, xc[:, sl], ba[:, sl], bx[:, sl], cneg[:, sl])
        a_parts.append(a)
        m_parts.append(mult)
        xi_parts.append(xi)
    cat = lambda ps: jnp.concatenate(ps, axis=1)
    return cat(a_parts), cat(m_parts), cat(xi_parts)


def _pool_project(pooled, wp_ref, scale):
    n_g = len(POOL_WINDOWS)
    gd = pooled.shape[1] // n_g
    pb = pooled.astype(BF16)
    outs = [_dot(pb[:, g * gd:(g + 1) * gd], wp_ref[g]) for g in range(n_g)]
    return jnp.concatenate(outs, axis=1) * scale


def _neg_c_softplus(lam):
    return -LRU_C * jax.nn.softplus(-lam)


CONV_HALO = CONV_W - 1
POOL_HALO = POOL_BUF


def _rows(k, n=1):
    return pl.ds(k * SUBLANES, n * SUBLANES)


def _tile_copies(hbm, buf, sem, b, t, slot, n_grp, to_hbm):
    tm = n_grp * SUBLANES
    copies = []
    for s in range(SUBLANES):
        far = hbm.at[b, pl.ds(t * tm + s * n_grp, n_grp), :]
        near = buf.at[slot, :, s, :]
        src, dst = (near, far) if to_hbm else (far, near)
        copies.append(pltpu.make_async_copy(src, dst, sem.at[slot]))
    return copies


def _fill_halo(ext_ref, carry_ref, halo, n_grp):
    sub = lax.broadcasted_iota(jnp.int32, (SUBLANES, ext_ref.shape[1]), 0)
    for j in range(1, halo + 1):
        cur = ext_ref[_rows(halo + n_grp - j), :]
        prev = carry_ref[_rows(halo - j), :]
        ext_ref[_rows(halo - j), :] = pltpu.roll(jnp.where(sub == SUBLANES - 1, prev, cur), 1, 0)
    carry_ref[...] = ext_ref[_rows(n_grp, halo), :]


def _seq_mixer_kernel(x_hbm, gmix_ref, w_in_ref, cw_ref, cb_ref, wg_ref, ba_ref, bx_ref,
                      lam_ref, wp_ref, ps_ref, w_out_ref, wup32_ref, wdn32_ref,
                      o_hbm, hlast_ref, nconv_ref, npool_ref, wup16_ref, wdn16_ref,
                      xbuf, obuf, in_sem, out_sem, hn_ref, extx_ref, carryx_ref, extp_ref,
                      carryp_ref, xc_ref, xcb_ref, gate_ref, ug_ref, a_ref, b_ref, ymix_ref, hc_ref,
                      *, n_t):
    n_grp = xbuf.shape[1]
    tm = n_grp * SUBLANES
    d = xbuf.shape[3]
    w_lru = a_ref.shape[1]
    w_pool = extp_ref.shape[1]
    hd = w_lru // LRU_HEADS
    half = w_lru // 2
    b = pl.program_id(0)
    t = pl.program_id(1)
    step = b * n_t + t
    n_steps = pl.num_programs(0) * n_t
    slot = step % 2

    @pl.when(step == 0)
    def _first_fetch():
        for c in _tile_copies(x_hbm, xbuf, in_sem, b, t, slot, n_grp, False):
            c.start()

    @pl.when(step + 1 < n_steps)
    def _prefetch():
        nxt = step + 1
        for c in _tile_copies(x_hbm, xbuf, in_sem, nxt // n_t, nxt % n_t, 1 - slot, n_grp, False):
            c.start()

    for c in _tile_copies(x_hbm, xbuf, in_sem, b, t, slot, n_grp, False):
        c.wait()

    @pl.when(t == 0)
    def _reset_state():
        hc_ref[...] = jnp.zeros_like(hc_ref)
        carryx_ref[...] = jnp.zeros_like(carryx_ref)
        carryp_ref[...] = jnp.zeros_like(carryp_ref)

    x_tile = lambda: xbuf[slot].reshape(tm, d)
    hn_ref[...] = _rmsnorm(x_tile(), gmix_ref[...]).astype(BF16)
    in_proj = lambda c0, c1: _dot(hn_ref[...], w_in_ref[:, c0:c1])
    sub = lax.broadcasted_iota(jnp.int32, (SUBLANES, w_lru), 0)
    first_token = (sub == 0) & (t == 0)
    cneg = _neg_c_softplus(lam_ref[...])

    extx_ref[_rows(CONV_HALO, n_grp), :] = in_proj(0, w_lru)
    _fill_halo(extx_ref, carryx_ref, CONV_HALO, n_grp)

    extp_ref[_rows(POOL_HALO, n_grp), 0:w_pool // 2] = in_proj(2 * w_lru, 2 * w_lru + w_pool // 2)
    for gp in range(n_grp // 2):
        g = 2 * gp
        acc = cb_ref[...] + extx_ref[_rows(CONV_HALO + g, 2), :] * cw_ref[CONV_W - 1:CONV_W, :]
        for k in range(CONV_W - 1):
            acc = acc + extx_ref[_rows(g + k, 2), :] * cw_ref[k:k + 1, :]
        xc_ref[_rows(g, 2), :] = acc
        xcb_ref[_rows(g, 2), :] = acc.astype(BF16)
    for hh in range(LRU_HEADS):
        gate_ref[:, 2 * hd * hh:2 * hd * (hh + 1)] = _dot(xcb_ref[:, hd * hh:hd * (hh + 1)], wg_ref[hh])

    extp_ref[_rows(POOL_HALO, n_grp), w_pool // 2:] = in_proj(2 * w_lru + w_pool // 2, 2 * w_lru + w_pool)
    _fill_halo(extp_ref, carryp_ref, POOL_HALO, n_grp)
    ug_ref[:, 0:half] = in_proj(w_lru, w_lru + half)

    def _scan_groups(g0, g1, a_cum, h_loc):
        for g in range(g0, g1):
            gates = gate_ref[_rows(g), :]
            ga = jnp.concatenate([gates[:, 2 * hd * hh:2 * hd * hh + hd] for hh in range(LRU_HEADS)], axis=1)
            gx = jnp.concatenate([gates[:, 2 * hd * hh + hd:2 * hd * (hh + 1)] for hh in range(LRU_HEADS)], axis=1)
            a, xi, mult = _gate_update(jnp.concatenate([ga, gx], axis=1), xc_ref[_rows(g), :],
                                       ba_ref[...], bx_ref[...], cneg)
            if g == 0:
                mult = jnp.where(first_token, 1.0, mult)
            bt = xi * mult
            a_cum = a if a_cum is None else a * a_cum
            h_loc = bt if h_loc is None else a * h_loc + bt
            a_ref[_rows(g), :] = a_cum
            b_ref[_rows(g), :] = h_loc
        return a_cum, h_loc

    a_tot, h_tot = _scan_groups(0, n_grp // 2, None, None)
    ug_ref[:, half:] = in_proj(w_lru + half, 2 * w_lru)
    a_tot, h_tot = _scan_groups(n_grp // 2, n_grp, a_tot, h_tot)

    gd = w_pool // len(POOL_WINDOWS)
    row = lax.broadcasted_iota(jnp.int32, (tm, 1), 0)
    pos = t * tm + (row & (SUBLANES - 1)) * n_grp + (row >> 3)
    pooled = []
    for gi, w in enumerate(POOL_WINDOWS):
        cols = slice(gi * gd, (gi + 1) * gd)
        lo = POOL_HALO - (w - 1)
        s = extp_ref[_rows(lo, n_grp + w - 1), cols]
        width = 1
        while width < w:
            s = s[width * SUBLANES:, :] + s[:-width * SUBLANES, :]
            width *= 2
        inv = 1.0 / jnp.minimum(pos + 1, w).astype(F32)
        pooled.append(s * inv - extp_ref[_rows(POOL_HALO, n_grp), cols])

    y_pool = _pool_project(jnp.concatenate(pooled, axis=1), wp_ref, ps_ref[...])
    ymix_ref[:, w_lru:] = y_pool.astype(BF16)
    obuf[slot] = (x_tile() + _dot(ymix_ref[:, w_lru:], w_out_ref[w_lru:, :])).reshape(n_grp, SUBLANES, d)

    p_cum, q_cum = a_tot, h_tot
    for dd in (1, 2, 4):
        keep = sub >= dd
        q_cum = jnp.where(keep, p_cum * pltpu.roll(q_cum, dd, 0) + q_cum, q_cum)
        p_cum = jnp.where(keep, p_cum * pltpu.roll(p_cum, dd, 0), p_cum)
    h_in = jnp.broadcast_to(hc_ref[...], (SUBLANES, w_lru))
    seg_end = p_cum * h_in + q_cum
    seg_in = jnp.where(sub == 0, h_in, pltpu.roll(seg_end, 1, 0))
    h_end = seg_end[SUBLANES - 1:SUBLANES, :]
    hc_ref[...] = h_end

    seg_in2 = jnp.concatenate([seg_in, seg_in], axis=0)
    for gp in range(n_grp // 2):
        r2 = _rows(2 * gp, 2)
        hs = b_ref[r2, :] + a_ref[r2, :] * seg_in2
        ymix_ref[r2, 0:w_lru] = (hs * jax.nn.gelu(ug_ref[r2, :], approximate=True)).astype(BF16)

    obuf[slot] += _dot(ymix_ref[:, 0:w_lru], w_out_ref[0:w_lru, :]).reshape(n_grp, SUBLANES, d)

    wup16_ref[...] = wup32_ref[...].astype(BF16)
    wdn16_ref[...] = wdn32_ref[...].astype(BF16)

    for c in _tile_copies(o_hbm, obuf, out_sem, b, t, slot, n_grp, True):
        c.start()

    @pl.when(step > 0)
    def _wait_prev_writeback():
        prv = step - 1
        for c in _tile_copies(o_hbm, obuf, out_sem, prv // n_t, prv % n_t, 1 - slot, n_grp, True):
            c.wait()

    @pl.when(step == n_steps - 1)
    def _wait_last_writeback():
        for c in _tile_copies(o_hbm, obuf, out_sem, b, t, slot, n_grp, True):
            c.wait()

    @pl.when(t == n_t - 1)
    def _emit_state():
        last = SUBLANES - 1
        hlast_ref[...] = h_end
        for i in range(CONV_HALO):
            nconv_ref[i:i + 1, :] = carryx_ref[pl.ds(i * SUBLANES + last, 1), :]
        for i in range(POOL_HALO):
            npool_ref[i:i + 1, :] = carryp_ref[pl.ds(i * SUBLANES + last, 1), :]


def _const_spec(shape):
    nd = len(shape)
    return pl.BlockSpec(shape, lambda *_: (0,) * nd, pipeline_mode=pl.Buffered(1))


def _seq_mixer(x, p, w_up, w_down, tm):
    bsz, seq, d = x.shape
    w_lru = p["conv_b"].shape[1]
    w_pool = p["pool_scale"].shape[1]
    weights = (p["norm_mix_g"], p["w_in"], p["conv_w"], p["conv_b"], p["w_gate"], p["b_rg_a"],
               p["b_rg_x"], p["lru_lambda"], p["w_pool"], p["pool_scale"], p["w_out"])
    n_t = seq // tm
    n_steps = bsz * n_t
    n_grp = tm // SUBLANES
    assert n_grp % 2 == 0 and n_grp >= POOL_HALO
    any_spec = pl.BlockSpec(memory_space=pl.ANY)
    state_spec = lambda rows, w: pl.BlockSpec((None, rows, w), lambda b, t: (b, 0, 0))
    slab_spec = lambda w: pl.BlockSpec((w.shape[0] // n_steps, w.shape[1]),
                                       lambda b, t: (b * n_t + t, 0))
    ext_rows = lambda halo: (halo + n_grp) * SUBLANES
    return pl.pallas_call(
        functools.partial(_seq_mixer_kernel, n_t=n_t),
        grid=(bsz, n_t),
        in_specs=([any_spec] + [_const_spec(w.shape) for w in weights]
                  + [slab_spec(w_up), slab_spec(w_down)]),
        out_specs=[any_spec, state_spec(1, w_lru), state_spec(CONV_W - 1, w_lru),
                   state_spec(POOL_BUF, w_pool), slab_spec(w_up), slab_spec(w_down)],
        out_shape=[jax.ShapeDtypeStruct((bsz, seq, d), F32),
                   jax.ShapeDtypeStruct((bsz, 1, w_lru), F32),
                   jax.ShapeDtypeStruct((bsz, CONV_W - 1, w_lru), F32),
                   jax.ShapeDtypeStruct((bsz, POOL_BUF, w_pool), F32),
                   jax.ShapeDtypeStruct(w_up.shape, BF16),
                   jax.ShapeDtypeStruct(w_down.shape, BF16)],
        scratch_shapes=[
            pltpu.VMEM((2, n_grp, SUBLANES, d), F32),
            pltpu.VMEM((2, n_grp, SUBLANES, d), F32),
            pltpu.SemaphoreType.DMA((2,)),
            pltpu.SemaphoreType.DMA((2,)),
            pltpu.VMEM((tm, d), BF16),
            pltpu.VMEM((ext_rows(CONV_HALO), w_lru), F32),
            pltpu.VMEM((CONV_HALO * SUBLANES, w_lru), F32),
            pltpu.VMEM((ext_rows(POOL_HALO), w_pool), F32),
            pltpu.VMEM((POOL_HALO * SUBLANES, w_pool), F32),
            pltpu.VMEM((tm, w_lru), F32),
            pltpu.VMEM((tm, w_lru), BF16),
            pltpu.VMEM((tm, 2 * w_lru), F32),
            pltpu.VMEM((tm, w_lru), F32),
            pltpu.VMEM((tm, w_lru), F32),
            pltpu.VMEM((tm, w_lru), F32),
            pltpu.VMEM((tm, d), BF16),
            pltpu.VMEM((1, w_lru), F32),
        ],
        compiler_params=pltpu.CompilerParams(
            dimension_semantics=("arbitrary", "arbitrary"),
            vmem_limit_bytes=MIX_VMEM_LIMIT),
        name="seq_mixer",
    )(x, *weights, w_up, w_down)


DEC_CHUNK = 512


def _dec_mixer_kernel(x_ref, xcol_ref, h0_ref, cbuf_ref, pbuf_ref, gmix_ref, w_in_ref, cw_ref, cb_ref,
                      wg_ref, ba_ref, bx_ref, lam_ref, wp_ref, ps_ref, w_out_ref,
                      o_ref, hnew_ref, nconv_ref, npool_ref, w_in16_ref, w_out16_ref,
                      hn_ref, proj_ref, ymix_ref, *, n_in):
    j = pl.program_id(0)
    w_lru = h0_ref.shape[1]
    w_pool = ps_ref.shape[1]

    @pl.when(j == 0)
    def _norm():
        hn_ref[...] = _rmsnorm(x_ref[...], gmix_ref[...]).astype(BF16)

    @pl.when(j < n_in)
    def _in_projection():
        w = w_in_ref[...].astype(BF16)
        w_in16_ref[...] = w
        proj_ref[j] = _dot(hn_ref[...], w)

    @pl.when(j == n_in)
    def _mix():
        per = w_lru // DEC_CHUNK
        branch = lambda k: jnp.concatenate([proj_ref[k * per + c] for c in range(per)], axis=1)
        ux, ug, up = branch(0), branch(1), branch(2)

        xc = cb_ref[...] + ux * cw_ref[CONV_W - 1:CONV_W, :]
        for k in range(CONV_W - 1):
            tap = cbuf_ref[:, k, :]
            xc = xc + tap * cw_ref[k:k + 1, :]
            if k > 0:
                nconv_ref[:, k - 1, :] = tap
        nconv_ref[:, CONV_W - 2, :] = ux

        a, mult, xi = _lru_gates(xc, wg_ref, ba_ref[...], bx_ref[...], _neg_c_softplus(lam_ref[...]))
        h_new = a * h0_ref[...] + xi * mult
        hnew_ref[...] = h_new
        ymix_ref[:, 0:w_lru] = (h_new * jax.nn.gelu(ug, approximate=True)).astype(BF16)

        gd = w_pool // len(POOL_WINDOWS)
        s = up
        width = 1
        pooled = []
        for g, w in enumerate(POOL_WINDOWS):
            while width < w:
                k = POOL_BUF - width
                hist = pbuf_ref[:, k, :]
                if k > 0:
                    npool_ref[:, k - 1, :] = hist
                s = s + hist[:, g * gd:]
                width += 1
            count = float(min(PAST_LEN + 1, w))
            pooled.append(s[:, 0:gd] / count - up[:, g * gd:(g + 1) * gd])
            if g + 1 < len(POOL_WINDOWS):
                s = s[:, gd:]
        npool_ref[:, POOL_BUF - 1, :] = up
        y_pool = _pool_project(jnp.concatenate(pooled, axis=1), wp_ref, ps_ref[...])
        ymix_ref[:, w_lru:] = y_pool.astype(BF16)

    @pl.when(j >= n_in)
    def _out_projection():
        w = w_out_ref[...].astype(BF16)
        w_out16_ref[...] = w
        o_ref[...] = xcol_ref[...] + _dot(ymix_ref[...], w)


def _dec_mixer(x, h0, cbuf, pbuf, p, w_in, w_out):
    rows, d = x.shape
    n_in = w_in.shape[1] // DEC_CHUNK
    n_out = w_out.shape[1] // DEC_CHUNK
    small = (p["conv_w"], p["conv_b"], p["w_gate"], p["b_rg_a"], p["b_rg_x"], p["lru_lambda"],
             p["w_pool"], p["pool_scale"])
    full = lambda a: pl.BlockSpec(a.shape, functools.partial(lambda nd, j: (0,) * nd, a.ndim))
    once = lambda a: _const_spec(a.shape)
    in_chunk = lambda j: (0, jnp.minimum(j, n_in - 1))
    out_chunk = lambda j: (0, jnp.maximum(j - n_in, 0))
    return pl.pallas_call(
        functools.partial(_dec_mixer_kernel, n_in=n_in),
        grid=(n_in + n_out,),
        in_specs=([once(x), pl.BlockSpec((rows, DEC_CHUNK), out_chunk), once(h0), once(cbuf), once(pbuf),
                   once(p["norm_mix_g"]), pl.BlockSpec((d, DEC_CHUNK), in_chunk)]
                  + [once(a) for a in small]
                  + [pl.BlockSpec((w_out.shape[0], DEC_CHUNK), out_chunk)]),
        out_specs=[pl.BlockSpec((rows, DEC_CHUNK), out_chunk), full(h0), full(cbuf), full(pbuf),
                   pl.BlockSpec((d, DEC_CHUNK), in_chunk),
                   pl.BlockSpec((w_out.shape[0], DEC_CHUNK), out_chunk)],
        out_shape=[jax.ShapeDtypeStruct(s.shape, F32) for s in (x, h0, cbuf, pbuf)]
                  + [jax.ShapeDtypeStruct(w_in.shape, BF16), jax.ShapeDtypeStruct(w_out.shape, BF16)],
        scratch_shapes=[
            pltpu.VMEM((rows, d), BF16),
            pltpu.VMEM((n_in, rows, DEC_CHUNK), F32),
            pltpu.VMEM((rows, w_out.shape[0]), BF16),
        ],
        compiler_params=pltpu.CompilerParams(
            dimension_semantics=("arbitrary",), vmem_limit_bytes=DEC_VMEM_LIMIT),
        name="dec_mixer",
    )(x, x, h0, cbuf, pbuf, p["norm_mix_g"], w_in, *small, w_out)


def _mlp_kernel(x_ref, g_ref, wup_ref, wdn_ref, gfin_ref, o_ref, h_ref):
    j = pl.program_id(1)

    @pl.when(j == 0)
    def _norm():
        x = x_ref[...]
        h_ref[...] = _rmsnorm(x, g_ref[...]).astype(BF16)
        o_ref[...] = x

    f = jnp.maximum(_dot(h_ref[...], wup_ref[...]), 0.0)
    o_ref[...] += _dot((f * f).astype(BF16), wdn_ref[...])

    @pl.when(j == pl.num_programs(1) - 1)
    def _finish():
        o_ref[...] = _rmsnorm(o_ref[...], gfin_ref[...])


def _mlp(x, p, tm, tf):
    n, d = x.shape
    d_ff = p["w_up"].shape[1]
    vec_spec = pl.BlockSpec((1, d), lambda i, j: (0, 0))
    tok_spec = pl.BlockSpec((tm, d), lambda i, j: (i, 0))
    return pl.pallas_call(
        _mlp_kernel,
        grid=(n // tm, d_ff // tf),
        in_specs=[tok_spec, vec_spec,
                  pl.BlockSpec((d, tf), lambda i, j: (0, j)),
                  pl.BlockSpec((tf, d), lambda i, j: (j, 0)),
                  vec_spec],
        out_specs=tok_spec,
        out_shape=jax.ShapeDtypeStruct((n, d), F32),
        scratch_shapes=[pltpu.VMEM((tm, d), BF16)],
        compiler_params=pltpu.CompilerParams(
            dimension_semantics=("arbitrary", "arbitrary"),
            vmem_limit_bytes=MLP_VMEM_LIMIT),
        name="mlp",
    )(x, p["norm_mlp_g"], p["w_up"], p["w_down"], p["norm_final_g"])


def _layer_params(l, norm_mix_g, w_in, conv_w, conv_b, w_rg_a, b_rg_a, w_rg_x, b_rg_x, lru_lambda,
                  w_pool, pool_scale, w_out, norm_mlp_g, w_up, w_down, norm_final_g):
    row = lambda v: v.reshape(1, -1)
    return {
        "norm_mix_g": row(norm_mix_g[l]),
        "w_in": w_in[l],
        "conv_w": conv_w[l],
        "conv_b": row(conv_b[l]),
        "w_gate": jnp.concatenate([w_rg_a[l], w_rg_x[l]], axis=-1).astype(BF16),
        "b_rg_a": row(b_rg_a[l]),
        "b_rg_x": row(b_rg_x[l]),
        "lru_lambda": row(lru_lambda[l]),
        "w_pool": w_pool[l].astype(BF16),
        "pool_scale": row(pool_scale[l]),
        "w_out": w_out[l],
        "norm_mlp_g": row(norm_mlp_g[l]),
        "w_up": w_up[l],
        "w_down": w_down[l],
        "norm_final_g": row(norm_final_g),
    }


def kernel(x_prompt, x_sample, state_lru_h, state_conv, state_pool, norm_mix_g, w_in, conv_w, conv_b, w_rg_a, b_rg_a, w_rg_x, b_rg_x, lru_lambda, w_pool, pool_scale, w_out, norm_mlp_g, w_up, w_down, norm_final_g):
    depth = w_in.shape[0]
    assert depth == 1, "final RMSNorm is fused into the (single) layer's MLP kernel"
    bp, tp, d = x_prompt.shape
    bs, ts, _ = x_sample.shape
    assert ts == 1 and tp % MIX_TM == 0 and (bp * tp) % MLP_TM == 0
    p = _layer_params(0, norm_mix_g, w_in, conv_w, conv_b, w_rg_a, b_rg_a, w_rg_x, b_rg_x,
                      lru_lambda, w_pool, pool_scale, w_out, norm_mlp_g, w_up, w_down, norm_final_g)

    x1s, hs, cs, ps, p["w_in"], p["w_out"] = _dec_mixer(
        x_sample.reshape(bs, d), state_lru_h[0], state_conv[0], state_pool[0], p, p["w_in"], p["w_out"])
    x1p, hp, cp, pp, p["w_up"], p["w_down"] = _seq_mixer(x_prompt, p, p["w_up"], p["w_down"], MIX_TM)
    yp = _mlp(x1p.reshape(bp * tp, d), p, MLP_TM, MLP_TF).reshape(bp, tp, d)
    ys = _mlp(x1s, p, bs, MLP_TF).reshape(bs, ts, d)

    return (yp, ys, hp.reshape(1, bp, -1), cp[None], pp[None], hs[None], cs[None], ps[None])
```

```python
import functools

import jax
import jax.numpy as jnp
from jax import lax
from jax.experimental import pallas as pl
from jax.experimental.pallas import tpu as pltpu

EPS = 1e-6
LRU_HEADS = 8
LRU_C = 8.0
CONV_W = 4
POOL_WINDOWS = (2, 4, 8, 16)
POOL_BUF = max(POOL_WINDOWS) - 1
PAST_LEN = 16384

SUBLANES = 8
MIB = 1024 * 1024

MIX_TM = 256
MLP_TM = 512
MLP_TF = 2048
MIX_VMEM_LIMIT = 60 * MIB
DEC_VMEM_LIMIT = 56 * MIB
MLP_VMEM_LIMIT = 60 * MIB

F32 = jnp.float32
BF16 = jnp.bfloat16


def _rmsnorm(x, g):
    return (x * lax.rsqrt(jnp.mean(x * x, axis=-1, keepdims=True) + EPS)) * g


def _dot(a, b):
    return jnp.dot(a, b, preferred_element_type=F32)


def _gate_update(g, xc, ba, bx, cneg):
    hd = xc.shape[1]
    r = jax.nn.sigmoid(g[:, :hd] + ba)
    i = jax.nn.sigmoid(g[:, hd:] + bx)
    a = jnp.exp(r * cneg)
    y = 1.0 - a * a
    mult = jnp.where(y > 0.0, y * lax.rsqrt(y), 0.0)
    return a, xc * i, mult


def _lru_gates(xc, wg_ref, ba, bx, cneg):
    hd = xc.shape[1] // LRU_HEADS
    xcb = xc.astype(BF16)
    a_parts, m_parts, xi_parts = [], [], []
    for h in range(LRU_HEADS):
        sl = slice(h * hd, (h + 1) * hd)
        g = _dot(xcb[:, sl], wg_ref[h])
        a, xi, mult = _gate_update(g, xc[:, sl], ba[:, sl], bx[:, sl], cneg[:, sl])
        a_parts.append(a)
        m_parts.append(mult)
        xi_parts.append(xi)
    cat = lambda ps: jnp.concatenate(ps, axis=1)
    return cat(a_parts), cat(m_parts), cat(xi_parts)


def _pool_project(pooled, wp_ref, scale):
    n_g = len(POOL_WINDOWS)
    gd = pooled.shape[1] // n_g
    pb = pooled.astype(BF16)
    outs = [_dot(pb[:, g * gd:(g + 1) * gd], wp_ref[g]) for g in range(n_g)]
    return jnp.concatenate(outs, axis=1) * scale


def _neg_c_softplus(lam):
    return -LRU_C * jax.nn.softplus(-lam)


CONV_HALO = CONV_W - 1
POOL_HALO = POOL_BUF


def _rows(k, n=1):
    return pl.ds(k * SUBLANES, n * SUBLANES)


def _tile_copies(hbm, buf, sem, b, t, slot, n_grp, to_hbm):
    tm = n_grp * SUBLANES
    copies = []
    for s in range(SUBLANES):
        far = hbm.at[b, pl.ds(t * tm + s * n_grp, n_grp), :]
        near = buf.at[slot, :, s, :]
        src, dst = (near, far) if to_hbm else (far, near)
        copies.append(pltpu.make_async_copy(src, dst, sem.at[slot]))
    return copies


def _fill_halo(ext_ref, carry_ref, halo, n_grp):
    sub = lax.broadcasted_iota(jnp.int32, (SUBLANES, ext_ref.shape[1]), 0)
    for j in range(1, halo + 1):
        cur = ext_ref[_rows(halo + n_grp - j), :]
        prev = carry_ref[_rows(halo - j), :]
        ext_ref[_rows(halo - j), :] = pltpu.roll(jnp.where(sub == SUBLANES - 1, prev, cur), 1, 0)
    carry_ref[...] = ext_ref[_rows(n_grp, halo), :]


def _seq_mixer_kernel(x_hbm, gmix_ref, w_in_ref, cw_ref, cb_ref, wg_ref, ba_ref, bx_ref,
                      lam_ref, wp_ref, ps_ref, w_out_ref, wup32_ref, wdn32_ref,
                      o_hbm, hlast_ref, nconv_ref, npool_ref, wup16_ref, wdn16_ref,
                      xbuf, obuf, in_sem, out_sem, hn_ref, extx_ref, carryx_ref, extp_ref,
                      carryp_ref, xc_ref, xcb_ref, gate_ref, ug_ref, a_ref, b_ref, ymix_ref, hc_ref,
                      *, n_t):
    n_grp = xbuf.shape[1]
    tm = n_grp * SUBLANES
    d = xbuf.shape[3]
    w_lru = a_ref.shape[1]
    w_pool = extp_ref.shape[1]
    hd = w_lru // LRU_HEADS
    half = w_lru // 2
    b = pl.program_id(0)
    t = pl.program_id(1)
    step = b * n_t + t
    n_steps = pl.num_programs(0) * n_t
    slot = step % 2

    @pl.when(step == 0)
    def _first_fetch():
        for c in _tile_copies(x_hbm, xbuf, in_sem, b, t, slot, n_grp, False):
            c.start()

    @pl.when(step + 1 < n_steps)
    def _prefetch():
        nxt = step + 1
        for c in _tile_copies(x_hbm, xbuf, in_sem, nxt // n_t, nxt % n_t, 1 - slot, n_grp, False):
            c.start()

    for c in _tile_copies(x_hbm, xbuf, in_sem, b, t, slot, n_grp, False):
        c.wait()

    @pl.when(t == 0)
    def _reset_state():
        hc_ref[...] = jnp.zeros_like(hc_ref)
        carryx_ref[...] = jnp.zeros_like(carryx_ref)
        carryp_ref[...] = jnp.zeros_like(carryp_ref)

    x_tile = lambda: xbuf[slot].reshape(tm, d)
    hn_ref[...] = _rmsnorm(x_tile(), gmix_ref[...]).astype(BF16)
    in_proj = lambda c0, c1: _dot(hn_ref[...], w_in_ref[:, c0:c1])
    sub = lax.broadcasted_iota(jnp.int32, (SUBLANES, w_lru), 0)
    first_token = (sub == 0) & (t == 0)
    cneg = _neg_c_softplus(lam_ref[...])

    extx_ref[_rows(CONV_HALO, n_grp), :] = in_proj(0, w_lru)
    _fill_halo(extx_ref, carryx_ref, CONV_HALO, n_grp)

    extp_ref[_rows(POOL_HALO, n_grp), 0:w_pool // 2] = in_proj(2 * w_lru, 2 * w_lru + w_pool // 2)
    for gp in range(n_grp // 2):
        g = 2 * gp
        acc = cb_ref[...] + extx_ref[_rows(CONV_HALO + g, 2), :] * cw_ref[CONV_W - 1:CONV_W, :]
        for k in range(CONV_W - 1):
            acc = acc + extx_ref[_rows(g + k, 2), :] * cw_ref[k:k + 1, :]
        xc_ref[_rows(g, 2), :] = acc
        xcb_ref[_rows(g, 2), :] = acc.astype(BF16)
    for hh in range(LRU_HEADS):
        gate_ref[:, 2 * hd * hh:2 * hd * (hh + 1)] = _dot(xcb_ref[:, hd * hh:hd * (hh + 1)], wg_ref[hh])

    extp_ref[_rows(POOL_HALO, n_grp), w_pool // 2:] = in_proj(2 * w_lru + w_pool // 2, 2 * w_lru + w_pool)
    _fill_halo(extp_ref, carryp_ref, POOL_HALO, n_grp)
    ug_ref[:, 0:half] = in_proj(w_lru, w_lru + half)

    def _scan_groups(g0, g1, a_cum, h_loc):
        for g in range(g0, g1):
            gates = gate_ref[_rows(g), :]
            ga = jnp.concatenate([gates[:, 2 * hd * hh:2 * hd * hh + hd] for hh in range(LRU_HEADS)], axis=1)
            gx = jnp.concatenate([gates[:, 2 * hd * hh + hd:2 * hd * (hh + 1)] for hh in range(LRU_HEADS)], axis=1)
            a, xi, mult = _gate_update(jnp.concatenate([ga, gx], axis=1), xc_ref[_rows(g), :],
                                       ba_ref[...], bx_ref[...], cneg)
            if g == 0:
                mult = jnp.where(first_token, 1.0, mult)
            bt = xi * mult
            a_cum = a if a_cum is None else a * a_cum
            h_loc = bt if h_loc is None else a * h_loc + bt
            a_ref[_rows(g), :] = a_cum
            b_ref[_rows(g), :] = h_loc
        return a_cum, h_loc

    a_tot, h_tot = _scan_groups(0, n_grp // 2, None, None)
    ug_ref[:, half:] = in_proj(w_lru + half, 2 * w_lru)
    a_tot, h_tot = _scan_groups(n_grp // 2, n_grp, a_tot, h_tot)

    gd = w_pool // len(POOL_WINDOWS)
    row = lax.broadcasted_iota(jnp.int32, (tm, 1), 0)
    pos = t * tm + (row & (SUBLANES - 1)) * n_grp + (row >> 3)
    pooled = []
    for gi, w in enumerate(POOL_WINDOWS):
        cols = slice(gi * gd, (gi + 1) * gd)
        lo = POOL_HALO - (w - 1)
        s = extp_ref[_rows(lo, n_grp + w - 1), cols]
        width = 1
        while width < w:
            s = s[width * SUBLANES:, :] + s[:-width * SUBLANES, :]
            width *= 2
        inv = 1.0 / jnp.minimum(pos + 1, w).astype(F32)
        pooled.append(s * inv - extp_ref[_rows(POOL_HALO, n_grp), cols])

    y_pool = _pool_project(jnp.concatenate(pooled, axis=1), wp_ref, ps_ref[...])
    ymix_ref[:, w_lru:] = y_pool.astype(BF16)
    obuf[slot] = (x_tile() + _dot(ymix_ref[:, w_lru:], w_out_ref[w_lru:, :])).reshape(n_grp, SUBLANES, d)

    p_cum, q_cum = a_tot, h_tot
    for dd in (1, 2, 4):
        keep = sub >= dd
        q_cum = jnp.where(keep, p_cum * pltpu.roll(q_cum, dd, 0) + q_cum, q_cum)
        p_cum = jnp.where(keep, p_cum * pltpu.roll(p_cum, dd, 0), p_cum)
    h_in = jnp.broadcast_to(hc_ref[...], (SUBLANES, w_lru))
    seg_end = p_cum * h_in + q_cum
    seg_in = jnp.where(sub == 0, h_in, pltpu.roll(seg_end, 1, 0))
    h_end = seg_end[SUBLANES - 1:SUBLANES, :]
    hc_ref[...] = h_end

    seg_in2 = jnp.concatenate([seg_in, seg_in], axis=0)
    for gp in range(n_grp // 2):
        r2 = _rows(2 * gp, 2)
        hs = b_ref[r2, :] + a_ref[r2, :] * seg_in2
        ymix_ref[r2, 0:w_lru] = (hs * jax.nn.gelu(ug_ref[r2, :], approximate=True)).astype(BF16)

    obuf[slot] += _dot(ymix_ref[:, 0:w_lru], w_out_ref[0:w_lru, :]).reshape(n_grp, SUBLANES, d)

    wup16_ref[...] = wup32_ref[...].astype(BF16)
    wdn16_ref[...] = wdn32_ref[...].astype(BF16)

    for c in _tile_copies(o_hbm, obuf, out_sem, b, t, slot, n_grp, True):
        c.start()

    @pl.when(step > 0)
    def _wait_prev_writeback():
        prv = step - 1
        for c in _tile_copies(o_hbm, obuf, out_sem, prv // n_t, prv % n_t, 1 - slot, n_grp, True):
            c.wait()

    @pl.when(step == n_steps - 1)
    def _wait_last_writeback():
        for c in _tile_copies(o_hbm, obuf, out_sem, b, t, slot, n_grp, True):
            c.wait()

    @pl.when(t == n_t - 1)
    def _emit_state():
        last = SUBLANES - 1
        hlast_ref[...] = h_end
        for i in range(CONV_HALO):
            nconv_ref[i:i + 1, :] = carryx_ref[pl.ds(i * SUBLANES + last, 1), :]
        for i in range(POOL_HALO):
            npool_ref[i:i + 1, :] = carryp_ref[pl.ds(i * SUBLANES + last, 1), :]


def _const_spec(shape):
    nd = len(shape)
    return pl.BlockSpec(shape, lambda *_: (0,) * nd, pipeline_mode=pl.Buffered(1))


def _seq_mixer(x, p, w_up, w_down, tm):
    bsz, seq, d = x.shape
    w_lru = p["conv_b"].shape[1]
    w_pool = p["pool_scale"].shape[1]
    weights = (p["norm_mix_g"], p["w_in"], p["conv_w"], p["conv_b"], p["w_gate"], p["b_rg_a"],
               p["b_rg_x"], p["lru_lambda"], p["w_pool"], p["pool_scale"], p["w_out"])
    n_t = seq // tm
    n_steps = bsz * n_t
    n_grp = tm // SUBLANES
    assert n_grp % 2 == 0 and n_grp >= POOL_HALO
    any_spec = pl.BlockSpec(memory_space=pl.ANY)
    state_spec = lambda rows, w: pl.BlockSpec((None, rows, w), lambda b, t: (b, 0, 0))
    slab_spec = lambda w: pl.BlockSpec((w.shape[0] // n_steps, w.shape[1]),
                                       lambda b, t: (b * n_t + t, 0))
    ext_rows = lambda halo: (halo + n_grp) * SUBLANES
    return pl.pallas_call(
        functools.partial(_seq_mixer_kernel, n_t=n_t),
        grid=(bsz, n_t),
        in_specs=([any_spec] + [_const_spec(w.shape) for w in weights]
                  + [slab_spec(w_up), slab_spec(w_down)]),
        out_specs=[any_spec, state_spec(1, w_lru), state_spec(CONV_W - 1, w_lru),
                   state_spec(POOL_BUF, w_pool), slab_spec(w_up), slab_spec(w_down)],
        out_shape=[jax.ShapeDtypeStruct((bsz, seq, d), F32),
                   jax.ShapeDtypeStruct((bsz, 1, w_lru), F32),
                   jax.ShapeDtypeStruct((bsz, CONV_W - 1, w_lru), F32),
                   jax.ShapeDtypeStruct((bsz, POOL_BUF, w_pool), F32),
                   jax.ShapeDtypeStruct(w_up.shape, BF16),
                   jax.ShapeDtypeStruct(w_down.shape, BF16)],
        scratch_shapes=[
            pltpu.VMEM((2, n_grp, SUBLANES, d), F32),
            pltpu.VMEM((2, n_grp, SUBLANES, d), F32),
            pltpu.SemaphoreType.DMA((2,)),
            pltpu.SemaphoreType.DMA((2,)),
            pltpu.VMEM((tm, d), BF16),
            pltpu.VMEM((ext_rows(CONV_HALO), w_lru), F32),
            pltpu.VMEM((CONV_HALO * SUBLANES, w_lru), F32),
            pltpu.VMEM((ext_rows(POOL_HALO), w_pool), F32),
            pltpu.VMEM((POOL_HALO * SUBLANES, w_pool), F32),
            pltpu.VMEM((tm, w_lru), F32),
            pltpu.VMEM((tm, w_lru), BF16),
            pltpu.VMEM((tm, 2 * w_lru), F32),
            pltpu.VMEM((tm, w_lru), F32),
            pltpu.VMEM((tm, w_lru), F32),
            pltpu.VMEM((tm, w_lru), F32),
            pltpu.VMEM((tm, d), BF16),
            pltpu.VMEM((1, w_lru), F32),
        ],
        compiler_params=pltpu.CompilerParams(
            dimension_semantics=("arbitrary", "arbitrary"),
            vmem_limit_bytes=MIX_VMEM_LIMIT),
        name="seq_mixer",
    )(x, *weights, w_up, w_down)


DEC_CHUNK = 512


def _dec_mixer_kernel(x_ref, xcol_ref, h0_ref, cbuf_ref, pbuf_ref, gmix_ref, w_in_ref, cw_ref, cb_ref,
                      wg_ref, ba_ref, bx_ref, lam_ref, wp_ref, ps_ref, w_out_ref,
                      o_ref, hnew_ref, nconv_ref, npool_ref, w_in16_ref, w_out16_ref,
                      hn_ref, proj_ref, ymix_ref, *, n_in):
    j = pl.program_id(0)
    w_lru = h0_ref.shape[1]
    w_pool = ps_ref.shape[1]

    @pl.when(j == 0)
    def _norm():
        hn_ref[...] = _rmsnorm(x_ref[...], gmix_ref[...]).astype(BF16)

    @pl.when(j < n_in)
    def _in_projection():
        w = w_in_ref[...].astype(BF16)
        w_in16_ref[...] = w
        proj_ref[j] = _dot(hn_ref[...], w)

    @pl.when(j == n_in)
    def _mix():
        per = w_lru // DEC_CHUNK
        branch = lambda k: jnp.concatenate([proj_ref[k * per + c] for c in range(per)], axis=1)
        ux, ug, up = branch(0), branch(1), branch(2)

        xc = cb_ref[...] + ux * cw_ref[CONV_W - 1:CONV_W, :]
        for k in range(CONV_W - 1):
            tap = cbuf_ref[k]
            xc = xc + tap * cw_ref[k:k + 1, :]
            if k > 0:
                nconv_ref[k - 1] = tap
        nconv_ref[CONV_W - 2] = ux

        a, mult, xi = _lru_gates(xc, wg_ref, ba_ref[...], bx_ref[...], _neg_c_softplus(lam_ref[...]))
        h_new = a * h0_ref[...] + xi * mult
        hnew_ref[...] = h_new
        ymix_ref[:, 0:w_lru] = (h_new * jax.nn.gelu(ug, approximate=True)).astype(BF16)

        gd = w_pool // len(POOL_WINDOWS)
        s = up
        width = 1
        pooled = []
        for g, w in enumerate(POOL_WINDOWS):
            while width < w:
                k = POOL_BUF - width
                hist = pbuf_ref[k]
                if k > 0:
                    npool_ref[k - 1] = hist
                s = s + hist[:, g * gd:]
                width += 1
            count = float(min(PAST_LEN + 1, w))
            pooled.append(s[:, 0:gd] / count - up[:, g * gd:(g + 1) * gd])
            if g + 1 < len(POOL_WINDOWS):
                s = s[:, gd:]
        npool_ref[POOL_BUF - 1] = up
        y_pool = _pool_project(jnp.concatenate(pooled, axis=1), wp_ref, ps_ref[...])
        ymix_ref[:, w_lru:] = y_pool.astype(BF16)

    @pl.when(j >= n_in)
    def _out_projection():
        w = w_out_ref[...].astype(BF16)
        w_out16_ref[...] = w
        o_ref[...] = xcol_ref[...] + _dot(ymix_ref[...], w)


def _dec_mixer(x, h0, cbuf, pbuf, p, w_in, w_out):
    rows, d = x.shape
    n_in = w_in.shape[1] // DEC_CHUNK
    n_out = w_out.shape[1] // DEC_CHUNK
    small = (p["conv_w"], p["conv_b"], p["w_gate"], p["b_rg_a"], p["b_rg_x"], p["lru_lambda"],
             p["w_pool"], p["pool_scale"])
    full = lambda a: pl.BlockSpec(a.shape, functools.partial(lambda nd, j: (0,) * nd, a.ndim))
    once = lambda a: _const_spec(a.shape)
    in_chunk = lambda j: (0, jnp.minimum(j, n_in - 1))
    out_chunk = lambda j: (0, jnp.maximum(j - n_in, 0))
    return pl.pallas_call(
        functools.partial(_dec_mixer_kernel, n_in=n_in),
        grid=(n_in + n_out,),
        in_specs=([once(x), pl.BlockSpec((rows, DEC_CHUNK), out_chunk), once(h0), once(cbuf), once(pbuf),
                   once(p["norm_mix_g"]), pl.BlockSpec((d, DEC_CHUNK), in_chunk)]
                  + [once(a) for a in small]
                  + [pl.BlockSpec((w_out.shape[0], DEC_CHUNK), out_chunk)]),
        out_specs=[pl.BlockSpec((rows, DEC_CHUNK), out_chunk), full(h0), full(cbuf), full(pbuf),
                   pl.BlockSpec((d, DEC_CHUNK), in_chunk),
                   pl.BlockSpec((w_out.shape[0], DEC_CHUNK), out_chunk)],
        out_shape=[jax.ShapeDtypeStruct(s.shape, F32) for s in (x, h0, cbuf, pbuf)]
                  + [jax.ShapeDtypeStruct(w_in.shape, BF16), jax.ShapeDtypeStruct(w_out.shape, BF16)],
        scratch_shapes=[
            pltpu.VMEM((rows, d), BF16),
            pltpu.VMEM((n_in, rows, DEC_CHUNK), F32),
            pltpu.VMEM((rows, w_out.shape[0]), BF16),
        ],
        compiler_params=pltpu.CompilerParams(
            dimension_semantics=("arbitrary",), vmem_limit_bytes=DEC_VMEM_LIMIT),
        name="dec_mixer",
    )(x, x, h0, cbuf, pbuf, p["norm_mix_g"], w_in, *small, w_out)


def _mlp_kernel(x_ref, g_ref, wup_ref, wdn_ref, gfin_ref, o_ref, h_ref):
    j = pl.program_id(1)

    @pl.when(j == 0)
    def _norm():
        x = x_ref[...]
        h_ref[...] = _rmsnorm(x, g_ref[...]).astype(BF16)
        o_ref[...] = x

    f = jnp.maximum(_dot(h_ref[...], wup_ref[...]), 0.0)
    o_ref[...] += _dot((f * f).astype(BF16), wdn_ref[...])

    @pl.when(j == pl.num_programs(1) - 1)
    def _finish():
        o_ref[...] = _rmsnorm(o_ref[...], gfin_ref[...])


def _mlp(x, p, tm, tf):
    n, d = x.shape
    d_ff = p["w_up"].shape[1]
    vec_spec = pl.BlockSpec((1, d), lambda i, j: (0, 0))
    tok_spec = pl.BlockSpec((tm, d), lambda i, j: (i, 0))
    return pl.pallas_call(
        _mlp_kernel,
        grid=(n // tm, d_ff // tf),
        in_specs=[tok_spec, vec_spec,
                  pl.BlockSpec((d, tf), lambda i, j: (0, j)),
                  pl.BlockSpec((tf, d), lambda i, j: (j, 0)),
                  vec_spec],
        out_specs=tok_spec,
        out_shape=jax.ShapeDtypeStruct((n, d), F32),
        scratch_shapes=[pltpu.VMEM((tm, d), BF16)],
        compiler_params=pltpu.CompilerParams(
            dimension_semantics=("arbitrary", "arbitrary"),
            vmem_limit_bytes=MLP_VMEM_LIMIT),
        name="mlp",
    )(x, p["norm_mlp_g"], p["w_up"], p["w_down"], p["norm_final_g"])


def _layer_params(l, norm_mix_g, w_in, conv_w, conv_b, w_rg_a, b_rg_a, w_rg_x, b_rg_x, lru_lambda,
                  w_pool, pool_scale, w_out, norm_mlp_g, w_up, w_down, norm_final_g):
    row = lambda v: v.reshape(1, -1)
    return {
        "norm_mix_g": row(norm_mix_g[l]),
        "w_in": w_in[l],
        "conv_w": conv_w[l],
        "conv_b": row(conv_b[l]),
        "w_gate": jnp.concatenate([w_rg_a[l], w_rg_x[l]], axis=-1).astype(BF16),
        "b_rg_a": row(b_rg_a[l]),
        "b_rg_x": row(b_rg_x[l]),
        "lru_lambda": row(lru_lambda[l]),
        "w_pool": w_pool[l].astype(BF16),
        "pool_scale": row(pool_scale[l]),
        "w_out": w_out[l],
        "norm_mlp_g": row(norm_mlp_g[l]),
        "w_up": w_up[l],
        "w_down": w_down[l],
        "norm_final_g": row(norm_final_g),
    }


def kernel(x_prompt, x_sample, state_lru_h, state_conv, state_pool, norm_mix_g, w_in, conv_w, conv_b, w_rg_a, b_rg_a, w_rg_x, b_rg_x, lru_lambda, w_pool, pool_scale, w_out, norm_mlp_g, w_up, w_down, norm_final_g):
    depth = w_in.shape[0]
    assert depth == 1, "final RMSNorm is fused into the (single) layer's MLP kernel"
    bp, tp, d = x_prompt.shape
    bs, ts, _ = x_sample.shape
    assert ts == 1 and tp % MIX_TM == 0 and (bp * tp) % MLP_TM == 0
    p = _layer_params(0, norm_mix_g, w_in, conv_w, conv_b, w_rg_a, b_rg_a, w_rg_x, b_rg_x,
                      lru_lambda, w_pool, pool_scale, w_out, norm_mlp_g, w_up, w_down, norm_final_g)

    hist_major = lambda s: jnp.transpose(s, (1, 0, 2))
    x1s, hs, cs, ps, p["w_in"], p["w_out"] = _dec_mixer(
        x_sample.reshape(bs, d), state_lru_h[0], hist_major(state_conv[0]), hist_major(state_pool[0]),
        p, p["w_in"], p["w_out"])
    cs, ps = hist_major(cs), hist_major(ps)
    x1p, hp, cp, pp, p["w_up"], p["w_down"] = _seq_mixer(x_prompt, p, p["w_up"], p["w_down"], MIX_TM)
    yp = _mlp(x1p.reshape(bp * tp, d), p, MLP_TM, MLP_TF).reshape(bp, tp, d)
    ys = _mlp(x1s, p, bs, MLP_TF).reshape(bs, ts, d)

    return (yp, ys, hp.reshape(1, bp, -1), cp[None], pp[None], hs[None], cs[None], ps[None])
```

```python
import functools

import jax
import jax.numpy as jnp
from jax import lax
from jax.experimental import pallas as pl
from jax.experimental.pallas import tpu as pltpu

EPS = 1e-6
LRU_HEADS = 8
LRU_C = 8.0
CONV_W = 4
POOL_WINDOWS = (2, 4, 8, 16)
POOL_BUF = max(POOL_WINDOWS) - 1
PAST_LEN = 16384

SUBLANES = 8
MIB = 1024 * 1024

MIX_TM = 256
MLP_TM = 512
MLP_TF = 2048
MIX_VMEM_LIMIT = 60 * MIB
DEC_VMEM_LIMIT = 56 * MIB
MLP_VMEM_LIMIT = 60 * MIB

F32 = jnp.float32
BF16 = jnp.bfloat16


def _rmsnorm(x, g):
    return (x * lax.rsqrt(jnp.mean(x * x, axis=-1, keepdims=True) + EPS)) * g


def _dot(a, b):
    return jnp.dot(a, b, preferred_element_type=F32)


def _gate_update(g, xc, ba, bx, cneg):
    hd = xc.shape[1]
    r = jax.nn.sigmoid(g[:, :hd] + ba)
    i = jax.nn.sigmoid(g[:, hd:] + bx)
    a = jnp.exp(r * cneg)
    y = 1.0 - a * a
    mult = jnp.where(y > 0.0, y * lax.rsqrt(y), 0.0)
    return a, xc * i, mult


def _lru_gates(xc, wg_ref, ba, bx, cneg):
    hd = xc.shape[1] // LRU_HEADS
    xcb = xc.astype(BF16)
    a_parts, m_parts, xi_parts = [], [], []
    for h in range(LRU_HEADS):
        sl = slice(h * hd, (h + 1) * hd)
        g = _dot(xcb[:, sl], wg_ref[h])
        a, xi, mult = _gate_update(g, xc[:, sl], ba[:, sl], bx[:, sl], cneg[:, sl])
        a_parts.append(a)
        m_parts.append(mult)
        xi_parts.append(xi)
    cat = lambda ps: jnp.concatenate(ps, axis=1)
    return cat(a_parts), cat(m_parts), cat(xi_parts)


def _pool_project(pooled, wp_ref, scale):
    n_g = len(POOL_WINDOWS)
    gd = pooled.shape[1] // n_g
    pb = pooled.astype(BF16)
    outs = [_dot(pb[:, g * gd:(g + 1) * gd], wp_ref[g]) for g in range(n_g)]
    return jnp.concatenate(outs, axis=1) * scale


def _neg_c_softplus(lam):
    return -LRU_C * jax.nn.softplus(-lam)


CONV_HALO = CONV_W - 1
POOL_HALO = POOL_BUF


def _rows(k, n=1):
    return pl.ds(k * SUBLANES, n * SUBLANES)


def _tile_copies(hbm, buf, sem, b, t, slot, n_grp, to_hbm):
    tm = n_grp * SUBLANES
    copies = []
    for s in range(SUBLANES):
        far = hbm.at[b, pl.ds(t * tm + s * n_grp, n_grp), :]
        near = buf.at[slot, :, s, :]
        src, dst = (near, far) if to_hbm else (far, near)
        copies.append(pltpu.make_async_copy(src, dst, sem.at[slot]))
    return copies


def _fill_halo(ext_ref, carry_ref, halo, n_grp):
    sub = lax.broadcasted_iota(jnp.int32, (SUBLANES, ext_ref.shape[1]), 0)
    for j in range(1, halo + 1):
        cur = ext_ref[_rows(halo + n_grp - j), :]
        prev = carry_ref[_rows(halo - j), :]
        ext_ref[_rows(halo - j), :] = pltpu.roll(jnp.where(sub == SUBLANES - 1, prev, cur), 1, 0)
    carry_ref[...] = ext_ref[_rows(n_grp, halo), :]


def _seq_mixer_kernel(x_hbm, gmix_ref, w_in_ref, cw_ref, cb_ref, wg_ref, ba_ref, bx_ref,
                      lam_ref, wp_ref, ps_ref, w_out_ref, wup32_ref, wdn32_ref,
                      o_hbm, hlast_ref, nconv_ref, npool_ref, wup16_ref, wdn16_ref,
                      xbuf, obuf, in_sem, out_sem, hn_ref, extx_ref, carryx_ref, extp_ref,
                      carryp_ref, xc_ref, xcb_ref, gate_ref, ug_ref, a_ref, b_ref, ymix_ref, hc_ref,
                      *, n_t):
    n_grp = xbuf.shape[1]
    tm = n_grp * SUBLANES
    d = xbuf.shape[3]
    w_lru = a_ref.shape[1]
    w_pool = extp_ref.shape[1]
    hd = w_lru // LRU_HEADS
    half = w_lru // 2
    b = pl.program_id(0)
    t = pl.program_id(1)
    step = b * n_t + t
    n_steps = pl.num_programs(0) * n_t
    slot = step % 2

    @pl.when(step == 0)
    def _first_fetch():
        for c in _tile_copies(x_hbm, xbuf, in_sem, b, t, slot, n_grp, False):
            c.start()

    @pl.when(step + 1 < n_steps)
    def _prefetch():
        nxt = step + 1
        for c in _tile_copies(x_hbm, xbuf, in_sem, nxt // n_t, nxt % n_t, 1 - slot, n_grp, False):
            c.start()

    for c in _tile_copies(x_hbm, xbuf, in_sem, b, t, slot, n_grp, False):
        c.wait()

    @pl.when(t == 0)
    def _reset_state():
        hc_ref[...] = jnp.zeros_like(hc_ref)
        carryx_ref[...] = jnp.zeros_like(carryx_ref)
        carryp_ref[...] = jnp.zeros_like(carryp_ref)

    x_tile = lambda: xbuf[slot].reshape(tm, d)
    hn_ref[...] = _rmsnorm(x_tile(), gmix_ref[...]).astype(BF16)
    in_proj = lambda c0, c1: _dot(hn_ref[...], w_in_ref[:, c0:c1])
    sub = lax.broadcasted_iota(jnp.int32, (SUBLANES, w_lru), 0)
    first_token = (sub == 0) & (t == 0)
    cneg = _neg_c_softplus(lam_ref[...])

    extx_ref[_rows(CONV_HALO, n_grp), :] = in_proj(0, w_lru)
    _fill_halo(extx_ref, carryx_ref, CONV_HALO, n_grp)

    extp_ref[_rows(POOL_HALO, n_grp), 0:w_pool // 2] = in_proj(2 * w_lru, 2 * w_lru + w_pool // 2)
    for gp in range(n_grp // 2):
        g = 2 * gp
        acc = cb_ref[...] + extx_ref[_rows(CONV_HALO + g, 2), :] * cw_ref[CONV_W - 1:CONV_W, :]
        for k in range(CONV_W - 1):
            acc = acc + extx_ref[_rows(g + k, 2), :] * cw_ref[k:k + 1, :]
        xc_ref[_rows(g, 2), :] = acc
        xcb_ref[_rows(g, 2), :] = acc.astype(BF16)
    for hh in range(LRU_HEADS):
        gate_ref[:, 2 * hd * hh:2 * hd * (hh + 1)] = _dot(xcb_ref[:, hd * hh:hd * (hh + 1)], wg_ref[hh])

    extp_ref[_rows(POOL_HALO, n_grp), w_pool // 2:] = in_proj(2 * w_lru + w_pool // 2, 2 * w_lru + w_pool)
    _fill_halo(extp_ref, carryp_ref, POOL_HALO, n_grp)
    ug_ref[:, 0:half] = in_proj(w_lru, w_lru + half)

    def _scan_groups(g0, g1, a_cum, h_loc):
        for g in range(g0, g1):
            gates = gate_ref[_rows(g), :]
            ga = jnp.concatenate([gates[:, 2 * hd * hh:2 * hd * hh + hd] for hh in range(LRU_HEADS)], axis=1)
            gx = jnp.concatenate([gates[:, 2 * hd * hh + hd:2 * hd * (hh + 1)] for hh in range(LRU_HEADS)], axis=1)
            a, xi, mult = _gate_update(jnp.concatenate([ga, gx], axis=1), xc_ref[_rows(g), :],
                                       ba_ref[...], bx_ref[...], cneg)
            if g == 0:
                mult = jnp.where(first_token, 1.0, mult)
            bt = xi * mult
            a_cum = a if a_cum is None else a * a_cum
            h_loc = bt if h_loc is None else a * h_loc + bt
            a_ref[_rows(g), :] = a_cum
            b_ref[_rows(g), :] = h_loc
        return a_cum, h_loc

    a_tot, h_tot = _scan_groups(0, n_grp // 2, None, None)
    ug_ref[:, half:] = in_proj(w_lru + half, 2 * w_lru)
    a_tot, h_tot = _scan_groups(n_grp // 2, n_grp, a_tot, h_tot)

    gd = w_pool // len(POOL_WINDOWS)
    row = lax.broadcasted_iota(jnp.int32, (tm, 1), 0)
    pos = t * tm + (row & (SUBLANES - 1)) * n_grp + (row >> 3)
    pooled = []
    for gi, w in enumerate(POOL_WINDOWS):
        cols = slice(gi * gd, (gi + 1) * gd)
        lo = POOL_HALO - (w - 1)
        s = extp_ref[_rows(lo, n_grp + w - 1), cols]
        width = 1
        while width < w:
            s = s[width * SUBLANES:, :] + s[:-width * SUBLANES, :]
            width *= 2
        inv = 1.0 / jnp.minimum(pos + 1, w).astype(F32)
        pooled.append(s * inv - extp_ref[_rows(POOL_HALO, n_grp), cols])

    y_pool = _pool_project(jnp.concatenate(pooled, axis=1), wp_ref, ps_ref[...])
    ymix_ref[:, w_lru:] = y_pool.astype(BF16)
    obuf[slot] = (x_tile() + _dot(ymix_ref[:, w_lru:], w_out_ref[w_lru:, :])).reshape(n_grp, SUBLANES, d)

    p_cum, q_cum = a_tot, h_tot
    for dd in (1, 2, 4):
        keep = sub >= dd
        q_cum = jnp.where(keep, p_cum * pltpu.roll(q_cum, dd, 0) + q_cum, q_cum)
        p_cum = jnp.where(keep, p_cum * pltpu.roll(p_cum, dd, 0), p_cum)
    h_in = jnp.broadcast_to(hc_ref[...], (SUBLANES, w_lru))
    seg_end = p_cum * h_in + q_cum
    seg_in = jnp.where(sub == 0, h_in, pltpu.roll(seg_end, 1, 0))
    h_end = seg_end[SUBLANES - 1:SUBLANES, :]
    hc_ref[...] = h_end

    seg_in2 = jnp.concatenate([seg_in, seg_in], axis=0)
    for gp in range(n_grp // 2):
        r2 = _rows(2 * gp, 2)
        hs = b_ref[r2, :] + a_ref[r2, :] * seg_in2
        ymix_ref[r2, 0:w_lru] = (hs * jax.nn.gelu(ug_ref[r2, :], approximate=True)).astype(BF16)

    obuf[slot] += _dot(ymix_ref[:, 0:w_lru], w_out_ref[0:w_lru, :]).reshape(n_grp, SUBLANES, d)

    wup16_ref[...] = wup32_ref[...].astype(BF16)
    wdn16_ref[...] = wdn32_ref[...].astype(BF16)

    for c in _tile_copies(o_hbm, obuf, out_sem, b, t, slot, n_grp, True):
        c.start()

    @pl.when(step > 0)
    def _wait_prev_writeback():
        prv = step - 1
        for c in _tile_copies(o_hbm, obuf, out_sem, prv // n_t, prv % n_t, 1 - slot, n_grp, True):
            c.wait()

    @pl.when(step == n_steps - 1)
    def _wait_last_writeback():
        for c in _tile_copies(o_hbm, obuf, out_sem, b, t, slot, n_grp, True):
            c.wait()

    @pl.when(t == n_t - 1)
    def _emit_state():
        last = SUBLANES - 1
        hlast_ref[...] = h_end
        for i in range(CONV_HALO):
            nconv_ref[i:i + 1, :] = carryx_ref[pl.ds(i * SUBLANES + last, 1), :]
        for i in range(POOL_HALO):
            npool_ref[i:i + 1, :] = carryp_ref[pl.ds(i * SUBLANES + last, 1), :]


def _const_spec(shape):
    nd = len(shape)
    return pl.BlockSpec(shape, lambda *_: (0,) * nd, pipeline_mode=pl.Buffered(1))


def _seq_mixer(x, p, w_up, w_down, tm):
    bsz, seq, d = x.shape
    w_lru = p["conv_b"].shape[1]
    w_pool = p["pool_scale"].shape[1]
    weights = (p["norm_mix_g"], p["w_in"], p["conv_w"], p["conv_b"], p["w_gate"], p["b_rg_a"],
               p["b_rg_x"], p["lru_lambda"], p["w_pool"], p["pool_scale"], p["w_out"])
    n_t = seq // tm
    n_steps = bsz * n_t
    n_grp = tm // SUBLANES
    assert n_grp % 2 == 0 and n_grp >= POOL_HALO
    any_spec = pl.BlockSpec(memory_space=pl.ANY)
    state_spec = lambda rows, w: pl.BlockSpec((None, rows, w), lambda b, t: (b, 0, 0))
    slab_spec = lambda w: pl.BlockSpec((w.shape[0] // n_steps, w.shape[1]),
                                       lambda b, t: (b * n_t + t, 0))
    ext_rows = lambda halo: (halo + n_grp) * SUBLANES
    return pl.pallas_call(
        functools.partial(_seq_mixer_kernel, n_t=n_t),
        grid=(bsz, n_t),
        in_specs=([any_spec] + [_const_spec(w.shape) for w in weights]
                  + [slab_spec(w_up), slab_spec(w_down)]),
        out_specs=[any_spec, state_spec(1, w_lru), state_spec(CONV_W - 1, w_lru),
                   state_spec(POOL_BUF, w_pool), slab_spec(w_up), slab_spec(w_down)],
        out_shape=[jax.ShapeDtypeStruct((bsz, seq, d), F32),
                   jax.ShapeDtypeStruct((bsz, 1, w_lru), F32),
                   jax.ShapeDtypeStruct((bsz, CONV_W - 1, w_lru), F32),
                   jax.ShapeDtypeStruct((bsz, POOL_BUF, w_pool), F32),
                   jax.ShapeDtypeStruct(w_up.shape, BF16),
                   jax.ShapeDtypeStruct(w_down.shape, BF16)],
        scratch_shapes=[
            pltpu.VMEM((2, n_grp, SUBLANES, d), F32),
            pltpu.VMEM((2, n_grp, SUBLANES, d), F32),
            pltpu.SemaphoreType.DMA((2,)),
            pltpu.SemaphoreType.DMA((2,)),
            pltpu.VMEM((tm, d), BF16),
            pltpu.VMEM((ext_rows(CONV_HALO), w_lru), F32),
            pltpu.VMEM((CONV_HALO * SUBLANES, w_lru), F32),
            pltpu.VMEM((ext_rows(POOL_HALO), w_pool), F32),
            pltpu.VMEM((POOL_HALO * SUBLANES, w_pool), F32),
            pltpu.VMEM((tm, w_lru), F32),
            pltpu.VMEM((tm, w_lru), BF16),
            pltpu.VMEM((tm, 2 * w_lru), F32),
            pltpu.VMEM((tm, w_lru), F32),
            pltpu.VMEM((tm, w_lru), F32),
            pltpu.VMEM((tm, w_lru), F32),
            pltpu.VMEM((tm, d), BF16),
            pltpu.VMEM((1, w_lru), F32),
        ],
        compiler_params=pltpu.CompilerParams(
            dimension_semantics=("arbitrary", "arbitrary"),
            vmem_limit_bytes=MIX_VMEM_LIMIT),
        name="seq_mixer",
    )(x, *weights, w_up, w_down)


DEC_CHUNK = 512


def _dec_mixer_kernel(x_ref, xcol_ref, h0_ref, cbuf_ref, pbuf_ref, gmix_ref, w_in_ref, cw_ref, cb_ref,
                      wg_ref, ba_ref, bx_ref, lam_ref, wp_ref, ps_ref, w_out_ref,
                      o_ref, hnew_ref, nconv_ref, npool_ref, w_in16_ref, w_out16_ref,
                      hn_ref, proj_ref, ymix_ref, *, n_in):
    j = pl.program_id(0)
    w_lru = h0_ref.shape[1]
    w_pool = ps_ref.shape[1]

    @pl.when(j == 0)
    def _norm():
        hn_ref[...] = _rmsnorm(x_ref[...], gmix_ref[...]).astype(BF16)

    @pl.when(j < n_in)
    def _in_projection():
        w = w_in_ref[...].astype(BF16)
        w_in16_ref[...] = w
        proj_ref[j] = _dot(hn_ref[...], w)

    @pl.when(j == n_in)
    def _mix():
        per = w_lru // DEC_CHUNK
        branch = lambda k: jnp.concatenate([proj_ref[k * per + c] for c in range(per)], axis=1)
        ux, ug, up = branch(0), branch(1), branch(2)

        xc = cb_ref[...] + ux * cw_ref[CONV_W - 1:CONV_W, :]
        for k in range(CONV_W - 1):
            tap = cbuf_ref[k]
            xc = xc + tap * cw_ref[k:k + 1, :]
            if k > 0:
                nconv_ref[k - 1] = tap
        nconv_ref[CONV_W - 2] = ux

        a, mult, xi = _lru_gates(xc, wg_ref, ba_ref[...], bx_ref[...], _neg_c_softplus(lam_ref[...]))
        h_new = a * h0_ref[...] + xi * mult
        hnew_ref[...] = h_new
        ymix_ref[:, 0:w_lru] = (h_new * jax.nn.gelu(ug, approximate=True)).astype(BF16)

        gd = w_pool // len(POOL_WINDOWS)
        s = up
        width = 1
        pooled = []
        for g, w in enumerate(POOL_WINDOWS):
            while width < w:
                k = POOL_BUF - width
                hist = pbuf_ref[k]
                if k > 0:
                    npool_ref[k - 1] = hist
                s = s + hist[:, g * gd:]
                width += 1
            count = float(min(PAST_LEN + 1, w))
            pooled.append(s[:, 0:gd] / count - up[:, g * gd:(g + 1) * gd])
            if g + 1 < len(POOL_WINDOWS):
                s = s[:, gd:]
        npool_ref[POOL_BUF - 1] = up
        y_pool = _pool_project(jnp.concatenate(pooled, axis=1), wp_ref, ps_ref[...])
        ymix_ref[:, w_lru:] = y_pool.astype(BF16)

    @pl.when(j >= n_in)
    def _out_projection():
        w = w_out_ref[...].astype(BF16)
        w_out16_ref[...] = w
        o_ref[...] = xcol_ref[...] + _dot(ymix_ref[...], w)


def _dec_mixer(x, h0, cbuf, pbuf, p, w_in, w_out):
    rows, d = x.shape
    n_in = w_in.shape[1] // DEC_CHUNK
    n_out = w_out.shape[1] // DEC_CHUNK
    small = (p["conv_w"], p["conv_b"], p["w_gate"], p["b_rg_a"], p["b_rg_x"], p["lru_lambda"],
             p["w_pool"], p["pool_scale"])
    full = lambda a: pl.BlockSpec(a.shape, functools.partial(lambda nd, j: (0,) * nd, a.ndim))
    once = lambda a: _const_spec(a.shape)
    in_chunk = lambda j: (0, jnp.minimum(j, n_in - 1))
    out_chunk = lambda j: (0, jnp.maximum(j - n_in, 0))
    return pl.pallas_call(
        functools.partial(_dec_mixer_kernel, n_in=n_in),
        grid=(n_in + n_out,),
        in_specs=([once(x), pl.BlockSpec((rows, DEC_CHUNK), out_chunk), once(h0), once(cbuf), once(pbuf),
                   once(p["norm_mix_g"]), pl.BlockSpec((d, DEC_CHUNK), in_chunk)]
                  + [once(a) for a in small]
                  + [pl.BlockSpec((w_out.shape[0], DEC_CHUNK), out_chunk)]),
        out_specs=[pl.BlockSpec((rows, DEC_CHUNK), out_chunk), full(h0), full(cbuf), full(pbuf),
                   pl.BlockSpec((d, DEC_CHUNK), in_chunk),
                   pl.BlockSpec((w_out.shape[0], DEC_CHUNK), out_chunk)],
        out_shape=[jax.ShapeDtypeStruct(s.shape, F32) for s in (x, h0, cbuf, pbuf)]
                  + [jax.ShapeDtypeStruct(w_in.shape, BF16), jax.ShapeDtypeStruct(w_out.shape, BF16)],
        scratch_shapes=[
            pltpu.VMEM((rows, d), BF16),
            pltpu.VMEM((n_in, rows, DEC_CHUNK), F32),
            pltpu.VMEM((rows, w_out.shape[0]), BF16),
        ],
        compiler_params=pltpu.CompilerParams(
            dimension_semantics=("arbitrary",), vmem_limit_bytes=DEC_VMEM_LIMIT),
        name="dec_mixer",
    )(x, x, h0, cbuf, pbuf, p["norm_mix_g"], w_in, *small, w_out)


def _mlp_kernel(x_ref, xs_ref, g_ref, wup_ref, wdn_ref, gfin_ref, o_ref, os_ref, h_ref):
    tm = x_ref.shape[0]
    i = pl.program_id(0)
    j = pl.program_id(1)
    first = j == 0
    last = j == pl.num_programs(1) - 1
    with_decode = i == 0

    def _start(src_ref, rows, acc_ref):
        x = src_ref[...]
        h_ref[rows, :] = _rmsnorm(x, g_ref[...]).astype(BF16)
        acc_ref[...] = x

    def _ffn(rows):
        f = jnp.maximum(_dot(h_ref[rows, :], wup_ref[...]), 0.0)
        return _dot((f * f).astype(BF16), wdn_ref[...])

    pl.when(first)(lambda: _start(x_ref, pl.ds(0, tm), o_ref))
    pl.when(first & with_decode)(lambda: _start(xs_ref, pl.ds(tm, xs_ref.shape[0]), os_ref))

    @pl.when(with_decode)
    def _both_streams():
        part = _ffn(pl.ds(0, h_ref.shape[0]))
        o_ref[...] += part[:tm]
        os_ref[...] += part[tm:]

    @pl.when(jnp.logical_not(with_decode))
    def _prompt_only():
        o_ref[...] += _ffn(pl.ds(0, tm))

    @pl.when(last)
    def _finish():
        o_ref[...] = _rmsnorm(o_ref[...], gfin_ref[...])

    @pl.when(last & with_decode)
    def _finish_decode():
        os_ref[...] = _rmsnorm(os_ref[...], gfin_ref[...])


def _mlp(x, xs, p, tm, tf):
    n, d = x.shape
    ns = xs.shape[0]
    d_ff = p["w_up"].shape[1]
    vec_spec = pl.BlockSpec((1, d), lambda i, j: (0, 0))
    tok_spec = pl.BlockSpec((tm, d), lambda i, j: (i, 0))
    dec_spec = pl.BlockSpec((ns, d), lambda i, j: (0, 0))
    return pl.pallas_call(
        _mlp_kernel,
        grid=(n // tm, d_ff // tf),
        in_specs=[tok_spec, dec_spec, vec_spec,
                  pl.BlockSpec((d, tf), lambda i, j: (0, j)),
                  pl.BlockSpec((tf, d), lambda i, j: (j, 0)),
                  vec_spec],
        out_specs=[tok_spec, dec_spec],
        out_shape=[jax.ShapeDtypeStruct((n, d), F32), jax.ShapeDtypeStruct((ns, d), F32)],
        scratch_shapes=[pltpu.VMEM((tm + ns, d), BF16)],
        compiler_params=pltpu.CompilerParams(
            dimension_semantics=("arbitrary", "arbitrary"),
            vmem_limit_bytes=MLP_VMEM_LIMIT),
        name="mlp",
    )(x, xs, p["norm_mlp_g"], p["w_up"], p["w_down"], p["norm_final_g"])


def _layer_params(l, norm_mix_g, w_in, conv_w, conv_b, w_rg_a, b_rg_a, w_rg_x, b_rg_x, lru_lambda,
                  w_pool, pool_scale, w_out, norm_mlp_g, w_up, w_down, norm_final_g):
    row = lambda v: v.reshape(1, -1)
    return {
        "norm_mix_g": row(norm_mix_g[l]),
        "w_in": w_in[l],
        "conv_w": conv_w[l],
        "conv_b": row(conv_b[l]),
        "w_gate": jnp.concatenate([w_rg_a[l], w_rg_x[l]], axis=-1).astype(BF16),
        "b_rg_a": row(b_rg_a[l]),
        "b_rg_x": row(b_rg_x[l]),
        "lru_lambda": row(lru_lambda[l]),
        "w_pool": w_pool[l].astype(BF16),
        "pool_scale": row(pool_scale[l]),
        "w_out": w_out[l],
        "norm_mlp_g": row(norm_mlp_g[l]),
        "w_up": w_up[l],
        "w_down": w_down[l],
        "norm_final_g": row(norm_final_g),
    }


def kernel(x_prompt, x_sample, state_lru_h, state_conv, state_pool, norm_mix_g, w_in, conv_w, conv_b, w_rg_a, b_rg_a, w_rg_x, b_rg_x, lru_lambda, w_pool, pool_scale, w_out, norm_mlp_g, w_up, w_down, norm_final_g):
    depth = w_in.shape[0]
    assert depth == 1, "final RMSNorm is fused into the (single) layer's MLP kernel"
    bp, tp, d = x_prompt.shape
    bs, ts, _ = x_sample.shape
    assert ts == 1 and tp % MIX_TM == 0 and (bp * tp) % MLP_TM == 0
    p = _layer_params(0, norm_mix_g, w_in, conv_w, conv_b, w_rg_a, b_rg_a, w_rg_x, b_rg_x,
                      lru_lambda, w_pool, pool_scale, w_out, norm_mlp_g, w_up, w_down, norm_final_g)

    hist_major = lambda s: jnp.transpose(s, (1, 0, 2))
    x1s, hs, cs, ps, p["w_in"], p["w_out"] = _dec_mixer(
        x_sample.reshape(bs, d), state_lru_h[0], hist_major(state_conv[0]), hist_major(state_pool[0]),
        p, p["w_in"], p["w_out"])
    cs, ps = hist_major(cs), hist_major(ps)
    x1p, hp, cp, pp, p["w_up"], p["w_down"] = _seq_mixer(x_prompt, p, p["w_up"], p["w_down"], MIX_TM)
    yp, ys = _mlp(x1p.reshape(bp * tp, d), x1s, p, MLP_TM, MLP_TF)
    yp, ys = yp.reshape(bp, tp, d), ys.reshape(bs, ts, d)

    return (yp, ys, hp.reshape(1, bp, -1), cp[None], pp[None], hs[None], cs[None], ps[None])
```

```python
import functools

import jax
import jax.numpy as jnp
from jax import lax
from jax.experimental import pallas as pl
from jax.experimental.pallas import tpu as pltpu

EPS = 1e-6
LRU_HEADS = 8
LRU_C = 8.0
CONV_W = 4
POOL_WINDOWS = (2, 4, 8, 16)
POOL_BUF = max(POOL_WINDOWS) - 1
PAST_LEN = 16384

SUBLANES = 8
MIB = 1024 * 1024

MIX_TM = 256
MLP_TM = 512
MLP_TF = 2048
MIX_VMEM_LIMIT = 60 * MIB
DEC_VMEM_LIMIT = 56 * MIB
MLP_VMEM_LIMIT = 60 * MIB

F32 = jnp.float32
BF16 = jnp.bfloat16


def _rmsnorm(x, g):
    return (x * lax.rsqrt(jnp.mean(x * x, axis=-1, keepdims=True) + EPS)) * g


def _dot(a, b):
    return jnp.dot(a, b, preferred_element_type=F32)


def _gate_update(g, xc, ba, bx, cneg):
    hd = xc.shape[1]
    r = jax.nn.sigmoid(g[:, :hd] + ba)
    i = jax.nn.sigmoid(g[:, hd:] + bx)
    a = jnp.exp(r * cneg)
    y = 1.0 - a * a
    mult = jnp.where(y > 0.0, y * lax.rsqrt(y), 0.0)
    return a, xc * i, mult


def _lru_gates(xc, wg_ref, ba, bx, cneg):
    hd = xc.shape[1] // LRU_HEADS
    xcb = xc.astype(BF16)
    a_parts, m_parts, xi_parts = [], [], []
    for h in range(LRU_HEADS):
        sl = slice(h * hd, (h + 1) * hd)
        g = _dot(xcb[:, sl], wg_ref[h])
        a, xi, mult = _gate_update(g, xc[:, sl], ba[:, sl], bx[:, sl], cneg[:, sl])
        a_parts.append(a)
        m_parts.append(mult)
        xi_parts.append(xi)
    cat = lambda ps: jnp.concatenate(ps, axis=1)
    return cat(a_parts), cat(m_parts), cat(xi_parts)


def _pool_project(pooled, wp_ref, scale):
    n_g = len(POOL_WINDOWS)
    gd = pooled.shape[1] // n_g
    pb = pooled.astype(BF16)
    outs = [_dot(pb[:, g * gd:(g + 1) * gd], wp_ref[g]) for g in range(n_g)]
    return jnp.concatenate(outs, axis=1) * scale


def _neg_c_softplus(lam):
    return -LRU_C * jax.nn.softplus(-lam)


CONV_HALO = CONV_W - 1
POOL_HALO = POOL_BUF


def _rows(k, n=1):
    return pl.ds(k * SUBLANES, n * SUBLANES)


def _tile_copies(hbm, buf, sem, b, t, slot, n_grp, to_hbm):
    tm = n_grp * SUBLANES
    copies = []
    for s in range(SUBLANES):
        far = hbm.at[b, pl.ds(t * tm + s * n_grp, n_grp), :]
        near = buf.at[slot, :, s, :]
        src, dst = (near, far) if to_hbm else (far, near)
        copies.append(pltpu.make_async_copy(src, dst, sem.at[slot]))
    return copies


def _fill_halo(ext_ref, carry_ref, halo, n_grp):
    sub = lax.broadcasted_iota(jnp.int32, (SUBLANES, ext_ref.shape[1]), 0)
    for j in range(1, halo + 1):
        cur = ext_ref[_rows(halo + n_grp - j), :]
        prev = carry_ref[_rows(halo - j), :]
        ext_ref[_rows(halo - j), :] = pltpu.roll(jnp.where(sub == SUBLANES - 1, prev, cur), 1, 0)
    carry_ref[...] = ext_ref[_rows(n_grp, halo), :]


def _seq_mixer_kernel(x_hbm, gmix_ref, w_in_ref, cw_ref, cb_ref, wg_ref, ba_ref, bx_ref,
                      lam_ref, wp_ref, ps_ref, w_out_ref, wup32_ref, wdn32_ref,
                      o_hbm, hlast_ref, nconv_ref, npool_ref, wup16_ref, wdn16_ref,
                      xbuf, obuf, in_sem, out_sem, hn_ref, extx_ref, carryx_ref, extp_ref,
                      carryp_ref, xc_ref, xcb_ref, gate_ref, ug_ref, a_ref, b_ref, ymix_ref, hc_ref,
                      *, n_t):
    n_grp = xbuf.shape[1]
    tm = n_grp * SUBLANES
    d = xbuf.shape[3]
    w_lru = a_ref.shape[1]
    w_pool = extp_ref.shape[1]
    hd = w_lru // LRU_HEADS
    half = w_lru // 2
    b = pl.program_id(0)
    t = pl.program_id(1)
    step = b * n_t + t
    n_steps = pl.num_programs(0) * n_t
    slot = step % 2

    @pl.when(step == 0)
    def _first_fetch():
        for c in _tile_copies(x_hbm, xbuf, in_sem, b, t, slot, n_grp, False):
            c.start()

    @pl.when(step + 1 < n_steps)
    def _prefetch():
        nxt = step + 1
        for c in _tile_copies(x_hbm, xbuf, in_sem, nxt // n_t, nxt % n_t, 1 - slot, n_grp, False):
            c.start()

    for c in _tile_copies(x_hbm, xbuf, in_sem, b, t, slot, n_grp, False):
        c.wait()

    @pl.when(t == 0)
    def _reset_state():
        hc_ref[...] = jnp.zeros_like(hc_ref)
        carryx_ref[...] = jnp.zeros_like(carryx_ref)
        carryp_ref[...] = jnp.zeros_like(carryp_ref)

    x_tile = lambda: xbuf[slot].reshape(tm, d)
    hn_ref[...] = _rmsnorm(x_tile(), gmix_ref[...]).astype(BF16)
    in_proj = lambda c0, c1: _dot(hn_ref[...], w_in_ref[:, c0:c1])
    sub = lax.broadcasted_iota(jnp.int32, (SUBLANES, w_lru), 0)
    first_token = (sub == 0) & (t == 0)
    cneg = _neg_c_softplus(lam_ref[...])

    extx_ref[_rows(CONV_HALO, n_grp), :] = in_proj(0, w_lru)
    _fill_halo(extx_ref, carryx_ref, CONV_HALO, n_grp)

    extp_ref[_rows(POOL_HALO, n_grp), 0:w_pool // 2] = in_proj(2 * w_lru, 2 * w_lru + w_pool // 2)
    for gp in range(n_grp // 2):
        g = 2 * gp
        acc = cb_ref[...] + extx_ref[_rows(CONV_HALO + g, 2), :] * cw_ref[CONV_W - 1:CONV_W, :]
        for k in range(CONV_W - 1):
            acc = acc + extx_ref[_rows(g + k, 2), :] * cw_ref[k:k + 1, :]
        xc_ref[_rows(g, 2), :] = acc
        xcb_ref[_rows(g, 2), :] = acc.astype(BF16)
    for hh in range(LRU_HEADS):
        gate_ref[:, 2 * hd * hh:2 * hd * (hh + 1)] = _dot(xcb_ref[:, hd * hh:hd * (hh + 1)], wg_ref[hh])

    extp_ref[_rows(POOL_HALO, n_grp), w_pool // 2:] = in_proj(2 * w_lru + w_pool // 2, 2 * w_lru + w_pool)
    _fill_halo(extp_ref, carryp_ref, POOL_HALO, n_grp)
    ug_ref[:, 0:half] = in_proj(w_lru, w_lru + half)

    def _scan_groups(g0, g1, a_cum, h_loc):
        for g in range(g0, g1):
            gates = gate_ref[_rows(g), :]
            ga = jnp.concatenate([gates[:, 2 * hd * hh:2 * hd * hh + hd] for hh in range(LRU_HEADS)], axis=1)
            gx = jnp.concatenate([gates[:, 2 * hd * hh + hd:2 * hd * (hh + 1)] for hh in range(LRU_HEADS)], axis=1)
            a, xi, mult = _gate_update(jnp.concatenate([ga, gx], axis=1), xc_ref[_rows(g), :],
                                       ba_ref[...], bx_ref[...], cneg)
            if g == 0:
                mult = jnp.where(first_token, 1.0, mult)
            bt = xi * mult
            a_cum = a if a_cum is None else a * a_cum
            h_loc = bt if h_loc is None else a * h_loc + bt
            a_ref[_rows(g), :] = a_cum
            b_ref[_rows(g), :] = h_loc
        return a_cum, h_loc

    a_tot, h_tot = _scan_groups(0, n_grp // 2, None, None)
    ug_ref[:, half:] = in_proj(w_lru + half, 2 * w_lru)
    a_tot, h_tot = _scan_groups(n_grp // 2, n_grp, a_tot, h_tot)

    gd = w_pool // len(POOL_WINDOWS)
    row = lax.broadcasted_iota(jnp.int32, (tm, 1), 0)
    pos = t * tm + (row & (SUBLANES - 1)) * n_grp + (row >> 3)
    pooled = []
    for gi, w in enumerate(POOL_WINDOWS):
        cols = slice(gi * gd, (gi + 1) * gd)
        lo = POOL_HALO - (w - 1)
        s = extp_ref[_rows(lo, n_grp + w - 1), cols]
        width = 1
        while width < w:
            s = s[width * SUBLANES:, :] + s[:-width * SUBLANES, :]
            width *= 2
        inv = 1.0 / jnp.minimum(pos + 1, w).astype(F32)
        pooled.append(s * inv - extp_ref[_rows(POOL_HALO, n_grp), cols])

    y_pool = _pool_project(jnp.concatenate(pooled, axis=1), wp_ref, ps_ref[...])
    ymix_ref[:, w_lru:] = y_pool.astype(BF16)
    obuf[slot] = (x_tile() + _dot(ymix_ref[:, w_lru:], w_out_ref[w_lru:, :])).reshape(n_grp, SUBLANES, d)

    p_cum, q_cum = a_tot, h_tot
    for dd in (1, 2, 4):
        keep = sub >= dd
        q_cum = jnp.where(keep, p_cum * pltpu.roll(q_cum, dd, 0) + q_cum, q_cum)
        p_cum = jnp.where(keep, p_cum * pltpu.roll(p_cum, dd, 0), p_cum)
    h_in = jnp.broadcast_to(hc_ref[...], (SUBLANES, w_lru))
    seg_end = p_cum * h_in + q_cum
    seg_in = jnp.where(sub == 0, h_in, pltpu.roll(seg_end, 1, 0))
    h_end = seg_end[SUBLANES - 1:SUBLANES, :]
    hc_ref[...] = h_end

    seg_in2 = jnp.concatenate([seg_in, seg_in], axis=0)
    for gp in range(n_grp // 2):
        r2 = _rows(2 * gp, 2)
        hs = b_ref[r2, :] + a_ref[r2, :] * seg_in2
        ymix_ref[r2, 0:w_lru] = (hs * jax.nn.gelu(ug_ref[r2, :], approximate=True)).astype(BF16)

    obuf[slot] += _dot(ymix_ref[:, 0:w_lru], w_out_ref[0:w_lru, :]).reshape(n_grp, SUBLANES, d)

    wup16_ref[...] = wup32_ref[...].astype(BF16)
    wdn16_ref[...] = wdn32_ref[...].astype(BF16)

    for c in _tile_copies(o_hbm, obuf, out_sem, b, t, slot, n_grp, True):
        c.start()

    @pl.when(step > 0)
    def _wait_prev_writeback():
        prv = step - 1
        for c in _tile_copies(o_hbm, obuf, out_sem, prv // n_t, prv % n_t, 1 - slot, n_grp, True):
            c.wait()

    @pl.when(step == n_steps - 1)
    def _wait_last_writeback():
        for c in _tile_copies(o_hbm, obuf, out_sem, b, t, slot, n_grp, True):
            c.wait()

    @pl.when(t == n_t - 1)
    def _emit_state():
        last = SUBLANES - 1
        hlast_ref[...] = h_end
        for i in range(CONV_HALO):
            nconv_ref[i:i + 1, :] = carryx_ref[pl.ds(i * SUBLANES + last, 1), :]
        for i in range(POOL_HALO):
            npool_ref[i:i + 1, :] = carryp_ref[pl.ds(i * SUBLANES + last, 1), :]


def _const_spec(shape):
    nd = len(shape)
    return pl.BlockSpec(shape, lambda *_: (0,) * nd, pipeline_mode=pl.Buffered(1))


def _seq_mixer(x, p, w_up, w_down, tm):
    bsz, seq, d = x.shape
    w_lru = p["conv_b"].shape[1]
    w_pool = p["pool_scale"].shape[1]
    weights = (p["norm_mix_g"], p["w_in"], p["conv_w"], p["conv_b"], p["w_gate"], p["b_rg_a"],
               p["b_rg_x"], p["lru_lambda"], p["w_pool"], p["pool_scale"], p["w_out"])
    n_t = seq // tm
    n_steps = bsz * n_t
    n_grp = tm // SUBLANES
    assert n_grp % 2 == 0 and n_grp >= POOL_HALO
    any_spec = pl.BlockSpec(memory_space=pl.ANY)
    state_spec = lambda rows, w: pl.BlockSpec((None, rows, w), lambda b, t: (b, 0, 0))
    slab_spec = lambda w: pl.BlockSpec((w.shape[0] // n_steps, w.shape[1]),
                                       lambda b, t: (b * n_t + t, 0))
    ext_rows = lambda halo: (halo + n_grp) * SUBLANES
    return pl.pallas_call(
        functools.partial(_seq_mixer_kernel, n_t=n_t),
        grid=(bsz, n_t),
        in_specs=([any_spec] + [_const_spec(w.shape) for w in weights]
                  + [slab_spec(w_up), slab_spec(w_down)]),
        out_specs=[any_spec, state_spec(1, w_lru), state_spec(CONV_W - 1, w_lru),
                   state_spec(POOL_BUF, w_pool), slab_spec(w_up), slab_spec(w_down)],
        out_shape=[jax.ShapeDtypeStruct((bsz, seq, d), F32),
                   jax.ShapeDtypeStruct((bsz, 1, w_lru), F32),
                   jax.ShapeDtypeStruct((bsz, CONV_W - 1, w_lru), F32),
                   jax.ShapeDtypeStruct((bsz, POOL_BUF, w_pool), F32),
                   jax.ShapeDtypeStruct(w_up.shape, BF16),
                   jax.ShapeDtypeStruct(w_down.shape, BF16)],
        scratch_shapes=[
            pltpu.VMEM((2, n_grp, SUBLANES, d), F32),
            pltpu.VMEM((2, n_grp, SUBLANES, d), F32),
            pltpu.SemaphoreType.DMA((2,)),
            pltpu.SemaphoreType.DMA((2,)),
            pltpu.VMEM((tm, d), BF16),
            pltpu.VMEM((ext_rows(CONV_HALO), w_lru), F32),
            pltpu.VMEM((CONV_HALO * SUBLANES, w_lru), F32),
            pltpu.VMEM((ext_rows(POOL_HALO), w_pool), F32),
            pltpu.VMEM((POOL_HALO * SUBLANES, w_pool), F32),
            pltpu.VMEM((tm, w_lru), F32),
            pltpu.VMEM((tm, w_lru), BF16),
            pltpu.VMEM((tm, 2 * w_lru), F32),
            pltpu.VMEM((tm, w_lru), F32),
            pltpu.VMEM((tm, w_lru), F32),
            pltpu.VMEM((tm, w_lru), F32),
            pltpu.VMEM((tm, d), BF16),
            pltpu.VMEM((1, w_lru), F32),
        ],
        compiler_params=pltpu.CompilerParams(
            dimension_semantics=("arbitrary", "arbitrary"),
            vmem_limit_bytes=MIX_VMEM_LIMIT),
        name="seq_mixer",
    )(x, *weights, w_up, w_down)


DEC_CHUNK = 512


def _dec_mixer_kernel(x_ref, xcol_ref, h0_ref, cbuf_hbm, pbuf_hbm, gmix_ref, w_in_ref, cw_ref, cb_ref,
                      wg_ref, ba_ref, bx_ref, lam_ref, wp_ref, ps_ref, w_out_ref,
                      o_ref, hnew_ref, nconv_hbm, npool_hbm, w_in16_ref, w_out16_ref,
                      hn_ref, proj_ref, ymix_ref, cbuf_ref, pbuf_ref, nconv_ref, npool_ref, state_sem,
                      *, n_in):
    j = pl.program_id(0)
    w_lru = h0_ref.shape[1]
    w_pool = ps_ref.shape[1]
    state_in = [pltpu.make_async_copy(cbuf_hbm, cbuf_ref, state_sem.at[0]),
                pltpu.make_async_copy(pbuf_hbm, pbuf_ref, state_sem.at[1])]
    state_out = [pltpu.make_async_copy(nconv_ref, nconv_hbm, state_sem.at[2]),
                 pltpu.make_async_copy(npool_ref, npool_hbm, state_sem.at[3])]

    @pl.when(j == 0)
    def _norm():
        for c in state_in:
            c.start()
        hn_ref[...] = _rmsnorm(x_ref[...], gmix_ref[...]).astype(BF16)

    @pl.when(j < n_in)
    def _in_projection():
        w = w_in_ref[...].astype(BF16)
        w_in16_ref[...] = w
        proj_ref[j] = _dot(hn_ref[...], w)

    @pl.when(j == n_in)
    def _mix():
        per = w_lru // DEC_CHUNK
        branch = lambda k: jnp.concatenate([proj_ref[k * per + c] for c in range(per)], axis=1)
        ux, ug, up = branch(0), branch(1), branch(2)
        for c in state_in:
            c.wait()

        xc = cb_ref[...] + ux * cw_ref[CONV_W - 1:CONV_W, :]
        for k in range(CONV_W - 1):
            tap = cbuf_ref[k]
            xc = xc + tap * cw_ref[k:k + 1, :]
            if k > 0:
                nconv_ref[k - 1] = tap
        nconv_ref[CONV_W - 2] = ux

        a, mult, xi = _lru_gates(xc, wg_ref, ba_ref[...], bx_ref[...], _neg_c_softplus(lam_ref[...]))
        h_new = a * h0_ref[...] + xi * mult
        hnew_ref[...] = h_new
        ymix_ref[:, 0:w_lru] = (h_new * jax.nn.gelu(ug, approximate=True)).astype(BF16)

        gd = w_pool // len(POOL_WINDOWS)
        s = up
        width = 1
        pooled = []
        for g, w in enumerate(POOL_WINDOWS):
            while width < w:
                k = POOL_BUF - width
                hist = pbuf_ref[k]
                if k > 0:
                    npool_ref[k - 1] = hist
                s = s + hist[:, g * gd:]
                width += 1
            count = float(min(PAST_LEN + 1, w))
            pooled.append(s[:, 0:gd] / count - up[:, g * gd:(g + 1) * gd])
            if g + 1 < len(POOL_WINDOWS):
                s = s[:, gd:]
        npool_ref[POOL_BUF - 1] = up
        y_pool = _pool_project(jnp.concatenate(pooled, axis=1), wp_ref, ps_ref[...])
        ymix_ref[:, w_lru:] = y_pool.astype(BF16)
        for c in state_out:
            c.start()

    @pl.when(j >= n_in)
    def _out_projection():
        w = w_out_ref[...].astype(BF16)
        w_out16_ref[...] = w
        o_ref[...] = xcol_ref[...] + _dot(ymix_ref[...], w)

    @pl.when(j == pl.num_programs(0) - 1)
    def _drain_state():
        for c in state_out:
            c.wait()


def _dec_mixer(x, h0, cbuf, pbuf, p, w_in, w_out):
    rows, d = x.shape
    n_in = w_in.shape[1] // DEC_CHUNK
    n_out = w_out.shape[1] // DEC_CHUNK
    small = (p["conv_w"], p["conv_b"], p["w_gate"], p["b_rg_a"], p["b_rg_x"], p["lru_lambda"],
             p["w_pool"], p["pool_scale"])
    full = lambda a: pl.BlockSpec(a.shape, functools.partial(lambda nd, j: (0,) * nd, a.ndim))
    once = lambda a: _const_spec(a.shape)
    in_hbm = pl.BlockSpec(memory_space=pl.ANY)
    in_chunk = lambda j: (0, jnp.minimum(j, n_in - 1))
    out_chunk = lambda j: (0, jnp.maximum(j - n_in, 0))
    return pl.pallas_call(
        functools.partial(_dec_mixer_kernel, n_in=n_in),
        grid=(n_in + n_out,),
        in_specs=([once(x), pl.BlockSpec((rows, DEC_CHUNK), out_chunk), once(h0), in_hbm, in_hbm,
                   once(p["norm_mix_g"]), pl.BlockSpec((d, DEC_CHUNK), in_chunk)]
                  + [once(a) for a in small]
                  + [pl.BlockSpec((w_out.shape[0], DEC_CHUNK), out_chunk)]),
        out_specs=[pl.BlockSpec((rows, DEC_CHUNK), out_chunk), full(h0), in_hbm, in_hbm,
                   pl.BlockSpec((d, DEC_CHUNK), in_chunk),
                   pl.BlockSpec((w_out.shape[0], DEC_CHUNK), out_chunk)],
        out_shape=[jax.ShapeDtypeStruct(s.shape, F32) for s in (x, h0, cbuf, pbuf)]
                  + [jax.ShapeDtypeStruct(w_in.shape, BF16), jax.ShapeDtypeStruct(w_out.shape, BF16)],
        scratch_shapes=[
            pltpu.VMEM((rows, d), BF16),
            pltpu.VMEM((n_in, rows, DEC_CHUNK), F32),
            pltpu.VMEM((rows, w_out.shape[0]), BF16),
            pltpu.VMEM(cbuf.shape, F32),
            pltpu.VMEM(pbuf.shape, F32),
            pltpu.VMEM(cbuf.shape, F32),
            pltpu.VMEM(pbuf.shape, F32),
            pltpu.SemaphoreType.DMA((4,)),
        ],
        compiler_params=pltpu.CompilerParams(
            dimension_semantics=("arbitrary",), vmem_limit_bytes=DEC_VMEM_LIMIT),
        name="dec_mixer",
    )(x, x, h0, cbuf, pbuf, p["norm_mix_g"], w_in, *small, w_out)


def _mlp_kernel(x_ref, xs_ref, g_ref, wup_ref, wdn_ref, gfin_ref, o_ref, os_ref, h_ref):
    tm = x_ref.shape[0]
    i = pl.program_id(0)
    j = pl.program_id(1)
    first = j == 0
    last = j == pl.num_programs(1) - 1
    with_decode = i == 0

    def _start(src_ref, rows, acc_ref):
        x = src_ref[...]
        h_ref[rows, :] = _rmsnorm(x, g_ref[...]).astype(BF16)
        acc_ref[...] = x

    def _ffn(rows):
        f = jnp.maximum(_dot(h_ref[rows, :], wup_ref[...]), 0.0)
        return _dot((f * f).astype(BF16), wdn_ref[...])

    pl.when(first)(lambda: _start(x_ref, pl.ds(0, tm), o_ref))
    pl.when(first & with_decode)(lambda: _start(xs_ref, pl.ds(tm, xs_ref.shape[0]), os_ref))

    @pl.when(with_decode)
    def _both_streams():
        part = _ffn(pl.ds(0, h_ref.shape[0]))
        o_ref[...] += part[:tm]
        os_ref[...] += part[tm:]

    @pl.when(jnp.logical_not(with_decode))
    def _prompt_only():
        o_ref[...] += _ffn(pl.ds(0, tm))

    @pl.when(last)
    def _finish():
        o_ref[...] = _rmsnorm(o_ref[...], gfin_ref[...])

    @pl.when(last & with_decode)
    def _finish_decode():
        os_ref[...] = _rmsnorm(os_ref[...], gfin_ref[...])


def _mlp(x, xs, p, tm, tf):
    n, d = x.shape
    ns = xs.shape[0]
    d_ff = p["w_up"].shape[1]
    vec_spec = pl.BlockSpec((1, d), lambda i, j: (0, 0))
    tok_spec = pl.BlockSpec((tm, d), lambda i, j: (i, 0))
    dec_spec = pl.BlockSpec((ns, d), lambda i, j: (0, 0))
    return pl.pallas_call(
        _mlp_kernel,
        grid=(n // tm, d_ff // tf),
        in_specs=[tok_spec, dec_spec, vec_spec,
                  pl.BlockSpec((d, tf), lambda i, j: (0, j)),
                  pl.BlockSpec((tf, d), lambda i, j: (j, 0)),
                  vec_spec],
        out_specs=[tok_spec, dec_spec],
        out_shape=[jax.ShapeDtypeStruct((n, d), F32), jax.ShapeDtypeStruct((ns, d), F32)],
        scratch_shapes=[pltpu.VMEM((tm + ns, d), BF16)],
        compiler_params=pltpu.CompilerParams(
            dimension_semantics=("arbitrary", "arbitrary"),
            vmem_limit_bytes=MLP_VMEM_LIMIT),
        name="mlp",
    )(x, xs, p["norm_mlp_g"], p["w_up"], p["w_down"], p["norm_final_g"])


def _layer_params(l, norm_mix_g, w_in, conv_w, conv_b, w_rg_a, b_rg_a, w_rg_x, b_rg_x, lru_lambda,
                  w_pool, pool_scale, w_out, norm_mlp_g, w_up, w_down, norm_final_g):
    row = lambda v: v.reshape(1, -1)
    return {
        "norm_mix_g": row(norm_mix_g[l]),
        "w_in": w_in[l],
        "conv_w": conv_w[l],
        "conv_b": row(conv_b[l]),
        "w_gate": jnp.concatenate([w_rg_a[l], w_rg_x[l]], axis=-1).astype(BF16),
        "b_rg_a": row(b_rg_a[l]),
        "b_rg_x": row(b_rg_x[l]),
        "lru_lambda": row(lru_lambda[l]),
        "w_pool": w_pool[l].astype(BF16),
        "pool_scale": row(pool_scale[l]),
        "w_out": w_out[l],
        "norm_mlp_g": row(norm_mlp_g[l]),
        "w_up": w_up[l],
        "w_down": w_down[l],
        "norm_final_g": row(norm_final_g),
    }


def kernel(x_prompt, x_sample, state_lru_h, state_conv, state_pool, norm_mix_g, w_in, conv_w, conv_b, w_rg_a, b_rg_a, w_rg_x, b_rg_x, lru_lambda, w_pool, pool_scale, w_out, norm_mlp_g, w_up, w_down, norm_final_g):
    depth = w_in.shape[0]
    assert depth == 1, "final RMSNorm is fused into the (single) layer's MLP kernel"
    bp, tp, d = x_prompt.shape
    bs, ts, _ = x_sample.shape
    assert ts == 1 and tp % MIX_TM == 0 and (bp * tp) % MLP_TM == 0
    p = _layer_params(0, norm_mix_g, w_in, conv_w, conv_b, w_rg_a, b_rg_a, w_rg_x, b_rg_x,
                      lru_lambda, w_pool, pool_scale, w_out, norm_mlp_g, w_up, w_down, norm_final_g)

    hist_major = lambda s: jnp.transpose(s, (1, 0, 2))
    x1s, hs, cs, ps, p["w_in"], p["w_out"] = _dec_mixer(
        x_sample.reshape(bs, d), state_lru_h[0], hist_major(state_conv[0]), hist_major(state_pool[0]),
        p, p["w_in"], p["w_out"])
    cs, ps = hist_major(cs), hist_major(ps)
    x1p, hp, cp, pp, p["w_up"], p["w_down"] = _seq_mixer(x_prompt, p, p["w_up"], p["w_down"], MIX_TM)
    yp, ys = _mlp(x1p.reshape(bp * tp, d), x1s, p, MLP_TM, MLP_TF)
    yp, ys = yp.reshape(bp, tp, d), ys.reshape(bs, ts, d)

    return (yp, ys, hp.reshape(1, bp, -1), cp[None], pp[None], hs[None], cs[None], ps[None])
```

```python
import functools

import jax
import jax.numpy as jnp
from jax import lax
from jax.experimental import pallas as pl
from jax.experimental.pallas import tpu as pltpu

EPS = 1e-6
LRU_HEADS = 8
LRU_C = 8.0
CONV_W = 4
POOL_WINDOWS = (2, 4, 8, 16)
POOL_BUF = max(POOL_WINDOWS) - 1
PAST_LEN = 16384

SUBLANES = 8
MIB = 1024 * 1024

MIX_TM = 256
MLP_TM = 512
MLP_TF = 2048
MIX_VMEM_LIMIT = 60 * MIB
DEC_VMEM_LIMIT = 56 * MIB
MLP_VMEM_LIMIT = 60 * MIB

F32 = jnp.float32
BF16 = jnp.bfloat16


def _rmsnorm(x, g):
    return (x * lax.rsqrt(jnp.mean(x * x, axis=-1, keepdims=True) + EPS)) * g


def _dot(a, b):
    return jnp.dot(a, b, preferred_element_type=F32)


def _gate_update(g, xc, ba, bx, cneg):
    hd = xc.shape[1]
    r = jax.nn.sigmoid(g[:, :hd] + ba)
    i = jax.nn.sigmoid(g[:, hd:] + bx)
    a = jnp.exp(r * cneg)
    y = 1.0 - a * a
    mult = jnp.where(y > 0.0, y * lax.rsqrt(y), 0.0)
    return a, xc * i, mult


def _lru_gates(xc, wg_ref, ba, bx, cneg):
    hd = xc.shape[1] // LRU_HEADS
    xcb = xc.astype(BF16)
    a_parts, m_parts, xi_parts = [], [], []
    for h in range(LRU_HEADS):
        sl = slice(h * hd, (h + 1) * hd)
        g = _dot(xcb[:, sl], wg_ref[h])
        a, xi, mult = _gate_update(g, xc[:, sl], ba[:, sl], bx[:, sl], cneg[:, sl])
        a_parts.append(a)
        m_parts.append(mult)
        xi_parts.append(xi)
    cat = lambda ps: jnp.concatenate(ps, axis=1)
    return cat(a_parts), cat(m_parts), cat(xi_parts)


def _pool_project(pooled, wp_ref, scale):
    n_g = len(POOL_WINDOWS)
    gd = pooled.shape[1] // n_g
    pb = pooled.astype(BF16)
    outs = [_dot(pb[:, g * gd:(g + 1) * gd], wp_ref[g]) for g in range(n_g)]
    return jnp.concatenate(outs, axis=1) * scale


def _neg_c_softplus(lam):
    return -LRU_C * jax.nn.softplus(-lam)


CONV_HALO = CONV_W - 1
POOL_HALO = POOL_BUF


def _rows(k, n=1):
    return pl.ds(k * SUBLANES, n * SUBLANES)


def _tile_copies(hbm, buf, sem, b, t, slot, n_grp, to_hbm):
    tm = n_grp * SUBLANES
    copies = []
    for s in range(SUBLANES):
        far = hbm.at[b, pl.ds(t * tm + s * n_grp, n_grp), :]
        near = buf.at[slot, :, s, :]
        src, dst = (near, far) if to_hbm else (far, near)
        copies.append(pltpu.make_async_copy(src, dst, sem.at[slot]))
    return copies


def _fill_halo(ext_ref, carry_ref, halo, n_grp):
    sub = lax.broadcasted_iota(jnp.int32, (SUBLANES, ext_ref.shape[1]), 0)
    for j in range(1, halo + 1):
        cur = ext_ref[_rows(halo + n_grp - j), :]
        prev = carry_ref[_rows(halo - j), :]
        ext_ref[_rows(halo - j), :] = pltpu.roll(jnp.where(sub == SUBLANES - 1, prev, cur), 1, 0)
    carry_ref[...] = ext_ref[_rows(n_grp, halo), :]


def _seq_mixer_kernel(x_hbm, gmix_ref, w_in_ref, cw_ref, cb_ref, wg_ref, ba_ref, bx_ref,
                      lam_ref, wp_ref, ps_ref, w_out_ref, wup32_ref, wdn32_ref,
                      o_hbm, hlast_ref, nconv_ref, npool_ref, wup16_ref, wdn16_ref,
                      xbuf, obuf, in_sem, out_sem, hn_ref, extx_ref, carryx_ref, extp_ref,
                      carryp_ref, xc_ref, xcb_ref, gate_ref, ug_ref, a_ref, b_ref, ymix_ref, hc_ref,
                      *, n_t):
    n_grp = xbuf.shape[1]
    tm = n_grp * SUBLANES
    d = xbuf.shape[3]
    w_lru = a_ref.shape[1]
    w_pool = extp_ref.shape[1]
    hd = w_lru // LRU_HEADS
    half = w_lru // 2
    b = pl.program_id(0)
    t = pl.program_id(1)
    step = b * n_t + t
    n_steps = pl.num_programs(0) * n_t
    slot = step % 2

    @pl.when(step == 0)
    def _first_fetch():
        for c in _tile_copies(x_hbm, xbuf, in_sem, b, t, slot, n_grp, False):
            c.start()

    @pl.when(step + 1 < n_steps)
    def _prefetch():
        nxt = step + 1
        for c in _tile_copies(x_hbm, xbuf, in_sem, nxt // n_t, nxt % n_t, 1 - slot, n_grp, False):
            c.start()

    for c in _tile_copies(x_hbm, xbuf, in_sem, b, t, slot, n_grp, False):
        c.wait()

    @pl.when(t == 0)
    def _reset_state():
        hc_ref[...] = jnp.zeros_like(hc_ref)
        carryx_ref[...] = jnp.zeros_like(carryx_ref)
        carryp_ref[...] = jnp.zeros_like(carryp_ref)

    x_tile = lambda: xbuf[slot].reshape(tm, d)
    hn_ref[...] = _rmsnorm(x_tile(), gmix_ref[...]).astype(BF16)
    in_proj = lambda c0, c1: _dot(hn_ref[...], w_in_ref[:, c0:c1])
    sub = lax.broadcasted_iota(jnp.int32, (SUBLANES, w_lru), 0)
    first_token = (sub == 0) & (t == 0)
    cneg = _neg_c_softplus(lam_ref[...])

    extx_ref[_rows(CONV_HALO, n_grp), :] = in_proj(0, w_lru)
    _fill_halo(extx_ref, carryx_ref, CONV_HALO, n_grp)

    extp_ref[_rows(POOL_HALO, n_grp), 0:w_pool // 2] = in_proj(2 * w_lru, 2 * w_lru + w_pool // 2)
    for gp in range(n_grp // 2):
        g = 2 * gp
        acc = cb_ref[...] + extx_ref[_rows(CONV_HALO + g, 2), :] * cw_ref[CONV_W - 1:CONV_W, :]
        for k in range(CONV_W - 1):
            acc = acc + extx_ref[_rows(g + k, 2), :] * cw_ref[k:k + 1, :]
        xc_ref[_rows(g, 2), :] = acc
        xcb_ref[_rows(g, 2), :] = acc.astype(BF16)
    for hh in range(LRU_HEADS):
        gate_ref[:, 2 * hd * hh:2 * hd * (hh + 1)] = _dot(xcb_ref[:, hd * hh:hd * (hh + 1)], wg_ref[hh])

    extp_ref[_rows(POOL_HALO, n_grp), w_pool // 2:] = in_proj(2 * w_lru + w_pool // 2, 2 * w_lru + w_pool)
    _fill_halo(extp_ref, carryp_ref, POOL_HALO, n_grp)
    ug_ref[:, 0:half] = in_proj(w_lru, w_lru + half)

    def _scan_groups(g0, g1, a_cum, h_loc):
        for g in range(g0, g1):
            gates = gate_ref[_rows(g), :]
            ga = jnp.concatenate([gates[:, 2 * hd * hh:2 * hd * hh + hd] for hh in range(LRU_HEADS)], axis=1)
            gx = jnp.concatenate([gates[:, 2 * hd * hh + hd:2 * hd * (hh + 1)] for hh in range(LRU_HEADS)], axis=1)
            a, xi, mult = _gate_update(jnp.concatenate([ga, gx], axis=1), xc_ref[_rows(g), :],
                                       ba_ref[...], bx_ref[...], cneg)
            if g == 0:
                mult = jnp.where(first_token, 1.0, mult)
            bt = xi * mult
            a_cum = a if a_cum is None else a * a_cum
            h_loc = bt if h_loc is None else a * h_loc + bt
            a_ref[_rows(g), :] = a_cum
            b_ref[_rows(g), :] = h_loc
        return a_cum, h_loc

    a_tot, h_tot = _scan_groups(0, n_grp // 2, None, None)
    ug_ref[:, half:] = in_proj(w_lru + half, 2 * w_lru)
    a_tot, h_tot = _scan_groups(n_grp // 2, n_grp, a_tot, h_tot)

    gd = w_pool // len(POOL_WINDOWS)
    row = lax.broadcasted_iota(jnp.int32, (tm, 1), 0)
    pos = t * tm + (row & (SUBLANES - 1)) * n_grp + (row >> 3)
    pooled = []
    for gi, w in enumerate(POOL_WINDOWS):
        cols = slice(gi * gd, (gi + 1) * gd)
        lo = POOL_HALO - (w - 1)
        s = extp_ref[_rows(lo, n_grp + w - 1), cols]
        width = 1
        while width < w:
            s = s[width * SUBLANES:, :] + s[:-width * SUBLANES, :]
            width *= 2
        inv = 1.0 / jnp.minimum(pos + 1, w).astype(F32)
        pooled.append(s * inv - extp_ref[_rows(POOL_HALO, n_grp), cols])

    y_pool = _pool_project(jnp.concatenate(pooled, axis=1), wp_ref, ps_ref[...])
    ymix_ref[:, w_lru:] = y_pool.astype(BF16)
    obuf[slot] = (x_tile() + _dot(ymix_ref[:, w_lru:], w_out_ref[w_lru:, :])).reshape(n_grp, SUBLANES, d)

    p_cum, q_cum = a_tot, h_tot
    for dd in (1, 2, 4):
        keep = sub >= dd
        q_cum = jnp.where(keep, p_cum * pltpu.roll(q_cum, dd, 0) + q_cum, q_cum)
        p_cum = jnp.where(keep, p_cum * pltpu.roll(p_cum, dd, 0), p_cum)
    h_in = jnp.broadcast_to(hc_ref[...], (SUBLANES, w_lru))
    seg_end = p_cum * h_in + q_cum
    seg_in = jnp.where(sub == 0, h_in, pltpu.roll(seg_end, 1, 0))
    h_end = seg_end[SUBLANES - 1:SUBLANES, :]
    hc_ref[...] = h_end

    seg_in2 = jnp.concatenate([seg_in, seg_in], axis=0)
    for gp in range(n_grp // 2):
        r2 = _rows(2 * gp, 2)
        hs = b_ref[r2, :] + a_ref[r2, :] * seg_in2
        ymix_ref[r2, 0:w_lru] = (hs * jax.nn.gelu(ug_ref[r2, :], approximate=True)).astype(BF16)

    obuf[slot] += _dot(ymix_ref[:, 0:w_lru], w_out_ref[0:w_lru, :]).reshape(n_grp, SUBLANES, d)

    wup16_ref[...] = wup32_ref[...].astype(BF16)
    wdn16_ref[...] = wdn32_ref[...].astype(BF16)

    for c in _tile_copies(o_hbm, obuf, out_sem, b, t, slot, n_grp, True):
        c.start()

    @pl.when(step > 0)
    def _wait_prev_writeback():
        prv = step - 1
        for c in _tile_copies(o_hbm, obuf, out_sem, prv // n_t, prv % n_t, 1 - slot, n_grp, True):
            c.wait()

    @pl.when(step == n_steps - 1)
    def _wait_last_writeback():
        for c in _tile_copies(o_hbm, obuf, out_sem, b, t, slot, n_grp, True):
            c.wait()

    @pl.when(t == n_t - 1)
    def _emit_state():
        last = SUBLANES - 1
        hlast_ref[...] = h_end
        for i in range(CONV_HALO):
            nconv_ref[i:i + 1, :] = carryx_ref[pl.ds(i * SUBLANES + last, 1), :]
        for i in range(POOL_HALO):
            npool_ref[i:i + 1, :] = carryp_ref[pl.ds(i * SUBLANES + last, 1), :]


def _const_spec(shape):
    nd = len(shape)
    return pl.BlockSpec(shape, lambda *_: (0,) * nd, pipeline_mode=pl.Buffered(1))


def _seq_mixer(x, p, w_up, w_down, tm):
    bsz, seq, d = x.shape
    w_lru = p["conv_b"].shape[1]
    w_pool = p["pool_scale"].shape[1]
    weights = (p["norm_mix_g"], p["w_in"], p["conv_w"], p["conv_b"], p["w_gate"], p["b_rg_a"],
               p["b_rg_x"], p["lru_lambda"], p["w_pool"], p["pool_scale"], p["w_out"])
    n_t = seq // tm
    n_steps = bsz * n_t
    n_grp = tm // SUBLANES
    assert n_grp % 2 == 0 and n_grp >= POOL_HALO
    any_spec = pl.BlockSpec(memory_space=pl.ANY)
    state_spec = lambda rows, w: pl.BlockSpec((None, rows, w), lambda b, t: (b, 0, 0))
    slab_spec = lambda w: pl.BlockSpec((w.shape[0] // n_steps, w.shape[1]),
                                       lambda b, t: (b * n_t + t, 0))
    ext_rows = lambda halo: (halo + n_grp) * SUBLANES
    return pl.pallas_call(
        functools.partial(_seq_mixer_kernel, n_t=n_t),
        grid=(bsz, n_t),
        in_specs=([any_spec] + [_const_spec(w.shape) for w in weights]
                  + [slab_spec(w_up), slab_spec(w_down)]),
        out_specs=[any_spec, state_spec(1, w_lru), state_spec(CONV_W - 1, w_lru),
                   state_spec(POOL_BUF, w_pool), slab_spec(w_up), slab_spec(w_down)],
        out_shape=[jax.ShapeDtypeStruct((bsz, seq, d), F32),
                   jax.ShapeDtypeStruct((bsz, 1, w_lru), F32),
                   jax.ShapeDtypeStruct((bsz, CONV_W - 1, w_lru), F32),
                   jax.ShapeDtypeStruct((bsz, POOL_BUF, w_pool), F32),
                   jax.ShapeDtypeStruct(w_up.shape, BF16),
                   jax.ShapeDtypeStruct(w_down.shape, BF16)],
        scratch_shapes=[
            pltpu.VMEM((2, n_grp, SUBLANES, d), F32),
            pltpu.VMEM((2, n_grp, SUBLANES, d), F32),
            pltpu.SemaphoreType.DMA((2,)),
            pltpu.SemaphoreType.DMA((2,)),
            pltpu.VMEM((tm, d), BF16),
            pltpu.VMEM((ext_rows(CONV_HALO), w_lru), F32),
            pltpu.VMEM((CONV_HALO * SUBLANES, w_lru), F32),
            pltpu.VMEM((ext_rows(POOL_HALO), w_pool), F32),
            pltpu.VMEM((POOL_HALO * SUBLANES, w_pool), F32),
            pltpu.VMEM((tm, w_lru), F32),
            pltpu.VMEM((tm, w_lru), BF16),
            pltpu.VMEM((tm, 2 * w_lru), F32),
            pltpu.VMEM((tm, w_lru), F32),
            pltpu.VMEM((tm, w_lru), F32),
            pltpu.VMEM((tm, w_lru), F32),
            pltpu.VMEM((tm, d), BF16),
            pltpu.VMEM((1, w_lru), F32),
        ],
        compiler_params=pltpu.CompilerParams(
            dimension_semantics=("arbitrary", "arbitrary"),
            vmem_limit_bytes=MIX_VMEM_LIMIT),
        name="seq_mixer",
    )(x, *weights, w_up, w_down)


DEC_CHUNK = 512


def _dec_mixer_kernel(x_ref, xcol_ref, h0_ref, cbuf_hbm, pbuf_hbm, gmix_ref, w_in_ref, cw_ref, cb_ref,
                      wg_ref, ba_ref, bx_ref, lam_ref, wp_ref, ps_ref, w_out_ref,
                      o_ref, hnew_ref, nconv_hbm, npool_hbm, w_in16_ref, w_out16_ref,
                      hn_ref, proj_ref, ymix_ref, cbuf_ref, pbuf_ref, nconv_ref, npool_ref, state_sem,
                      *, n_in):
    j = pl.program_id(0)
    w_lru = h0_ref.shape[1]
    w_pool = ps_ref.shape[1]
    state_in = [pltpu.make_async_copy(cbuf_hbm, cbuf_ref, state_sem.at[0]),
                pltpu.make_async_copy(pbuf_hbm, pbuf_ref, state_sem.at[1])]
    state_out = [pltpu.make_async_copy(nconv_ref, nconv_hbm, state_sem.at[2]),
                 pltpu.make_async_copy(npool_ref, npool_hbm, state_sem.at[3])]

    @pl.when(j == 0)
    def _norm():
        for c in state_in:
            c.start()
        hn_ref[...] = _rmsnorm(x_ref[...], gmix_ref[...]).astype(BF16)

    @pl.when(j < n_in)
    def _in_projection():
        w = w_in_ref[...].astype(BF16)
        w_in16_ref[...] = w
        proj_ref[j] = _dot(hn_ref[...], w)

    @pl.when(j == n_in)
    def _mix():
        per = w_lru // DEC_CHUNK
        branch = lambda k: jnp.concatenate([proj_ref[k * per + c] for c in range(per)], axis=1)
        ux, ug, up = branch(0), branch(1), branch(2)
        for c in state_in:
            c.wait()

        xc = cb_ref[...] + ux * cw_ref[CONV_W - 1:CONV_W, :]
        for k in range(CONV_W - 1):
            tap = cbuf_ref[k]
            xc = xc + tap * cw_ref[k:k + 1, :]
            if k > 0:
                nconv_ref[k - 1] = tap
        nconv_ref[CONV_W - 2] = ux

        a, mult, xi = _lru_gates(xc, wg_ref, ba_ref[...], bx_ref[...], _neg_c_softplus(lam_ref[...]))
        h_new = a * h0_ref[...] + xi * mult
        hnew_ref[...] = h_new
        ymix_ref[:, 0:w_lru] = (h_new * jax.nn.gelu(ug, approximate=True)).astype(BF16)

        gd = w_pool // len(POOL_WINDOWS)
        s = up
        width = 1
        pooled = []
        for g, w in enumerate(POOL_WINDOWS):
            while width < w:
                k = POOL_BUF - width
                hist = pbuf_ref[k]
                if k > 0:
                    npool_ref[k - 1] = hist
                s = s + hist[:, g * gd:]
                width += 1
            count = float(min(PAST_LEN + 1, w))
            pooled.append(s[:, 0:gd] / count - up[:, g * gd:(g + 1) * gd])
            if g + 1 < len(POOL_WINDOWS):
                s = s[:, gd:]
        npool_ref[POOL_BUF - 1] = up
        y_pool = _pool_project(jnp.concatenate(pooled, axis=1), wp_ref, ps_ref[...])
        ymix_ref[:, w_lru:] = y_pool.astype(BF16)
        for c in state_out:
            c.start()

    @pl.when(j >= n_in)
    def _out_projection():
        w = w_out_ref[...].astype(BF16)
        w_out16_ref[...] = w
        o_ref[...] = xcol_ref[...] + _dot(ymix_ref[...], w)

    @pl.when(j == pl.num_programs(0) - 1)
    def _drain_state():
        for c in state_out:
            c.wait()


def _dec_mixer(x, h0, cbuf, pbuf, p, w_in, w_out):
    rows, d = x.shape
    n_in = w_in.shape[1] // DEC_CHUNK
    n_out = w_out.shape[1] // DEC_CHUNK
    small = (p["conv_w"], p["conv_b"], p["w_gate"], p["b_rg_a"], p["b_rg_x"], p["lru_lambda"],
             p["w_pool"], p["pool_scale"])
    full = lambda a: pl.BlockSpec(a.shape, functools.partial(lambda nd, j: (0,) * nd, a.ndim))
    once = lambda a: _const_spec(a.shape)
    in_hbm = pl.BlockSpec(memory_space=pl.ANY)
    in_chunk = lambda j: (0, jnp.minimum(j, n_in - 1))
    out_chunk = lambda j: (0, jnp.maximum(j - n_in, 0))
    return pl.pallas_call(
        functools.partial(_dec_mixer_kernel, n_in=n_in),
        grid=(n_in + n_out,),
        in_specs=([once(x), pl.BlockSpec((rows, DEC_CHUNK), out_chunk), once(h0), in_hbm, in_hbm,
                   once(p["norm_mix_g"]), pl.BlockSpec((d, DEC_CHUNK), in_chunk)]
                  + [once(a) for a in small]
                  + [pl.BlockSpec((w_out.shape[0], DEC_CHUNK), out_chunk)]),
        out_specs=[pl.BlockSpec((rows, DEC_CHUNK), out_chunk), full(h0), in_hbm, in_hbm,
                   pl.BlockSpec((d, DEC_CHUNK), in_chunk),
                   pl.BlockSpec((w_out.shape[0], DEC_CHUNK), out_chunk)],
        out_shape=[jax.ShapeDtypeStruct(s.shape, F32) for s in (x, h0, cbuf, pbuf)]
                  + [jax.ShapeDtypeStruct(w_in.shape, BF16), jax.ShapeDtypeStruct(w_out.shape, BF16)],
        scratch_shapes=[
            pltpu.VMEM((rows, d), BF16),
            pltpu.VMEM((n_in, rows, DEC_CHUNK), F32),
            pltpu.VMEM((rows, w_out.shape[0]), BF16),
            pltpu.VMEM(cbuf.shape, F32),
            pltpu.VMEM(pbuf.shape, F32),
            pltpu.VMEM(cbuf.shape, F32),
            pltpu.VMEM(pbuf.shape, F32),
            pltpu.SemaphoreType.DMA((4,)),
        ],
        compiler_params=pltpu.CompilerParams(
            dimension_semantics=("arbitrary",), vmem_limit_bytes=DEC_VMEM_LIMIT),
        name="dec_mixer",
    )(x, x, h0, cbuf, pbuf, p["norm_mix_g"], w_in, *small, w_out)


def _mlp_kernel(x_ref, xs_ref, g_ref, wup_ref, wdn_ref, gfin_ref, o_ref, os_ref, h_ref):
    tm = x_ref.shape[0]
    i = pl.program_id(0)
    j = pl.program_id(1)
    first = j == 0
    last = j == pl.num_programs(1) - 1
    with_decode = i == 0

    def _start(src_ref, rows, acc_ref):
        x = src_ref[...]
        h_ref[rows, :] = _rmsnorm(x, g_ref[...]).astype(BF16)
        acc_ref[...] = x

    def _ffn(rows):
        f = jnp.maximum(_dot(h_ref[rows, :], wup_ref[...]), 0.0)
        return _dot((f * f).astype(BF16), wdn_ref[...])

    pl.when(first & with_decode)(lambda: _start(x_ref, pl.ds(0, tm), o_ref))
    pl.when(first & with_decode)(lambda: _start(xs_ref, pl.ds(tm, xs_ref.shape[0]), os_ref))

    @pl.when(with_decode)
    def _both_streams():
        part = _ffn(pl.ds(0, h_ref.shape[0]))
        o_ref[...] += part[:tm]
        os_ref[...] += part[tm:]

    prompt_only = jnp.logical_not(with_decode)
    halves = (pl.ds(0, tm // 2), pl.ds(tm // 2, tm // 2))

    @pl.when(prompt_only & first)
    def _first_tile():
        fs = []
        for r in halves:
            x = x_ref[r, :]
            h_ref[r, :] = _rmsnorm(x, g_ref[...]).astype(BF16)
            o_ref[r, :] = x
            fs.append(jnp.maximum(_dot(h_ref[r, :], wup_ref[...]), 0.0))
        f = jnp.concatenate(fs, axis=0)
        o_ref[...] += _dot((f * f).astype(BF16), wdn_ref[...])

    @pl.when(prompt_only & jnp.logical_not(first) & jnp.logical_not(last))
    def _middle_tile():
        o_ref[...] += _ffn(pl.ds(0, tm))

    @pl.when(prompt_only & last)
    def _last_tile():
        f = jnp.maximum(_dot(h_ref[pl.ds(0, tm), :], wup_ref[...]), 0.0)
        fb = (f * f).astype(BF16)
        for k, r in enumerate(halves):
            acc = o_ref[r, :] + _dot(fb[k * (tm // 2):(k + 1) * (tm // 2), :], wdn_ref[...])
            o_ref[r, :] = _rmsnorm(acc, gfin_ref[...])

    @pl.when(last & with_decode)
    def _finish():
        o_ref[...] = _rmsnorm(o_ref[...], gfin_ref[...])

    @pl.when(last & with_decode)
    def _finish_decode():
        os_ref[...] = _rmsnorm(os_ref[...], gfin_ref[...])


def _mlp(x, xs, p, tm, tf):
    n, d = x.shape
    ns = xs.shape[0]
    d_ff = p["w_up"].shape[1]
    vec_spec = pl.BlockSpec((1, d), lambda i, j: (0, 0))
    tok_spec = pl.BlockSpec((tm, d), lambda i, j: (i, 0))
    dec_spec = pl.BlockSpec((ns, d), lambda i, j: (0, 0))
    return pl.pallas_call(
        _mlp_kernel,
        grid=(n // tm, d_ff // tf),
        in_specs=[tok_spec, dec_spec, vec_spec,
                  pl.BlockSpec((d, tf), lambda i, j: (0, j)),
                  pl.BlockSpec((tf, d), lambda i, j: (j, 0)),
                  vec_spec],
        out_specs=[tok_spec, dec_spec],
        out_shape=[jax.ShapeDtypeStruct((n, d), F32), jax.ShapeDtypeStruct((ns, d), F32)],
        scratch_shapes=[pltpu.VMEM((tm + ns, d), BF16)],
        compiler_params=pltpu.CompilerParams(
            dimension_semantics=("arbitrary", "arbitrary"),
            vmem_limit_bytes=MLP_VMEM_LIMIT),
        name="mlp",
    )(x, xs, p["norm_mlp_g"], p["w_up"], p["w_down"], p["norm_final_g"])


def _layer_params(l, norm_mix_g, w_in, conv_w, conv_b, w_rg_a, b_rg_a, w_rg_x, b_rg_x, lru_lambda,
                  w_pool, pool_scale, w_out, norm_mlp_g, w_up, w_down, norm_final_g):
    row = lambda v: v.reshape(1, -1)
    return {
        "norm_mix_g": row(norm_mix_g[l]),
        "w_in": w_in[l],
        "conv_w": conv_w[l],
        "conv_b": row(conv_b[l]),
        "w_gate": jnp.concatenate([w_rg_a[l], w_rg_x[l]], axis=-1).astype(BF16),
        "b_rg_a": row(b_rg_a[l]),
        "b_rg_x": row(b_rg_x[l]),
        "lru_lambda": row(lru_lambda[l]),
        "w_pool": w_pool[l].astype(BF16),
        "pool_scale": row(pool_scale[l]),
        "w_out": w_out[l],
        "norm_mlp_g": row(norm_mlp_g[l]),
        "w_up": w_up[l],
        "w_down": w_down[l],
        "norm_final_g": row(norm_final_g),
    }


def kernel(x_prompt, x_sample, state_lru_h, state_conv, state_pool, norm_mix_g, w_in, conv_w, conv_b, w_rg_a, b_rg_a, w_rg_x, b_rg_x, lru_lambda, w_pool, pool_scale, w_out, norm_mlp_g, w_up, w_down, norm_final_g):
    depth = w_in.shape[0]
    assert depth == 1, "final RMSNorm is fused into the (single) layer's MLP kernel"
    bp, tp, d = x_prompt.shape
    bs, ts, _ = x_sample.shape
    assert ts == 1 and tp % MIX_TM == 0 and (bp * tp) % MLP_TM == 0
    p = _layer_params(0, norm_mix_g, w_in, conv_w, conv_b, w_rg_a, b_rg_a, w_rg_x, b_rg_x,
                      lru_lambda, w_pool, pool_scale, w_out, norm_mlp_g, w_up, w_down, norm_final_g)

    hist_major = lambda s: jnp.transpose(s, (1, 0, 2))
    x1s, hs, cs, ps, p["w_in"], p["w_out"] = _dec_mixer(
        x_sample.reshape(bs, d), state_lru_h[0], hist_major(state_conv[0]), hist_major(state_pool[0]),
        p, p["w_in"], p["w_out"])
    cs, ps = hist_major(cs), hist_major(ps)
    x1p, hp, cp, pp, p["w_up"], p["w_down"] = _seq_mixer(x_prompt, p, p["w_up"], p["w_down"], MIX_TM)
    yp, ys = _mlp(x1p.reshape(bp * tp, d), x1s, p, MLP_TM, MLP_TF)
    yp, ys = yp.reshape(bp, tp, d), ys.reshape(bs, ts, d)

    return (yp, ys, hp.reshape(1, bp, -1), cp[None], pp[None], hs[None], cs[None], ps[None])
```

```python
import functools

import jax
import jax.numpy as jnp
from jax import lax
from jax.experimental import pallas as pl
from jax.experimental.pallas import tpu as pltpu

EPS = 1e-6
LRU_HEADS = 8
LRU_C = 8.0
CONV_W = 4
POOL_WINDOWS = (2, 4, 8, 16)
POOL_BUF = max(POOL_WINDOWS) - 1
PAST_LEN = 16384

SUBLANES = 8
MIB = 1024 * 1024

MIX_TM = 256
MLP_TM = 512
MLP_TF = 2048
MIX_VMEM_LIMIT = 62 * MIB
DEC_VMEM_LIMIT = 56 * MIB
MLP_VMEM_LIMIT = 60 * MIB

F32 = jnp.float32
BF16 = jnp.bfloat16


def _rmsnorm(x, g):
    return (x * lax.rsqrt(jnp.mean(x * x, axis=-1, keepdims=True) + EPS)) * g


def _dot(a, b):
    return jnp.dot(a, b, preferred_element_type=F32)


def _gate_update(g, xc, ba, bx, cneg):
    hd = xc.shape[1]
    r = jax.nn.sigmoid(g[:, :hd] + ba)
    i = jax.nn.sigmoid(g[:, hd:] + bx)
    a = jnp.exp(r * cneg)
    y = 1.0 - a * a
    mult = jnp.where(y > 0.0, y * lax.rsqrt(y), 0.0)
    return a, xc * i, mult


def _lru_gates(xc, wg_ref, ba, bx, cneg):
    hd = xc.shape[1] // LRU_HEADS
    xcb = xc.astype(BF16)
    a_parts, m_parts, xi_parts = [], [], []
    for h in range(LRU_HEADS):
        sl = slice(h * hd, (h + 1) * hd)
        g = _dot(xcb[:, sl], wg_ref[h])
        a, xi, mult = _gate_update(g, xc[:, sl], ba[:, sl], bx[:, sl], cneg[:, sl])
        a_parts.append(a)
        m_parts.append(mult)
        xi_parts.append(xi)
    cat = lambda ps: jnp.concatenate(ps, axis=1)
    return cat(a_parts), cat(m_parts), cat(xi_parts)


def _pool_project(pooled, wp_ref, scale):
    n_g = len(POOL_WINDOWS)
    gd = pooled.shape[1] // n_g
    pb = pooled.astype(BF16)
    outs = [_dot(pb[:, g * gd:(g + 1) * gd], wp_ref[g]) for g in range(n_g)]
    return jnp.concatenate(outs, axis=1) * scale


def _neg_c_softplus(lam):
    return -LRU_C * jax.nn.softplus(-lam)


CONV_HALO = CONV_W - 1
POOL_HALO = POOL_BUF
X_SLOTS = 3


def _rows(k, n=1):
    return pl.ds(k * SUBLANES, n * SUBLANES)


def _tile_copies(hbm, buf, sem, tile, n_t, slot, to_hbm):
    n_grp = buf.shape[1]
    tm = n_grp * SUBLANES
    b, t = tile // n_t, tile % n_t
    copies = []
    for s in range(SUBLANES):
        far = hbm.at[b, pl.ds(t * tm + s * n_grp, n_grp), :]
        near = buf.at[slot, :, s, :]
        src, dst = (near, far) if to_hbm else (far, near)
        copies.append(pltpu.make_async_copy(src, dst, sem.at[slot]))
    return copies


def _fill_halo(ext_ref, carry_ref, halo, n_grp):
    sub = lax.broadcasted_iota(jnp.int32, (SUBLANES, ext_ref.shape[1]), 0)
    for j in range(1, halo + 1):
        cur = ext_ref[_rows(halo + n_grp - j), :]
        prev = carry_ref[_rows(halo - j), :]
        ext_ref[_rows(halo - j), :] = pltpu.roll(jnp.where(sub == SUBLANES - 1, prev, cur), 1, 0)
    carry_ref[...] = ext_ref[_rows(n_grp, halo), :]


def _seq_mixer_kernel(x_hbm, gmix_ref, w_in_ref, cw_ref, cb_ref, wg_ref, ba_ref, bx_ref,
                      lam_ref, wp_ref, ps_ref, w_out_ref, wup32_ref, wdn32_ref,
                      o_hbm, hlast_ref, nconv_ref, npool_ref, wup16_ref, wdn16_ref,
                      xbuf, obuf, in_sem, out_sem, hy_ref, extx_ref, carryx_ref, extp_ref,
                      carryp_ref, nextx_ref, nextp_ref, nextg_ref, xc_ref, xcb_ref, gate_ref, ug_ref,
                      a_ref, b_ref, hc_ref, *, n_t):
    hn_ref = ymix_ref = hy_ref
    n_grp = xbuf.shape[1]
    tm = n_grp * SUBLANES
    d = xbuf.shape[3]
    w_lru = a_ref.shape[1]
    w_pool = extp_ref.shape[1]
    hd = w_lru // LRU_HEADS
    b = pl.program_id(0)
    t = pl.program_id(1)
    step = b * n_t + t
    n_steps = pl.num_programs(0) * n_t
    even = t % 2 == 0
    first_tile = step - t % 2

    fetch = lambda tile: _tile_copies(x_hbm, xbuf, in_sem, tile, n_t, tile % X_SLOTS, False)
    writeback = lambda tile: _tile_copies(o_hbm, obuf, out_sem, tile, n_t, tile % 2, True)

    @pl.when(step == 0)
    def _first_fetch():
        for c in fetch(0) + fetch(1):
            c.start()

    @pl.when(step + 2 < n_steps)
    def _prefetch():
        for c in fetch(step + 2):
            c.start()

    @pl.when(even)
    def _wait_pair():
        for c in fetch(step) + fetch(step + 1):
            c.wait()

    def x_pair():
        tiles = [xbuf[(first_tile + k) % X_SLOTS].reshape(tm, d) for k in range(2)]
        return jnp.concatenate(tiles, axis=0)

    def side_job():
        wup16_ref[...] = wup32_ref[...].astype(BF16)
        wdn16_ref[...] = wdn32_ref[...].astype(BF16)

    def mix_tile(row0):
        sub = lax.broadcasted_iota(jnp.int32, (SUBLANES, w_lru), 0)
        first_token = (sub == 0) & (t == 0)
        cneg = _neg_c_softplus(lam_ref[...])
        _fill_halo(extx_ref, carryx_ref, CONV_HALO, n_grp)
        _fill_halo(extp_ref, carryp_ref, POOL_HALO, n_grp)

        for gp in range(n_grp // 2):
            g = 2 * gp
            acc = cb_ref[...] + extx_ref[_rows(CONV_HALO + g, 2), :] * cw_ref[CONV_W - 1:CONV_W, :]
            for k in range(CONV_W - 1):
                acc = acc + extx_ref[_rows(g + k, 2), :] * cw_ref[k:k + 1, :]
            xc_ref[_rows(g, 2), :] = acc
            xcb_ref[_rows(g, 2), :] = acc.astype(BF16)
        for hh in range(LRU_HEADS):
            gate_ref[:, 2 * hd * hh:2 * hd * (hh + 1)] = _dot(xcb_ref[:, hd * hh:hd * (hh + 1)], wg_ref[hh])

        a_cum = h_loc = None
        for g in range(n_grp):
            gates = gate_ref[_rows(g), :]
            ga = jnp.concatenate([gates[:, 2 * hd * hh:2 * hd * hh + hd] for hh in range(LRU_HEADS)], axis=1)
            gx = jnp.concatenate([gates[:, 2 * hd * hh + hd:2 * hd * (hh + 1)] for hh in range(LRU_HEADS)], axis=1)
            a, xi, mult = _gate_update(jnp.concatenate([ga, gx], axis=1), xc_ref[_rows(g), :],
                                       ba_ref[...], bx_ref[...], cneg)
            if g == 0:
                mult = jnp.where(first_token, 1.0, mult)
            bt = xi * mult
            a_cum = a if a_cum is None else a * a_cum
            h_loc = bt if h_loc is None else a * h_loc + bt
            a_ref[_rows(g), :] = a_cum
            b_ref[_rows(g), :] = h_loc

        gd = w_pool // len(POOL_WINDOWS)
        row = lax.broadcasted_iota(jnp.int32, (tm, 1), 0)
        pos = t * tm + (row & (SUBLANES - 1)) * n_grp + (row >> 3)
        pooled = []
        for gi, w in enumerate(POOL_WINDOWS):
            cols = slice(gi * gd, (gi + 1) * gd)
            lo = POOL_HALO - (w - 1)
            s = extp_ref[_rows(lo, n_grp + w - 1), cols]
            width = 1
            while width < w:
                s = s[width * SUBLANES:, :] + s[:-width * SUBLANES, :]
                width *= 2
            inv = 1.0 / jnp.minimum(pos + 1, w).astype(F32)
            pooled.append(s * inv - extp_ref[_rows(POOL_HALO, n_grp), cols])
        y_pool = _pool_project(jnp.concatenate(pooled, axis=1), wp_ref, ps_ref[...])
        ymix_ref[pl.ds(row0, tm), w_lru:] = y_pool.astype(BF16)

        p_cum, q_cum = a_cum, h_loc
        for dd in (1, 2, 4):
            keep = sub >= dd
            q_cum = jnp.where(keep, p_cum * pltpu.roll(q_cum, dd, 0) + q_cum, q_cum)
            p_cum = jnp.where(keep, p_cum * pltpu.roll(p_cum, dd, 0), p_cum)
        h_in = jnp.broadcast_to(hc_ref[...], (SUBLANES, w_lru))
        seg_end = p_cum * h_in + q_cum
        seg_in = jnp.where(sub == 0, h_in, pltpu.roll(seg_end, 1, 0))
        hc_ref[...] = seg_end[SUBLANES - 1:SUBLANES, :]

        seg_in2 = jnp.concatenate([seg_in, seg_in], axis=0)
        for gp in range(n_grp // 2):
            r2 = _rows(2 * gp, 2)
            hs = b_ref[r2, :] + a_ref[r2, :] * seg_in2
            gated = hs * jax.nn.gelu(ug_ref[r2, :], approximate=True)
            ymix_ref[pl.ds(row0 + 2 * gp * SUBLANES, 2 * SUBLANES), 0:w_lru] = gated.astype(BF16)

    @pl.when(even)
    def _even_step():
        @pl.when(t == 0)
        def _reset_state():
            hc_ref[...] = jnp.zeros_like(hc_ref)
            carryx_ref[...] = jnp.zeros_like(carryx_ref)
            carryp_ref[...] = jnp.zeros_like(carryp_ref)

        hn_ref[...] = _rmsnorm(x_pair(), gmix_ref[...]).astype(BF16)
        in_proj = lambda c0, c1: _dot(hn_ref[...], w_in_ref[:, c0:c1])
        ux = in_proj(0, w_lru)
        extx_ref[_rows(CONV_HALO, n_grp), :] = ux[0:tm, :]
        nextx_ref[...] = ux[tm:, :]
        for c0 in (0, w_pool // 2):
            up = in_proj(2 * w_lru + c0, 2 * w_lru + c0 + w_pool // 2)
            extp_ref[_rows(POOL_HALO, n_grp), c0:c0 + w_pool // 2] = up[0:tm, :]
            nextp_ref[:, c0:c0 + w_pool // 2] = up[tm:, :]
        for c0 in (0, w_lru // 2):
            ugc = in_proj(w_lru + c0, w_lru + c0 + w_lru // 2)
            ug_ref[:, c0:c0 + w_lru // 2] = ugc[0:tm, :]
            nextg_ref[:, c0:c0 + w_lru // 2] = ugc[tm:, :]
        mix_tile(0)
        side_job()

        @pl.when(step >= 2)
        def _wait_writeback():
            for c in writeback(step - 2) + writeback(step - 1):
                c.wait()

        x2 = x_pair()
        for k in range(2):
            obuf[k] = x2[k * tm:(k + 1) * tm, :].reshape(n_grp, SUBLANES, d)

    @pl.when(jnp.logical_not(even))
    def _odd_step():
        extx_ref[_rows(CONV_HALO, n_grp), :] = nextx_ref[...]
        extp_ref[_rows(POOL_HALO, n_grp), :] = nextp_ref[...]
        ug_ref[...] = nextg_ref[...]

        mix_tile(tm)
        y = _dot(ymix_ref[:, w_lru:], w_out_ref[w_lru:, :]) + _dot(ymix_ref[:, 0:w_lru], w_out_ref[0:w_lru, :])
        for k in range(2):
            obuf[k] += y[k * tm:(k + 1) * tm, :].reshape(n_grp, SUBLANES, d)
        side_job()
        for c in writeback(step - 1) + writeback(step):
            c.start()

        @pl.when(step == n_steps - 1)
        def _drain():
            for c in writeback(step - 1) + writeback(step):
                c.wait()

        @pl.when(t == n_t - 1)
        def _emit_state():
            last = SUBLANES - 1
            hlast_ref[...] = hc_ref[...]
            for i in range(CONV_HALO):
                nconv_ref[i:i + 1, :] = carryx_ref[pl.ds(i * SUBLANES + last, 1), :]
            for i in range(POOL_HALO):
                npool_ref[i:i + 1, :] = carryp_ref[pl.ds(i * SUBLANES + last, 1), :]


def _const_spec(shape):
    nd = len(shape)
    return pl.BlockSpec(shape, lambda *_: (0,) * nd, pipeline_mode=pl.Buffered(1))


def _seq_mixer(x, p, w_up, w_down, tm):
    bsz, seq, d = x.shape
    w_lru = p["conv_b"].shape[1]
    w_pool = p["pool_scale"].shape[1]
    weights = (p["norm_mix_g"], p["w_in"], p["conv_w"], p["conv_b"], p["w_gate"], p["b_rg_a"],
               p["b_rg_x"], p["lru_lambda"], p["w_pool"], p["pool_scale"], p["w_out"])
    n_t = seq // tm
    n_steps = bsz * n_t
    n_grp = tm // SUBLANES
    assert n_grp % 2 == 0 and n_grp >= POOL_HALO and n_t % 2 == 0
    any_spec = pl.BlockSpec(memory_space=pl.ANY)
    state_spec = lambda rows, w: pl.BlockSpec((None, rows, w), lambda b, t: (b, 0, 0))
    slab_spec = lambda w: pl.BlockSpec((w.shape[0] // n_steps, w.shape[1]),
                                       lambda b, t: (b * n_t + t, 0))
    ext_rows = lambda halo: (halo + n_grp) * SUBLANES
    return pl.pallas_call(
        functools.partial(_seq_mixer_kernel, n_t=n_t),
        grid=(bsz, n_t),
        in_specs=([any_spec] + [_const_spec(w.shape) for w in weights]
                  + [slab_spec(w_up), slab_spec(w_down)]),
        out_specs=[any_spec, state_spec(1, w_lru), state_spec(CONV_W - 1, w_lru),
                   state_spec(POOL_BUF, w_pool), slab_spec(w_up), slab_spec(w_down)],
        out_shape=[jax.ShapeDtypeStruct((bsz, seq, d), F32),
                   jax.ShapeDtypeStruct((bsz, 1, w_lru), F32),
                   jax.ShapeDtypeStruct((bsz, CONV_W - 1, w_lru), F32),
                   jax.ShapeDtypeStruct((bsz, POOL_BUF, w_pool), F32),
                   jax.ShapeDtypeStruct(w_up.shape, BF16),
                   jax.ShapeDtypeStruct(w_down.shape, BF16)],
        scratch_shapes=[
            pltpu.VMEM((X_SLOTS, n_grp, SUBLANES, d), F32),
            pltpu.VMEM((2, n_grp, SUBLANES, d), F32),
            pltpu.SemaphoreType.DMA((X_SLOTS,)),
            pltpu.SemaphoreType.DMA((2,)),
            pltpu.VMEM((2 * tm, d), BF16),
            pltpu.VMEM((ext_rows(CONV_HALO), w_lru), F32),
            pltpu.VMEM((CONV_HALO * SUBLANES, w_lru), F32),
            pltpu.VMEM((ext_rows(POOL_HALO), w_pool), F32),
            pltpu.VMEM((POOL_HALO * SUBLANES, w_pool), F32),
            pltpu.VMEM((tm, w_lru), F32),
            pltpu.VMEM((tm, w_pool), F32),
            pltpu.VMEM((tm, w_lru), F32),
            pltpu.VMEM((tm, w_lru), F32),
            pltpu.VMEM((tm, w_lru), BF16),
            pltpu.VMEM((tm, 2 * w_lru), F32),
            pltpu.VMEM((tm, w_lru), F32),
            pltpu.VMEM((tm, w_lru), F32),
            pltpu.VMEM((tm, w_lru), F32),
            pltpu.VMEM((1, w_lru), F32),
        ],
        compiler_params=pltpu.CompilerParams(
            dimension_semantics=("arbitrary", "arbitrary"),
            vmem_limit_bytes=MIX_VMEM_LIMIT),
        name="seq_mixer",
    )(x, *weights, w_up, w_down)


DEC_CHUNK = 512


def _dec_mixer_kernel(x_ref, xcol_ref, h0_ref, cbuf_hbm, pbuf_hbm, gmix_ref, w_in_ref, cw_ref, cb_ref,
                      wg_ref, ba_ref, bx_ref, lam_ref, wp_ref, ps_ref, w_out_ref,
                      o_ref, hnew_ref, nconv_hbm, npool_hbm, w_in16_ref, w_out16_ref,
                      hn_ref, proj_ref, ymix_ref, cbuf_ref, pbuf_ref, nconv_ref, npool_ref, state_sem,
                      *, n_in):
    j = pl.program_id(0)
    w_lru = h0_ref.shape[1]
    w_pool = ps_ref.shape[1]
    state_in = [pltpu.make_async_copy(cbuf_hbm, cbuf_ref, state_sem.at[0]),
                pltpu.make_async_copy(pbuf_hbm, pbuf_ref, state_sem.at[1])]
    state_out = [pltpu.make_async_copy(nconv_ref, nconv_hbm, state_sem.at[2]),
                 pltpu.make_async_copy(npool_ref, npool_hbm, state_sem.at[3])]

    @pl.when(j == 0)
    def _norm():
        for c in state_in:
            c.start()
        hn_ref[...] = _rmsnorm(x_ref[...], gmix_ref[...]).astype(BF16)

    @pl.when(j < n_in)
    def _in_projection():
        w = w_in_ref[...].astype(BF16)
        w_in16_ref[...] = w
        proj_ref[j] = _dot(hn_ref[...], w)

    @pl.when(j == n_in)
    def _mix():
        per = w_lru // DEC_CHUNK
        branch = lambda k: jnp.concatenate([proj_ref[k * per + c] for c in range(per)], axis=1)
        ux, ug, up = branch(0), branch(1), branch(2)
        for c in state_in:
            c.wait()

        xc = cb_ref[...] + ux * cw_ref[CONV_W - 1:CONV_W, :]
        for k in range(CONV_W - 1):
            tap = cbuf_ref[k]
            xc = xc + tap * cw_ref[k:k + 1, :]
            if k > 0:
                nconv_ref[k - 1] = tap
        nconv_ref[CONV_W - 2] = ux

        a, mult, xi = _lru_gates(xc, wg_ref, ba_ref[...], bx_ref[...], _neg_c_softplus(lam_ref[...]))
        h_new = a * h0_ref[...] + xi * mult
        hnew_ref[...] = h_new
        ymix_ref[:, 0:w_lru] = (h_new * jax.nn.gelu(ug, approximate=True)).astype(BF16)

        gd = w_pool // len(POOL_WINDOWS)
        s = up
        width = 1
        pooled = []
        for g, w in enumerate(POOL_WINDOWS):
            while width < w:
                k = POOL_BUF - width
                hist = pbuf_ref[k]
                if k > 0:
                    npool_ref[k - 1] = hist
                s = s + hist[:, g * gd:]
                width += 1
            count = float(min(PAST_LEN + 1, w))
            pooled.append(s[:, 0:gd] / count - up[:, g * gd:(g + 1) * gd])
            if g + 1 < len(POOL_WINDOWS):
                s = s[:, gd:]
        npool_ref[POOL_BUF - 1] = up
        y_pool = _pool_project(jnp.concatenate(pooled, axis=1), wp_ref, ps_ref[...])
        ymix_ref[:, w_lru:] = y_pool.astype(BF16)
        for c in state_out:
            c.start()

    @pl.when(j >= n_in)
    def _out_projection():
        w = w_out_ref[...].astype(BF16)
        w_out16_ref[...] = w
        o_ref[...] = xcol_ref[...] + _dot(ymix_ref[...], w)

    @pl.when(j == pl.num_programs(0) - 1)
    def _drain_state():
        for c in state_out:
            c.wait()


def _dec_mixer(x, h0, cbuf, pbuf, p, w_in, w_out):
    rows, d = x.shape
    n_in = w_in.shape[1] // DEC_CHUNK
    n_out = w_out.shape[1] // DEC_CHUNK
    small = (p["conv_w"], p["conv_b"], p["w_gate"], p["b_rg_a"], p["b_rg_x"], p["lru_lambda"],
             p["w_pool"], p["pool_scale"])
    full = lambda a: pl.BlockSpec(a.shape, functools.partial(lambda nd, j: (0,) * nd, a.ndim))
    once = lambda a: _const_spec(a.shape)
    in_hbm = pl.BlockSpec(memory_space=pl.ANY)
    in_chunk = lambda j: (0, jnp.minimum(j, n_in - 1))
    out_chunk = lambda j: (0, jnp.maximum(j - n_in, 0))
    return pl.pallas_call(
        functools.partial(_dec_mixer_kernel, n_in=n_in),
        grid=(n_in + n_out,),
        in_specs=([once(x), pl.BlockSpec((rows, DEC_CHUNK), out_chunk), once(h0), in_hbm, in_hbm,
                   once(p["norm_mix_g"]), pl.BlockSpec((d, DEC_CHUNK), in_chunk)]
                  + [once(a) for a in small]
                  + [pl.BlockSpec((w_out.shape[0], DEC_CHUNK), out_chunk)]),
        out_specs=[pl.BlockSpec((rows, DEC_CHUNK), out_chunk), full(h0), in_hbm, in_hbm,
                   pl.BlockSpec((d, DEC_CHUNK), in_chunk),
                   pl.BlockSpec((w_out.shape[0], DEC_CHUNK), out_chunk)],
        out_shape=[jax.ShapeDtypeStruct(s.shape, F32) for s in (x, h0, cbuf, pbuf)]
                  + [jax.ShapeDtypeStruct(w_in.shape, BF16), jax.ShapeDtypeStruct(w_out.shape, BF16)],
        scratch_shapes=[
            pltpu.VMEM((rows, d), BF16),
            pltpu.VMEM((n_in, rows, DEC_CHUNK), F32),
            pltpu.VMEM((rows, w_out.shape[0]), BF16),
            pltpu.VMEM(cbuf.shape, F32),
            pltpu.VMEM(pbuf.shape, F32),
            pltpu.VMEM(cbuf.shape, F32),
            pltpu.VMEM(pbuf.shape, F32),
            pltpu.SemaphoreType.DMA((4,)),
        ],
        compiler_params=pltpu.CompilerParams(
            dimension_semantics=("arbitrary",), vmem_limit_bytes=DEC_VMEM_LIMIT),
        name="dec_mixer",
    )(x, x, h0, cbuf, pbuf, p["norm_mix_g"], w_in, *small, w_out)


def _mlp_kernel(x_ref, xs_ref, g_ref, wup_ref, wdn_ref, gfin_ref, o_ref, os_ref, h_ref):
    tm = x_ref.shape[0]
    i = pl.program_id(0)
    j = pl.program_id(1)
    first = j == 0
    last = j == pl.num_programs(1) - 1
    with_decode = i == 0

    def _start(src_ref, rows, acc_ref):
        x = src_ref[...]
        h_ref[rows, :] = _rmsnorm(x, g_ref[...]).astype(BF16)
        acc_ref[...] = x

    def _ffn(rows):
        f = jnp.maximum(_dot(h_ref[rows, :], wup_ref[...]), 0.0)
        return _dot((f * f).astype(BF16), wdn_ref[...])

    pl.when(first)(lambda: _start(x_ref, pl.ds(0, tm), o_ref))
    pl.when(first & with_decode)(lambda: _start(xs_ref, pl.ds(tm, xs_ref.shape[0]), os_ref))

    @pl.when(with_decode)
    def _both_streams():
        part = _ffn(pl.ds(0, h_ref.shape[0]))
        o_ref[...] += part[:tm]
        os_ref[...] += part[tm:]

    @pl.when(jnp.logical_not(with_decode))
    def _prompt_only():
        o_ref[...] += _ffn(pl.ds(0, tm))

    @pl.when(last)
    def _finish():
        o_ref[...] = _rmsnorm(o_ref[...], gfin_ref[...])

    @pl.when(last & with_decode)
    def _finish_decode():
        os_ref[...] = _rmsnorm(os_ref[...], gfin_ref[...])


def _mlp(x, xs, p, tm, tf):
    n, d = x.shape
    ns = xs.shape[0]
    d_ff = p["w_up"].shape[1]
    vec_spec = pl.BlockSpec((1, d), lambda i, j: (0, 0))
    tok_spec = pl.BlockSpec((tm, d), lambda i, j: (i, 0))
    dec_spec = pl.BlockSpec((ns, d), lambda i, j: (0, 0))
    return pl.pallas_call(
        _mlp_kernel,
        grid=(n // tm, d_ff // tf),
        in_specs=[tok_spec, dec_spec, vec_spec,
                  pl.BlockSpec((d, tf), lambda i, j: (0, j)),
                  pl.BlockSpec((tf, d), lambda i, j: (j, 0)),
                  vec_spec],
        out_specs=[tok_spec, dec_spec],
        out_shape=[jax.ShapeDtypeStruct((n, d), F32), jax.ShapeDtypeStruct((ns, d), F32)],
        scratch_shapes=[pltpu.VMEM((tm + ns, d), BF16)],
        compiler_params=pltpu.CompilerParams(
            dimension_semantics=("arbitrary", "arbitrary"),
            vmem_limit_bytes=MLP_VMEM_LIMIT),
        name="mlp",
    )(x, xs, p["norm_mlp_g"], p["w_up"], p["w_down"], p["norm_final_g"])


def _layer_params(l, norm_mix_g, w_in, conv_w, conv_b, w_rg_a, b_rg_a, w_rg_x, b_rg_x, lru_lambda,
                  w_pool, pool_scale, w_out, norm_mlp_g, w_up, w_down, norm_final_g):
    row = lambda v: v.reshape(1, -1)
    return {
        "norm_mix_g": row(norm_mix_g[l]),
        "w_in": w_in[l],
        "conv_w": conv_w[l],
        "conv_b": row(conv_b[l]),
        "w_gate": jnp.concatenate([w_rg_a[l], w_rg_x[l]], axis=-1).astype(BF16),
        "b_rg_a": row(b_rg_a[l]),
        "b_rg_x": row(b_rg_x[l]),
        "lru_lambda": row(lru_lambda[l]),
        "w_pool": w_pool[l].astype(BF16),
        "pool_scale": row(pool_scale[l]),
        "w_out": w_out[l],
        "norm_mlp_g": row(norm_mlp_g[l]),
        "w_up": w_up[l],
        "w_down": w_down[l],
        "norm_final_g": row(norm_final_g),
    }


def kernel(x_prompt, x_sample, state_lru_h, state_conv, state_pool, norm_mix_g, w_in, conv_w, conv_b, w_rg_a, b_rg_a, w_rg_x, b_rg_x, lru_lambda, w_pool, pool_scale, w_out, norm_mlp_g, w_up, w_down, norm_final_g):
    depth = w_in.shape[0]
    assert depth == 1, "final RMSNorm is fused into the (single) layer's MLP kernel"
    bp, tp, d = x_prompt.shape
    bs, ts, _ = x_sample.shape
    assert ts == 1 and tp % MIX_TM == 0 and (bp * tp) % MLP_TM == 0
    p = _layer_params(0, norm_mix_g, w_in, conv_w, conv_b, w_rg_a, b_rg_a, w_rg_x, b_rg_x,
                      lru_lambda, w_pool, pool_scale, w_out, norm_mlp_g, w_up, w_down, norm_final_g)

    hist_major = lambda s: jnp.transpose(s, (1, 0, 2))
    x1s, hs, cs, ps, p["w_in"], p["w_out"] = _dec_mixer(
        x_sample.reshape(bs, d), state_lru_h[0], hist_major(state_conv[0]), hist_major(state_pool[0]),
        p, p["w_in"], p["w_out"])
    cs, ps = hist_major(cs), hist_major(ps)
    x1p, hp, cp, pp, p["w_up"], p["w_down"] = _seq_mixer(x_prompt, p, p["w_up"], p["w_down"], MIX_TM)
    yp, ys = _mlp(x1p.reshape(bp * tp, d), x1s, p, MLP_TM, MLP_TF)
    yp, ys = yp.reshape(bp, tp, d), ys.reshape(bs, ts, d)

    return (yp, ys, hp.reshape(1, bp, -1), cp[None], pp[None], hs[None], cs[None], ps[None])
```

```python
import functools

import jax
import jax.numpy as jnp
from jax import lax
from jax.experimental import pallas as pl
from jax.experimental.pallas import tpu as pltpu

EPS = 1e-6
LRU_HEADS = 8
LRU_C = 8.0
CONV_W = 4
POOL_WINDOWS = (2, 4, 8, 16)
POOL_BUF = max(POOL_WINDOWS) - 1
PAST_LEN = 16384

SUBLANES = 8
MIB = 1024 * 1024

MIX_TM = 256
MLP_TM = 512
MLP_TF = 2048
MIX_VMEM_LIMIT = 60 * MIB
DEC_VMEM_LIMIT = 56 * MIB
MLP_VMEM_LIMIT = 60 * MIB

F32 = jnp.float32
BF16 = jnp.bfloat16


def _rmsnorm(x, g):
    return (x * lax.rsqrt(jnp.mean(x * x, axis=-1, keepdims=True) + EPS)) * g


def _dot(a, b):
    return jnp.dot(a, b, preferred_element_type=F32)


def _gate_update(g, xc, ba, bx, cneg):
    hd = xc.shape[1]
    r = jax.nn.sigmoid(g[:, :hd] + ba)
    i = jax.nn.sigmoid(g[:, hd:] + bx)
    a = jnp.exp(r * cneg)
    y = 1.0 - a * a
    mult = jnp.where(y > 0.0, y * lax.rsqrt(y), 0.0)
    return a, xc * i, mult


def _lru_gates(xc, wg_ref, ba, bx, cneg):
    hd = xc.shape[1] // LRU_HEADS
    xcb = xc.astype(BF16)
    a_parts, m_parts, xi_parts = [], [], []
    for h in range(LRU_HEADS):
        sl = slice(h * hd, (h + 1) * hd)
        g = _dot(xcb[:, sl], wg_ref[h])
        a, xi, mult = _gate_update(g, xc[:, sl], ba[:, sl], bx[:, sl], cneg[:, sl])
        a_parts.append(a)
        m_parts.append(mult)
        xi_parts.append(xi)
    cat = lambda ps: jnp.concatenate(ps, axis=1)
    return cat(a_parts), cat(m_parts), cat(xi_parts)


def _pool_project(pooled, wp_ref, scale):
    n_g = len(POOL_WINDOWS)
    gd = pooled.shape[1] // n_g
    pb = pooled.astype(BF16)
    outs = [_dot(pb[:, g * gd:(g + 1) * gd], wp_ref[g]) for g in range(n_g)]
    return jnp.concatenate(outs, axis=1) * scale


def _neg_c_softplus(lam):
    return -LRU_C * jax.nn.softplus(-lam)


CONV_HALO = CONV_W - 1
POOL_HALO = POOL_BUF


def _rows(k, n=1):
    return pl.ds(k * SUBLANES, n * SUBLANES)


def _tile_copies(hbm, buf, sem, b, t, slot, n_grp, to_hbm):
    tm = n_grp * SUBLANES
    copies = []
    for s in range(SUBLANES):
        far = hbm.at[b, pl.ds(t * tm + s * n_grp, n_grp), :]
        near = buf.at[slot, :, s, :]
        src, dst = (near, far) if to_hbm else (far, near)
        copies.append(pltpu.make_async_copy(src, dst, sem.at[slot]))
    return copies


def _fill_halo(ext_ref, carry_ref, halo, n_grp):
    sub = lax.broadcasted_iota(jnp.int32, (SUBLANES, ext_ref.shape[1]), 0)
    for j in range(1, halo + 1):
        cur = ext_ref[_rows(halo + n_grp - j), :]
        prev = carry_ref[_rows(halo - j), :]
        ext_ref[_rows(halo - j), :] = pltpu.roll(jnp.where(sub == SUBLANES - 1, prev, cur), 1, 0)
    carry_ref[...] = ext_ref[_rows(n_grp, halo), :]


def _seq_mixer_kernel(x_hbm, gmix_ref, w_in_ref, cw_ref, cb_ref, wg_ref, ba_ref, bx_ref,
                      lam_ref, wp_ref, ps_ref, w_out_ref, wup32_ref, wdn32_ref,
                      o_hbm, hlast_ref, nconv_ref, npool_ref, wup16_ref, wdn16_ref,
                      xbuf, obuf, in_sem, out_sem, hn_ref, extx_ref, carryx_ref, extp_ref,
                      carryp_ref, xc_ref, xcb_ref, gate_ref, ug_ref, a_ref, b_ref, ymix_ref, hc_ref,
                      *, n_t):
    n_grp = xbuf.shape[1]
    tm = n_grp * SUBLANES
    d = xbuf.shape[3]
    w_lru = a_ref.shape[1]
    w_pool = extp_ref.shape[1]
    hd = w_lru // LRU_HEADS
    b = pl.program_id(0)
    t = pl.program_id(1)
    step = b * n_t + t
    n_steps = pl.num_programs(0) * n_t
    slot = step % 2

    @pl.when(step == 0)
    def _first_fetch():
        for c in _tile_copies(x_hbm, xbuf, in_sem, b, t, slot, n_grp, False):
            c.start()

    @pl.when(step + 1 < n_steps)
    def _prefetch():
        nxt = step + 1
        for c in _tile_copies(x_hbm, xbuf, in_sem, nxt // n_t, nxt % n_t, 1 - slot, n_grp, False):
            c.start()

    for c in _tile_copies(x_hbm, xbuf, in_sem, b, t, slot, n_grp, False):
        c.wait()

    @pl.when(t == 0)
    def _reset_state():
        hc_ref[...] = jnp.zeros_like(hc_ref)
        carryx_ref[...] = jnp.zeros_like(carryx_ref)
        carryp_ref[...] = jnp.zeros_like(carryp_ref)

    x_tile = lambda: xbuf[slot].reshape(tm, d)
    hn_ref[...] = _rmsnorm(x_tile(), gmix_ref[...]).astype(BF16)
    in_proj = lambda c0, c1: _dot(hn_ref[...], w_in_ref[:, c0:c1])
    sub = lax.broadcasted_iota(jnp.int32, (SUBLANES, w_lru), 0)
    first_token = (sub == 0) & (t == 0)
    cneg = _neg_c_softplus(lam_ref[...])

    extx_ref[_rows(CONV_HALO, n_grp), :] = in_proj(0, w_lru)
    _fill_halo(extx_ref, carryx_ref, CONV_HALO, n_grp)

    extp_ref[_rows(POOL_HALO, n_grp), :] = in_proj(2 * w_lru, 2 * w_lru + w_pool)
    for gp in range(n_grp // 2):
        g = 2 * gp
        acc = cb_ref[...] + extx_ref[_rows(CONV_HALO + g, 2), :] * cw_ref[CONV_W - 1:CONV_W, :]
        for k in range(CONV_W - 1):
            acc = acc + extx_ref[_rows(g + k, 2), :] * cw_ref[k:k + 1, :]
        xc_ref[_rows(g, 2), :] = acc
        xcb_ref[_rows(g, 2), :] = acc.astype(BF16)
    for hh in range(LRU_HEADS):
        gate_ref[:, 2 * hd * hh:2 * hd * (hh + 1)] = _dot(xcb_ref[:, hd * hh:hd * (hh + 1)], wg_ref[hh])

    _fill_halo(extp_ref, carryp_ref, POOL_HALO, n_grp)
    ug_ref[...] = in_proj(w_lru, 2 * w_lru)

    def _scan_groups(g0, g1, a_cum, h_loc):
        for g in range(g0, g1):
            gates = gate_ref[_rows(g), :]
            ga = jnp.concatenate([gates[:, 2 * hd * hh:2 * hd * hh + hd] for hh in range(LRU_HEADS)], axis=1)
            gx = jnp.concatenate([gates[:, 2 * hd * hh + hd:2 * hd * (hh + 1)] for hh in range(LRU_HEADS)], axis=1)
            a, xi, mult = _gate_update(jnp.concatenate([ga, gx], axis=1), xc_ref[_rows(g), :],
                                       ba_ref[...], bx_ref[...], cneg)
            if g == 0:
                mult = jnp.where(first_token, 1.0, mult)
            bt = xi * mult
            a_cum = a if a_cum is None else a * a_cum
            h_loc = bt if h_loc is None else a * h_loc + bt
            a_ref[_rows(g), :] = a_cum
            b_ref[_rows(g), :] = h_loc
        return a_cum, h_loc

    a_tot, h_tot = _scan_groups(0, n_grp, None, None)

    gd = w_pool // len(POOL_WINDOWS)
    row = lax.broadcasted_iota(jnp.int32, (tm, 1), 0)
    pos = t * tm + (row & (SUBLANES - 1)) * n_grp + (row >> 3)
    pooled = []
    for gi, w in enumerate(POOL_WINDOWS):
        cols = slice(gi * gd, (gi + 1) * gd)
        lo = POOL_HALO - (w - 1)
        s = extp_ref[_rows(lo, n_grp + w - 1), cols]
        width = 1
        while width < w:
            s = s[width * SUBLANES:, :] + s[:-width * SUBLANES, :]
            width *= 2
        inv = 1.0 / jnp.minimum(pos + 1, w).astype(F32)
        pooled.append(s * inv - extp_ref[_rows(POOL_HALO, n_grp), cols])

    y_pool = _pool_project(jnp.concatenate(pooled, axis=1), wp_ref, ps_ref[...])
    ymix_ref[:, w_lru:] = y_pool.astype(BF16)
    obuf[slot] = (x_tile() + _dot(ymix_ref[:, w_lru:], w_out_ref[w_lru:, :])).reshape(n_grp, SUBLANES, d)

    p_cum, q_cum = a_tot, h_tot
    for dd in (1, 2, 4):
        keep = sub >= dd
        q_cum = jnp.where(keep, p_cum * pltpu.roll(q_cum, dd, 0) + q_cum, q_cum)
        p_cum = jnp.where(keep, p_cum * pltpu.roll(p_cum, dd, 0), p_cum)
    h_in = jnp.broadcast_to(hc_ref[...], (SUBLANES, w_lru))
    seg_end = p_cum * h_in + q_cum
    seg_in = jnp.where(sub == 0, h_in, pltpu.roll(seg_end, 1, 0))
    h_end = seg_end[SUBLANES - 1:SUBLANES, :]
    hc_ref[...] = h_end

    seg_in2 = jnp.concatenate([seg_in, seg_in], axis=0)
    for gp in range(n_grp // 2):
        r2 = _rows(2 * gp, 2)
        hs = b_ref[r2, :] + a_ref[r2, :] * seg_in2
        ymix_ref[r2, 0:w_lru] = (hs * jax.nn.gelu(ug_ref[r2, :], approximate=True)).astype(BF16)

    obuf[slot] += _dot(ymix_ref[:, 0:w_lru], w_out_ref[0:w_lru, :]).reshape(n_grp, SUBLANES, d)

    wup16_ref[...] = wup32_ref[...].astype(BF16)
    wdn16_ref[...] = wdn32_ref[...].astype(BF16)

    for c in _tile_copies(o_hbm, obuf, out_sem, b, t, slot, n_grp, True):
        c.start()

    @pl.when(step > 0)
    def _wait_prev_writeback():
        prv = step - 1
        for c in _tile_copies(o_hbm, obuf, out_sem, prv // n_t, prv % n_t, 1 - slot, n_grp, True):
            c.wait()

    @pl.when(step == n_steps - 1)
    def _wait_last_writeback():
        for c in _tile_copies(o_hbm, obuf, out_sem, b, t, slot, n_grp, True):
            c.wait()

    @pl.when(t == n_t - 1)
    def _emit_state():
        last = SUBLANES - 1
        hlast_ref[...] = h_end
        for i in range(CONV_HALO):
            nconv_ref[i:i + 1, :] = carryx_ref[pl.ds(i * SUBLANES + last, 1), :]
        for i in range(POOL_HALO):
            npool_ref[i:i + 1, :] = carryp_ref[pl.ds(i * SUBLANES + last, 1), :]


def _const_spec(shape):
    nd = len(shape)
    return pl.BlockSpec(shape, lambda *_: (0,) * nd, pipeline_mode=pl.Buffered(1))


def _seq_mixer(x, p, w_up, w_down, tm):
    bsz, seq, d = x.shape
    w_lru = p["conv_b"].shape[1]
    w_pool = p["pool_scale"].shape[1]
    weights = (p["norm_mix_g"], p["w_in"], p["conv_w"], p["conv_b"], p["w_gate"], p["b_rg_a"],
               p["b_rg_x"], p["lru_lambda"], p["w_pool"], p["pool_scale"], p["w_out"])
    n_t = seq // tm
    n_steps = bsz * n_t
    n_grp = tm // SUBLANES
    assert n_grp % 2 == 0 and n_grp >= POOL_HALO
    any_spec = pl.BlockSpec(memory_space=pl.ANY)
    state_spec = lambda rows, w: pl.BlockSpec((None, rows, w), lambda b, t: (b, 0, 0))
    slab_spec = lambda w: pl.BlockSpec((w.shape[0] // n_steps, w.shape[1]),
                                       lambda b, t: (b * n_t + t, 0))
    ext_rows = lambda halo: (halo + n_grp) * SUBLANES
    return pl.pallas_call(
        functools.partial(_seq_mixer_kernel, n_t=n_t),
        grid=(bsz, n_t),
        in_specs=([any_spec] + [_const_spec(w.shape) for w in weights]
                  + [slab_spec(w_up), slab_spec(w_down)]),
        out_specs=[any_spec, state_spec(1, w_lru), state_spec(CONV_W - 1, w_lru),
                   state_spec(POOL_BUF, w_pool), slab_spec(w_up), slab_spec(w_down)],
        out_shape=[jax.ShapeDtypeStruct((bsz, seq, d), F32),
                   jax.ShapeDtypeStruct((bsz, 1, w_lru), F32),
                   jax.ShapeDtypeStruct((bsz, CONV_W - 1, w_lru), F32),
                   jax.ShapeDtypeStruct((bsz, POOL_BUF, w_pool), F32),
                   jax.ShapeDtypeStruct(w_up.shape, BF16),
                   jax.ShapeDtypeStruct(w_down.shape, BF16)],
        scratch_shapes=[
            pltpu.VMEM((2, n_grp, SUBLANES, d), F32),
            pltpu.VMEM((2, n_grp, SUBLANES, d), F32),
            pltpu.SemaphoreType.DMA((2,)),
            pltpu.SemaphoreType.DMA((2,)),
            pltpu.VMEM((tm, d), BF16),
            pltpu.VMEM((ext_rows(CONV_HALO), w_lru), F32),
            pltpu.VMEM((CONV_HALO * SUBLANES, w_lru), F32),
            pltpu.VMEM((ext_rows(POOL_HALO), w_pool), F32),
            pltpu.VMEM((POOL_HALO * SUBLANES, w_pool), F32),
            pltpu.VMEM((tm, w_lru), F32),
            pltpu.VMEM((tm, w_lru), BF16),
            pltpu.VMEM((tm, 2 * w_lru), F32),
            pltpu.VMEM((tm, w_lru), F32),
            pltpu.VMEM((tm, w_lru), F32),
            pltpu.VMEM((tm, w_lru), F32),
            pltpu.VMEM((tm, d), BF16),
            pltpu.VMEM((1, w_lru), F32),
        ],
        compiler_params=pltpu.CompilerParams(
            dimension_semantics=("arbitrary", "arbitrary"),
            vmem_limit_bytes=MIX_VMEM_LIMIT),
        name="seq_mixer",
    )(x, *weights, w_up, w_down)


DEC_CHUNK = 512


def _dec_mixer_kernel(x_ref, xcol_ref, h0_ref, cbuf_hbm, pbuf_hbm, gmix_ref, w_in_ref, cw_ref, cb_ref,
                      wg_ref, ba_ref, bx_ref, lam_ref, wp_ref, ps_ref, w_out_ref,
                      o_ref, hnew_ref, nconv_hbm, npool_hbm, w_in16_ref, w_out16_ref,
                      hn_ref, proj_ref, ymix_ref, cbuf_ref, pbuf_ref, nconv_ref, npool_ref, state_sem,
                      *, n_in):
    j = pl.program_id(0)
    w_lru = h0_ref.shape[1]
    w_pool = ps_ref.shape[1]
    state_in = [pltpu.make_async_copy(cbuf_hbm, cbuf_ref, state_sem.at[0]),
                pltpu.make_async_copy(pbuf_hbm, pbuf_ref, state_sem.at[1])]
    state_out = [pltpu.make_async_copy(nconv_ref, nconv_hbm, state_sem.at[2]),
                 pltpu.make_async_copy(npool_ref, npool_hbm, state_sem.at[3])]

    @pl.when(j == 0)
    def _norm():
        for c in state_in:
            c.start()
        hn_ref[...] = _rmsnorm(x_ref[...], gmix_ref[...]).astype(BF16)

    @pl.when(j < n_in)
    def _in_projection():
        w = w_in_ref[...].astype(BF16)
        w_in16_ref[...] = w
        proj_ref[j] = _dot(hn_ref[...], w)

    @pl.when(j == n_in)
    def _mix():
        per = w_lru // DEC_CHUNK
        branch = lambda k: jnp.concatenate([proj_ref[k * per + c] for c in range(per)], axis=1)
        ux, ug, up = branch(0), branch(1), branch(2)
        for c in state_in:
            c.wait()

        xc = cb_ref[...] + ux * cw_ref[CONV_W - 1:CONV_W, :]
        for k in range(CONV_W - 1):
            tap = cbuf_ref[k]
            xc = xc + tap * cw_ref[k:k + 1, :]
            if k > 0:
                nconv_ref[k - 1] = tap
        nconv_ref[CONV_W - 2] = ux

        a, mult, xi = _lru_gates(xc, wg_ref, ba_ref[...], bx_ref[...], _neg_c_softplus(lam_ref[...]))
        h_new = a * h0_ref[...] + xi * mult
        hnew_ref[...] = h_new
        ymix_ref[:, 0:w_lru] = (h_new * jax.nn.gelu(ug, approximate=True)).astype(BF16)

        gd = w_pool // len(POOL_WINDOWS)
        s = up
        width = 1
        pooled = []
        for g, w in enumerate(POOL_WINDOWS):
            while width < w:
                k = POOL_BUF - width
                hist = pbuf_ref[k]
                if k > 0:
                    npool_ref[k - 1] = hist
                s = s + hist[:, g * gd:]
                width += 1
            count = float(min(PAST_LEN + 1, w))
            pooled.append(s[:, 0:gd] / count - up[:, g * gd:(g + 1) * gd])
            if g + 1 < len(POOL_WINDOWS):
                s = s[:, gd:]
        npool_ref[POOL_BUF - 1] = up
        y_pool = _pool_project(jnp.concatenate(pooled, axis=1), wp_ref, ps_ref[...])
        ymix_ref[:, w_lru:] = y_pool.astype(BF16)
        for c in state_out:
            c.start()

    @pl.when(j >= n_in)
    def _out_projection():
        w = w_out_ref[...].astype(BF16)
        w_out16_ref[...] = w
        o_ref[...] = xcol_ref[...] + _dot(ymix_ref[...], w)

    @pl.when(j == pl.num_programs(0) - 1)
    def _drain_state():
        for c in state_out:
            c.wait()


def _dec_mixer(x, h0, cbuf, pbuf, p, w_in, w_out):
    rows, d = x.shape
    n_in = w_in.shape[1] // DEC_CHUNK
    n_out = w_out.shape[1] // DEC_CHUNK
    small = (p["conv_w"], p["conv_b"], p["w_gate"], p["b_rg_a"], p["b_rg_x"], p["lru_lambda"],
             p["w_pool"], p["pool_scale"])
    full = lambda a: pl.BlockSpec(a.shape, functools.partial(lambda nd, j: (0,) * nd, a.ndim))
    once = lambda a: _const_spec(a.shape)
    in_hbm = pl.BlockSpec(memory_space=pl.ANY)
    in_chunk = lambda j: (0, jnp.minimum(j, n_in - 1))
    out_chunk = lambda j: (0, jnp.maximum(j - n_in, 0))
    return pl.pallas_call(
        functools.partial(_dec_mixer_kernel, n_in=n_in),
        grid=(n_in + n_out,),
        in_specs=([once(x), pl.BlockSpec((rows, DEC_CHUNK), out_chunk), once(h0), in_hbm, in_hbm,
                   once(p["norm_mix_g"]), pl.BlockSpec((d, DEC_CHUNK), in_chunk)]
                  + [once(a) for a in small]
                  + [pl.BlockSpec((w_out.shape[0], DEC_CHUNK), out_chunk)]),
        out_specs=[pl.BlockSpec((rows, DEC_CHUNK), out_chunk), full(h0), in_hbm, in_hbm,
                   pl.BlockSpec((d, DEC_CHUNK), in_chunk),
                   pl.BlockSpec((w_out.shape[0], DEC_CHUNK), out_chunk)],
        out_shape=[jax.ShapeDtypeStruct(s.shape, F32) for s in (x, h0, cbuf, pbuf)]
                  + [jax.ShapeDtypeStruct(w_in.shape, BF16), jax.ShapeDtypeStruct(w_out.shape, BF16)],
        scratch_shapes=[
            pltpu.VMEM((rows, d), BF16),
            pltpu.VMEM((n_in, rows, DEC_CHUNK), F32),
            pltpu.VMEM((rows, w_out.shape[0]), BF16),
            pltpu.VMEM(cbuf.shape, F32),
            pltpu.VMEM(pbuf.shape, F32),
            pltpu.VMEM(cbuf.shape, F32),
            pltpu.VMEM(pbuf.shape, F32),
            pltpu.SemaphoreType.DMA((4,)),
        ],
        compiler_params=pltpu.CompilerParams(
            dimension_semantics=("arbitrary",), vmem_limit_bytes=DEC_VMEM_LIMIT),
        name="dec_mixer",
    )(x, x, h0, cbuf, pbuf, p["norm_mix_g"], w_in, *small, w_out)


def _mlp_kernel(x_ref, xs_ref, g_ref, wup_ref, wdn_ref, gfin_ref, o_ref, os_ref, h_ref):
    tm = x_ref.shape[0]
    i = pl.program_id(0)
    j = pl.program_id(1)
    first = j == 0
    last = j == pl.num_programs(1) - 1
    with_decode = i == 0

    def _start(src_ref, rows, acc_ref):
        x = src_ref[...]
        h_ref[rows, :] = _rmsnorm(x, g_ref[...]).astype(BF16)
        acc_ref[...] = x

    def _ffn(rows):
        f = jnp.maximum(_dot(h_ref[rows, :], wup_ref[...]), 0.0)
        return _dot((f * f).astype(BF16), wdn_ref[...])

    pl.when(first)(lambda: _start(x_ref, pl.ds(0, tm), o_ref))
    pl.when(first & with_decode)(lambda: _start(xs_ref, pl.ds(tm, xs_ref.shape[0]), os_ref))

    @pl.when(with_decode)
    def _both_streams():
        part = _ffn(pl.ds(0, h_ref.shape[0]))
        o_ref[...] += part[:tm]
        os_ref[...] += part[tm:]

    @pl.when(jnp.logical_not(with_decode))
    def _prompt_only():
        o_ref[...] += _ffn(pl.ds(0, tm))

    @pl.when(last)
    def _finish():
        o_ref[...] = _rmsnorm(o_ref[...], gfin_ref[...])

    @pl.when(last & with_decode)
    def _finish_decode():
        os_ref[...] = _rmsnorm(os_ref[...], gfin_ref[...])


def _mlp(x, xs, p, tm, tf):
    n, d = x.shape
    ns = xs.shape[0]
    d_ff = p["w_up"].shape[1]
    vec_spec = pl.BlockSpec((1, d), lambda i, j: (0, 0))
    tok_spec = pl.BlockSpec((tm, d), lambda i, j: (i, 0))
    dec_spec = pl.BlockSpec((ns, d), lambda i, j: (0, 0))
    return pl.pallas_call(
        _mlp_kernel,
        grid=(n // tm, d_ff // tf),
        in_specs=[tok_spec, dec_spec, vec_spec,
                  pl.BlockSpec((d, tf), lambda i, j: (0, j)),
                  pl.BlockSpec((tf, d), lambda i, j: (j, 0)),
                  vec_spec],
        out_specs=[tok_spec, dec_spec],
        out_shape=[jax.ShapeDtypeStruct((n, d), F32), jax.ShapeDtypeStruct((ns, d), F32)],
        scratch_shapes=[pltpu.VMEM((tm + ns, d), BF16)],
        compiler_params=pltpu.CompilerParams(
            dimension_semantics=("arbitrary", "arbitrary"),
            vmem_limit_bytes=MLP_VMEM_LIMIT),
        name="mlp",
    )(x, xs, p["norm_mlp_g"], p["w_up"], p["w_down"], p["norm_final_g"])


def _layer_params(l, norm_mix_g, w_in, conv_w, conv_b, w_rg_a, b_rg_a, w_rg_x, b_rg_x, lru_lambda,
                  w_pool, pool_scale, w_out, norm_mlp_g, w_up, w_down, norm_final_g):
    row = lambda v: v.reshape(1, -1)
    return {
        "norm_mix_g": row(norm_mix_g[l]),
        "w_in": w_in[l],
        "conv_w": conv_w[l],
        "conv_b": row(conv_b[l]),
        "w_gate": jnp.concatenate([w_rg_a[l], w_rg_x[l]], axis=-1).astype(BF16),
        "b_rg_a": row(b_rg_a[l]),
        "b_rg_x": row(b_rg_x[l]),
        "lru_lambda": row(lru_lambda[l]),
        "w_pool": w_pool[l].astype(BF16),
        "pool_scale": row(pool_scale[l]),
        "w_out": w_out[l],
        "norm_mlp_g": row(norm_mlp_g[l]),
        "w_up": w_up[l],
        "w_down": w_down[l],
        "norm_final_g": row(norm_final_g),
    }


def kernel(x_prompt, x_sample, state_lru_h, state_conv, state_pool, norm_mix_g, w_in, conv_w, conv_b, w_rg_a, b_rg_a, w_rg_x, b_rg_x, lru_lambda, w_pool, pool_scale, w_out, norm_mlp_g, w_up, w_down, norm_final_g):
    depth = w_in.shape[0]
    assert depth == 1, "final RMSNorm is fused into the (single) layer's MLP kernel"
    bp, tp, d = x_prompt.shape
    bs, ts, _ = x_sample.shape
    assert ts == 1 and tp % MIX_TM == 0 and (bp * tp) % MLP_TM == 0
    p = _layer_params(0, norm_mix_g, w_in, conv_w, conv_b, w_rg_a, b_rg_a, w_rg_x, b_rg_x,
                      lru_lambda, w_pool, pool_scale, w_out, norm_mlp_g, w_up, w_down, norm_final_g)

    hist_major = lambda s: jnp.transpose(s, (1, 0, 2))
    x1s, hs, cs, ps, p["w_in"], p["w_out"] = _dec_mixer(
        x_sample.reshape(bs, d), state_lru_h[0], hist_major(state_conv[0]), hist_major(state_pool[0]),
        p, p["w_in"], p["w_out"])
    cs, ps = hist_major(cs), hist_major(ps)
    x1p, hp, cp, pp, p["w_up"], p["w_down"] = _seq_mixer(x_prompt, p, p["w_up"], p["w_down"], MIX_TM)
    yp, ys = _mlp(x1p.reshape(bp * tp, d), x1s, p, MLP_TM, MLP_TF)
    yp, ys = yp.reshape(bp, tp, d), ys.reshape(bs, ts, d)

    return (yp, ys, hp.reshape(1, bp, -1), cp[None], pp[None], hs[None], cs[None], ps[None])
```

```python
import functools

import jax
import jax.numpy as jnp
from jax import lax
from jax.experimental import pallas as pl
from jax.experimental.pallas import tpu as pltpu

EPS = 1e-6
LRU_HEADS = 8
LRU_C = 8.0
CONV_W = 4
POOL_WINDOWS = (2, 4, 8, 16)
POOL_BUF = max(POOL_WINDOWS) - 1
PAST_LEN = 16384

SUBLANES = 8
MIB = 1024 * 1024

MIX_TM = 256
MLP_TM = 512
MLP_TF = 2048
NORM_CHUNKS = 4
MIX_VMEM_LIMIT = 60 * MIB
DEC_VMEM_LIMIT = 56 * MIB
MLP_VMEM_LIMIT = 60 * MIB

F32 = jnp.float32
BF16 = jnp.bfloat16


def _rmsnorm(x, g):
    return (x * lax.rsqrt(jnp.mean(x * x, axis=-1, keepdims=True) + EPS)) * g


def _dot(a, b):
    return jnp.dot(a, b, preferred_element_type=F32)


def _gate_update(g, xc, ba, bx, cneg):
    hd = xc.shape[1]
    r = jax.nn.sigmoid(g[:, :hd] + ba)
    i = jax.nn.sigmoid(g[:, hd:] + bx)
    a = jnp.exp(r * cneg)
    y = 1.0 - a * a
    mult = jnp.where(y > 0.0, y * lax.rsqrt(y), 0.0)
    return a, xc * i, mult


def _lru_gates(xc, wg_ref, ba, bx, cneg):
    hd = xc.shape[1] // LRU_HEADS
    xcb = xc.astype(BF16)
    a_parts, m_parts, xi_parts = [], [], []
    for h in range(LRU_HEADS):
        sl = slice(h * hd, (h + 1) * hd)
        g = _dot(xcb[:, sl], wg_ref[h])
        a, xi, mult = _gate_update(g, xc[:, sl], ba[:, sl], bx[:, sl], cneg[:, sl])
        a_parts.append(a)
        m_parts.append(mult)
        xi_parts.append(xi)
    cat = lambda ps: jnp.concatenate(ps, axis=1)
    return cat(a_parts), cat(m_parts), cat(xi_parts)


def _pool_project(pooled, wp_ref, scale):
    n_g = len(POOL_WINDOWS)
    gd = pooled.shape[1] // n_g
    pb = pooled.astype(BF16)
    outs = [_dot(pb[:, g * gd:(g + 1) * gd], wp_ref[g]) for g in range(n_g)]
    return jnp.concatenate(outs, axis=1) * scale


def _neg_c_softplus(lam):
    return -LRU_C * jax.nn.softplus(-lam)


CONV_HALO = CONV_W - 1
POOL_HALO = POOL_BUF


def _rows(k, n=1):
    return pl.ds(k * SUBLANES, n * SUBLANES)


def _tile_copies(hbm, buf, sem, b, t, slot, n_grp, to_hbm):
    tm = n_grp * SUBLANES
    copies = []
    for s in range(SUBLANES):
        far = hbm.at[b, pl.ds(t * tm + s * n_grp, n_grp), :]
        near = buf.at[slot, :, s, :]
        src, dst = (near, far) if to_hbm else (far, near)
        copies.append(pltpu.make_async_copy(src, dst, sem.at[slot]))
    return copies


def _fill_halo(ext_ref, carry_ref, halo, n_grp):
    sub = lax.broadcasted_iota(jnp.int32, (SUBLANES, ext_ref.shape[1]), 0)
    for j in range(1, halo + 1):
        cur = ext_ref[_rows(halo + n_grp - j), :]
        prev = carry_ref[_rows(halo - j), :]
        ext_ref[_rows(halo - j), :] = pltpu.roll(jnp.where(sub == SUBLANES - 1, prev, cur), 1, 0)
    carry_ref[...] = ext_ref[_rows(n_grp, halo), :]


def _seq_mixer_kernel(x_hbm, gmix_ref, w_in_ref, cw_ref, cb_ref, wg_ref, ba_ref, bx_ref,
                      lam_ref, wp_ref, ps_ref, w_out_ref, wup32_ref, wdn32_ref,
                      o_hbm, hlast_ref, nconv_ref, npool_ref, wup16_ref, wdn16_ref,
                      xbuf, obuf, in_sem, out_sem, hn_ref, extx_ref, carryx_ref, extp_ref,
                      carryp_ref, xc_ref, xcb_ref, gate_ref, ug_ref, a_ref, b_ref, ymix_ref, hc_ref,
                      *, n_t):
    n_grp = xbuf.shape[1]
    tm = n_grp * SUBLANES
    d = xbuf.shape[3]
    w_lru = a_ref.shape[1]
    w_pool = extp_ref.shape[1]
    hd = w_lru // LRU_HEADS
    half = w_lru // 2
    b = pl.program_id(0)
    t = pl.program_id(1)
    step = b * n_t + t
    n_steps = pl.num_programs(0) * n_t
    slot = step % 2

    @pl.when(step == 0)
    def _first_fetch():
        for c in _tile_copies(x_hbm, xbuf, in_sem, b, t, slot, n_grp, False):
            c.start()

    @pl.when(step + 1 < n_steps)
    def _prefetch():
        nxt = step + 1
        for c in _tile_copies(x_hbm, xbuf, in_sem, nxt // n_t, nxt % n_t, 1 - slot, n_grp, False):
            c.start()

    for c in _tile_copies(x_hbm, xbuf, in_sem, b, t, slot, n_grp, False):
        c.wait()

    @pl.when(t == 0)
    def _reset_state():
        hc_ref[...] = jnp.zeros_like(hc_ref)
        carryx_ref[...] = jnp.zeros_like(carryx_ref)
        carryp_ref[...] = jnp.zeros_like(carryp_ref)

    x_tile = lambda: xbuf[slot].reshape(tm, d)
    hn_ref[...] = _rmsnorm(x_tile(), gmix_ref[...]).astype(BF16)
    in_proj = lambda c0, c1: _dot(hn_ref[...], w_in_ref[:, c0:c1])
    sub = lax.broadcasted_iota(jnp.int32, (SUBLANES, w_lru), 0)
    first_token = (sub == 0) & (t == 0)
    cneg = _neg_c_softplus(lam_ref[...])

    extx_ref[_rows(CONV_HALO, n_grp), :] = in_proj(0, w_lru)
    _fill_halo(extx_ref, carryx_ref, CONV_HALO, n_grp)

    extp_ref[_rows(POOL_HALO, n_grp), 0:w_pool // 2] = in_proj(2 * w_lru, 2 * w_lru + w_pool // 2)
    for gp in range(n_grp // 2):
        g = 2 * gp
        acc = cb_ref[...] + extx_ref[_rows(CONV_HALO + g, 2), :] * cw_ref[CONV_W - 1:CONV_W, :]
        for k in range(CONV_W - 1):
            acc = acc + extx_ref[_rows(g + k, 2), :] * cw_ref[k:k + 1, :]
        xc_ref[_rows(g, 2), :] = acc
        xcb_ref[_rows(g, 2), :] = acc.astype(BF16)
    for hh in range(LRU_HEADS):
        gate_ref[:, 2 * hd * hh:2 * hd * (hh + 1)] = _dot(xcb_ref[:, hd * hh:hd * (hh + 1)], wg_ref[hh])

    extp_ref[_rows(POOL_HALO, n_grp), w_pool // 2:] = in_proj(2 * w_lru + w_pool // 2, 2 * w_lru + w_pool)
    _fill_halo(extp_ref, carryp_ref, POOL_HALO, n_grp)
    ug_ref[:, 0:half] = in_proj(w_lru, w_lru + half)

    def _scan_groups(g0, g1, a_cum, h_loc):
        for g in range(g0, g1):
            gates = gate_ref[_rows(g), :]
            ga = jnp.concatenate([gates[:, 2 * hd * hh:2 * hd * hh + hd] for hh in range(LRU_HEADS)], axis=1)
            gx = jnp.concatenate([gates[:, 2 * hd * hh + hd:2 * hd * (hh + 1)] for hh in range(LRU_HEADS)], axis=1)
            a, xi, mult = _gate_update(jnp.concatenate([ga, gx], axis=1), xc_ref[_rows(g), :],
                                       ba_ref[...], bx_ref[...], cneg)
            if g == 0:
                mult = jnp.where(first_token, 1.0, mult)
            bt = xi * mult
            a_cum = a if a_cum is None else a * a_cum
            h_loc = bt if h_loc is None else a * h_loc + bt
            a_ref[_rows(g), :] = a_cum
            b_ref[_rows(g), :] = h_loc
        return a_cum, h_loc

    a_tot, h_tot = _scan_groups(0, n_grp // 2, None, None)
    ug_ref[:, half:] = in_proj(w_lru + half, 2 * w_lru)
    a_tot, h_tot = _scan_groups(n_grp // 2, n_grp, a_tot, h_tot)

    gd = w_pool // len(POOL_WINDOWS)
    row = lax.broadcasted_iota(jnp.int32, (tm, 1), 0)
    pos = t * tm + (row & (SUBLANES - 1)) * n_grp + (row >> 3)
    pooled = []
    for gi, w in enumerate(POOL_WINDOWS):
        cols = slice(gi * gd, (gi + 1) * gd)
        lo = POOL_HALO - (w - 1)
        s = extp_ref[_rows(lo, n_grp + w - 1), cols]
        width = 1
        while width < w:
            s = s[width * SUBLANES:, :] + s[:-width * SUBLANES, :]
            width *= 2
        inv = 1.0 / jnp.minimum(pos + 1, w).astype(F32)
        pooled.append(s * inv - extp_ref[_rows(POOL_HALO, n_grp), cols])

    y_pool = _pool_project(jnp.concatenate(pooled, axis=1), wp_ref, ps_ref[...])
    ymix_ref[:, w_lru:] = y_pool.astype(BF16)
    obuf[slot] = (x_tile() + _dot(ymix_ref[:, w_lru:], w_out_ref[w_lru:, :])).reshape(n_grp, SUBLANES, d)

    p_cum, q_cum = a_tot, h_tot
    for dd in (1, 2, 4):
        keep = sub >= dd
        q_cum = jnp.where(keep, p_cum * pltpu.roll(q_cum, dd, 0) + q_cum, q_cum)
        p_cum = jnp.where(keep, p_cum * pltpu.roll(p_cum, dd, 0), p_cum)
    h_in = jnp.broadcast_to(hc_ref[...], (SUBLANES, w_lru))
    seg_end = p_cum * h_in + q_cum
    seg_in = jnp.where(sub == 0, h_in, pltpu.roll(seg_end, 1, 0))
    h_end = seg_end[SUBLANES - 1:SUBLANES, :]
    hc_ref[...] = h_end

    seg_in2 = jnp.concatenate([seg_in, seg_in], axis=0)
    for gp in range(n_grp // 2):
        r2 = _rows(2 * gp, 2)
        hs = b_ref[r2, :] + a_ref[r2, :] * seg_in2
        ymix_ref[r2, 0:w_lru] = (hs * jax.nn.gelu(ug_ref[r2, :], approximate=True)).astype(BF16)

    obuf[slot] += _dot(ymix_ref[:, 0:w_lru], w_out_ref[0:w_lru, :]).reshape(n_grp, SUBLANES, d)

    wup16_ref[...] = wup32_ref[...].astype(BF16)
    wdn16_ref[...] = wdn32_ref[...].astype(BF16)

    for c in _tile_copies(o_hbm, obuf, out_sem, b, t, slot, n_grp, True):
        c.start()

    @pl.when(step > 0)
    def _wait_prev_writeback():
        prv = step - 1
        for c in _tile_copies(o_hbm, obuf, out_sem, prv // n_t, prv % n_t, 1 - slot, n_grp, True):
            c.wait()

    @pl.when(step == n_steps - 1)
    def _wait_last_writeback():
        for c in _tile_copies(o_hbm, obuf, out_sem, b, t, slot, n_grp, True):
            c.wait()

    @pl.when(t == n_t - 1)
    def _emit_state():
        last = SUBLANES - 1
        hlast_ref[...] = h_end
        for i in range(CONV_HALO):
            nconv_ref[i:i + 1, :] = carryx_ref[pl.ds(i * SUBLANES + last, 1), :]
        for i in range(POOL_HALO):
            npool_ref[i:i + 1, :] = carryp_ref[pl.ds(i * SUBLANES + last, 1), :]


def _const_spec(shape):
    nd = len(shape)
    return pl.BlockSpec(shape, lambda *_: (0,) * nd, pipeline_mode=pl.Buffered(1))


def _seq_mixer(x, p, w_up, w_down, tm):
    bsz, seq, d = x.shape
    w_lru = p["conv_b"].shape[1]
    w_pool = p["pool_scale"].shape[1]
    weights = (p["norm_mix_g"], p["w_in"], p["conv_w"], p["conv_b"], p["w_gate"], p["b_rg_a"],
               p["b_rg_x"], p["lru_lambda"], p["w_pool"], p["pool_scale"], p["w_out"])
    n_t = seq // tm
    n_steps = bsz * n_t
    n_grp = tm // SUBLANES
    assert n_grp % 2 == 0 and n_grp >= POOL_HALO
    any_spec = pl.BlockSpec(memory_space=pl.ANY)
    state_spec = lambda rows, w: pl.BlockSpec((None, rows, w), lambda b, t: (b, 0, 0))
    slab_spec = lambda w: pl.BlockSpec((w.shape[0] // n_steps, w.shape[1]),
                                       lambda b, t: (b * n_t + t, 0))
    ext_rows = lambda halo: (halo + n_grp) * SUBLANES
    return pl.pallas_call(
        functools.partial(_seq_mixer_kernel, n_t=n_t),
        grid=(bsz, n_t),
        in_specs=([any_spec] + [_const_spec(w.shape) for w in weights]
                  + [slab_spec(w_up), slab_spec(w_down)]),
        out_specs=[any_spec, state_spec(1, w_lru), state_spec(CONV_W - 1, w_lru),
                   state_spec(POOL_BUF, w_pool), slab_spec(w_up), slab_spec(w_down)],
        out_shape=[jax.ShapeDtypeStruct((bsz, seq, d), F32),
                   jax.ShapeDtypeStruct((bsz, 1, w_lru), F32),
                   jax.ShapeDtypeStruct((bsz, CONV_W - 1, w_lru), F32),
                   jax.ShapeDtypeStruct((bsz, POOL_BUF, w_pool), F32),
                   jax.ShapeDtypeStruct(w_up.shape, BF16),
                   jax.ShapeDtypeStruct(w_down.shape, BF16)],
        scratch_shapes=[
            pltpu.VMEM((2, n_grp, SUBLANES, d), F32),
            pltpu.VMEM((2, n_grp, SUBLANES, d), F32),
            pltpu.SemaphoreType.DMA((2,)),
            pltpu.SemaphoreType.DMA((2,)),
            pltpu.VMEM((tm, d), BF16),
            pltpu.VMEM((ext_rows(CONV_HALO), w_lru), F32),
            pltpu.VMEM((CONV_HALO * SUBLANES, w_lru), F32),
            pltpu.VMEM((ext_rows(POOL_HALO), w_pool), F32),
            pltpu.VMEM((POOL_HALO * SUBLANES, w_pool), F32),
            pltpu.VMEM((tm, w_lru), F32),
            pltpu.VMEM((tm, w_lru), BF16),
            pltpu.VMEM((tm, 2 * w_lru), F32),
            pltpu.VMEM((tm, w_lru), F32),
            pltpu.VMEM((tm, w_lru), F32),
            pltpu.VMEM((tm, w_lru), F32),
            pltpu.VMEM((tm, d), BF16),
            pltpu.VMEM((1, w_lru), F32),
        ],
        compiler_params=pltpu.CompilerParams(
            dimension_semantics=("arbitrary", "arbitrary"),
            vmem_limit_bytes=MIX_VMEM_LIMIT),
        name="seq_mixer",
    )(x, *weights, w_up, w_down)


DEC_CHUNK = 512


def _dec_mixer_kernel(x_ref, xcol_ref, h0_ref, cbuf_hbm, pbuf_hbm, gmix_ref, w_in_ref, cw_ref, cb_ref,
                      wg_ref, ba_ref, bx_ref, lam_ref, wp_ref, ps_ref, w_out_ref,
                      o_ref, hnew_ref, nconv_hbm, npool_hbm, w_in16_ref, w_out16_ref,
                      hn_ref, proj_ref, ymix_ref, cbuf_ref, pbuf_ref, nconv_ref, npool_ref, state_sem,
                      *, n_in):
    j = pl.program_id(0)
    w_lru = h0_ref.shape[1]
    w_pool = ps_ref.shape[1]
    state_in = [pltpu.make_async_copy(cbuf_hbm, cbuf_ref, state_sem.at[0]),
                pltpu.make_async_copy(pbuf_hbm, pbuf_ref, state_sem.at[1])]
    state_out = [pltpu.make_async_copy(nconv_ref, nconv_hbm, state_sem.at[2]),
                 pltpu.make_async_copy(npool_ref, npool_hbm, state_sem.at[3])]

    @pl.when(j == 0)
    def _norm():
        for c in state_in:
            c.start()
        hn_ref[...] = _rmsnorm(x_ref[...], gmix_ref[...]).astype(BF16)

    @pl.when(j < n_in)
    def _in_projection():
        w = w_in_ref[...].astype(BF16)
        w_in16_ref[...] = w
        proj_ref[j] = _dot(hn_ref[...], w)

    @pl.when(j == n_in)
    def _mix():
        per = w_lru // DEC_CHUNK
        branch = lambda k: jnp.concatenate([proj_ref[k * per + c] for c in range(per)], axis=1)
        ux, ug, up = branch(0), branch(1), branch(2)
        for c in state_in:
            c.wait()

        xc = cb_ref[...] + ux * cw_ref[CONV_W - 1:CONV_W, :]
        for k in range(CONV_W - 1):
            tap = cbuf_ref[k]
            xc = xc + tap * cw_ref[k:k + 1, :]
            if k > 0:
                nconv_ref[k - 1] = tap
        nconv_ref[CONV_W - 2] = ux

        a, mult, xi = _lru_gates(xc, wg_ref, ba_ref[...], bx_ref[...], _neg_c_softplus(lam_ref[...]))
        h_new = a * h0_ref[...] + xi * mult
        hnew_ref[...] = h_new
        ymix_ref[:, 0:w_lru] = (h_new * jax.nn.gelu(ug, approximate=True)).astype(BF16)

        gd = w_pool // len(POOL_WINDOWS)
        s = up
        width = 1
        pooled = []
        for g, w in enumerate(POOL_WINDOWS):
            while width < w:
                k = POOL_BUF - width
                hist = pbuf_ref[k]
                if k > 0:
                    npool_ref[k - 1] = hist
                s = s + hist[:, g * gd:]
                width += 1
            count = float(min(PAST_LEN + 1, w))
            pooled.append(s[:, 0:gd] / count - up[:, g * gd:(g + 1) * gd])
            if g + 1 < len(POOL_WINDOWS):
                s = s[:, gd:]
        npool_ref[POOL_BUF - 1] = up
        y_pool = _pool_project(jnp.concatenate(pooled, axis=1), wp_ref, ps_ref[...])
        ymix_ref[:, w_lru:] = y_pool.astype(BF16)
        for c in state_out:
            c.start()

    @pl.when(j >= n_in)
    def _out_projection():
        w = w_out_ref[...].astype(BF16)
        w_out16_ref[...] = w
        o_ref[...] = xcol_ref[...] + _dot(ymix_ref[...], w)

    @pl.when(j == pl.num_programs(0) - 1)
    def _drain_state():
        for c in state_out:
            c.wait()


def _dec_mixer(x, h0, cbuf, pbuf, p, w_in, w_out):
    rows, d = x.shape
    n_in = w_in.shape[1] // DEC_CHUNK
    n_out = w_out.shape[1] // DEC_CHUNK
    small = (p["conv_w"], p["conv_b"], p["w_gate"], p["b_rg_a"], p["b_rg_x"], p["lru_lambda"],
             p["w_pool"], p["pool_scale"])
    full = lambda a: pl.BlockSpec(a.shape, functools.partial(lambda nd, j: (0,) * nd, a.ndim))
    once = lambda a: _const_spec(a.shape)
    in_hbm = pl.BlockSpec(memory_space=pl.ANY)
    in_chunk = lambda j: (0, jnp.minimum(j, n_in - 1))
    out_chunk = lambda j: (0, jnp.maximum(j - n_in, 0))
    return pl.pallas_call(
        functools.partial(_dec_mixer_kernel, n_in=n_in),
        grid=(n_in + n_out,),
        in_specs=([once(x), pl.BlockSpec((rows, DEC_CHUNK), out_chunk), once(h0), in_hbm, in_hbm,
                   once(p["norm_mix_g"]), pl.BlockSpec((d, DEC_CHUNK), in_chunk)]
                  + [once(a) for a in small]
                  + [pl.BlockSpec((w_out.shape[0], DEC_CHUNK), out_chunk)]),
        out_specs=[pl.BlockSpec((rows, DEC_CHUNK), out_chunk), full(h0), in_hbm, in_hbm,
                   pl.BlockSpec((d, DEC_CHUNK), in_chunk),
                   pl.BlockSpec((w_out.shape[0], DEC_CHUNK), out_chunk)],
        out_shape=[jax.ShapeDtypeStruct(s.shape, F32) for s in (x, h0, cbuf, pbuf)]
                  + [jax.ShapeDtypeStruct(w_in.shape, BF16), jax.ShapeDtypeStruct(w_out.shape, BF16)],
        scratch_shapes=[
            pltpu.VMEM((rows, d), BF16),
            pltpu.VMEM((n_in, rows, DEC_CHUNK), F32),
            pltpu.VMEM((rows, w_out.shape[0]), BF16),
            pltpu.VMEM(cbuf.shape, F32),
            pltpu.VMEM(pbuf.shape, F32),
            pltpu.VMEM(cbuf.shape, F32),
            pltpu.VMEM(pbuf.shape, F32),
            pltpu.SemaphoreType.DMA((4,)),
        ],
        compiler_params=pltpu.CompilerParams(
            dimension_semantics=("arbitrary",), vmem_limit_bytes=DEC_VMEM_LIMIT),
        name="dec_mixer",
    )(x, x, h0, cbuf, pbuf, p["norm_mix_g"], w_in, *small, w_out)


def _mlp_kernel(x_ref, xs_ref, g_ref, wup_ref, wdn_ref, gfin_ref, o_ref, os_ref, h_ref):
    tm = x_ref.shape[0]
    i = pl.program_id(0)
    j = pl.program_id(1)
    first = j == 0
    last = j == pl.num_programs(1) - 1
    with_decode = i == 0

    def _start(src_ref, rows, acc_ref):
        x = src_ref[...]
        h_ref[rows, :] = _rmsnorm(x, g_ref[...]).astype(BF16)
        acc_ref[...] = x

    def _ffn(rows):
        f = jnp.maximum(_dot(h_ref[rows, :], wup_ref[...]), 0.0)
        return _dot((f * f).astype(BF16), wdn_ref[...])

    pl.when(first & with_decode)(lambda: _start(x_ref, pl.ds(0, tm), o_ref))
    pl.when(first & with_decode)(lambda: _start(xs_ref, pl.ds(tm, xs_ref.shape[0]), os_ref))

    @pl.when(with_decode)
    def _both_streams():
        part = _ffn(pl.ds(0, h_ref.shape[0]))
        o_ref[...] += part[:tm]
        os_ref[...] += part[tm:]

    prompt_only = jnp.logical_not(with_decode)

    @pl.when(prompt_only & first)
    def _first_tile():
        x = x_ref[...]
        o_ref[...] = x
        scale = lax.rsqrt(jnp.mean(x * x, axis=-1, keepdims=True) + EPS)
        d = x.shape[1]
        kc = d // NORM_CHUNKS
        f = None
        for c in range(NORM_CHUNKS):
            cols = slice(c * kc, (c + 1) * kc)
            hc = ((x_ref[:, cols] * scale) * g_ref[:, cols]).astype(BF16)
            h_ref[pl.ds(0, tm), cols] = hc
            part = _dot(hc, wup_ref[cols, :])
            f = part if f is None else f + part
        f = jnp.maximum(f, 0.0)
        o_ref[...] += _dot((f * f).astype(BF16), wdn_ref[...])

    @pl.when(prompt_only & jnp.logical_not(first))
    def _prompt_only():
        o_ref[...] += _ffn(pl.ds(0, tm))

    @pl.when(last)
    def _finish():
        o_ref[...] = _rmsnorm(o_ref[...], gfin_ref[...])

    @pl.when(last & with_decode)
    def _finish_decode():
        os_ref[...] = _rmsnorm(os_ref[...], gfin_ref[...])


def _mlp(x, xs, p, tm, tf):
    n, d = x.shape
    ns = xs.shape[0]
    d_ff = p["w_up"].shape[1]
    vec_spec = pl.BlockSpec((1, d), lambda i, j: (0, 0))
    tok_spec = pl.BlockSpec((tm, d), lambda i, j: (i, 0))
    dec_spec = pl.BlockSpec((ns, d), lambda i, j: (0, 0))
    return pl.pallas_call(
        _mlp_kernel,
        grid=(n // tm, d_ff // tf),
        in_specs=[tok_spec, dec_spec, vec_spec,
                  pl.BlockSpec((d, tf), lambda i, j: (0, j)),
                  pl.BlockSpec((tf, d), lambda i, j: (j, 0)),
                  vec_spec],
        out_specs=[tok_spec, dec_spec],
        out_shape=[jax.ShapeDtypeStruct((n, d), F32), jax.ShapeDtypeStruct((ns, d), F32)],
        scratch_shapes=[pltpu.VMEM((tm + ns, d), BF16)],
        compiler_params=pltpu.CompilerParams(
            dimension_semantics=("arbitrary", "arbitrary"),
            vmem_limit_bytes=MLP_VMEM_LIMIT),
        name="mlp",
    )(x, xs, p["norm_mlp_g"], p["w_up"], p["w_down"], p["norm_final_g"])


def _layer_params(l, norm_mix_g, w_in, conv_w, conv_b, w_rg_a, b_rg_a, w_rg_x, b_rg_x, lru_lambda,
                  w_pool, pool_scale, w_out, norm_mlp_g, w_up, w_down, norm_final_g):
    row = lambda v: v.reshape(1, -1)
    return {
        "norm_mix_g": row(norm_mix_g[l]),
        "w_in": w_in[l],
        "conv_w": conv_w[l],
        "conv_b": row(conv_b[l]),
        "w_gate": jnp.concatenate([w_rg_a[l], w_rg_x[l]], axis=-1).astype(BF16),
        "b_rg_a": row(b_rg_a[l]),
        "b_rg_x": row(b_rg_x[l]),
        "lru_lambda": row(lru_lambda[l]),
        "w_pool": w_pool[l].astype(BF16),
        "pool_scale": row(pool_scale[l]),
        "w_out": w_out[l],
        "norm_mlp_g": row(norm_mlp_g[l]),
        "w_up": w_up[l],
        "w_down": w_down[l],
        "norm_final_g": row(norm_final_g),
    }


def kernel(x_prompt, x_sample, state_lru_h, state_conv, state_pool, norm_mix_g, w_in, conv_w, conv_b, w_rg_a, b_rg_a, w_rg_x, b_rg_x, lru_lambda, w_pool, pool_scale, w_out, norm_mlp_g, w_up, w_down, norm_final_g):
    depth = w_in.shape[0]
    assert depth == 1, "final RMSNorm is fused into the (single) layer's MLP kernel"
    bp, tp, d = x_prompt.shape
    bs, ts, _ = x_sample.shape
    assert ts == 1 and tp % MIX_TM == 0 and (bp * tp) % MLP_TM == 0
    p = _layer_params(0, norm_mix_g, w_in, conv_w, conv_b, w_rg_a, b_rg_a, w_rg_x, b_rg_x,
                      lru_lambda, w_pool, pool_scale, w_out, norm_mlp_g, w_up, w_down, norm_final_g)

    hist_major = lambda s: jnp.transpose(s, (1, 0, 2))
    x1s, hs, cs, ps, p["w_in"], p["w_out"] = _dec_mixer(
        x_sample.reshape(bs, d), state_lru_h[0], hist_major(state_conv[0]), hist_major(state_pool[0]),
        p, p["w_in"], p["w_out"])
    cs, ps = hist_major(cs), hist_major(ps)
    x1p, hp, cp, pp, p["w_up"], p["w_down"] = _seq_mixer(x_prompt, p, p["w_up"], p["w_down"], MIX_TM)
    yp, ys = _mlp(x1p.reshape(bp * tp, d), x1s, p, MLP_TM, MLP_TF)
    yp, ys = yp.reshape(bp, tp, d), ys.reshape(bs, ts, d)

    return (yp, ys, hp.reshape(1, bp, -1), cp[None], pp[None], hs[None], cs[None], ps[None])
```

```python
import functools

import jax
import jax.numpy as jnp
from jax import lax
from jax.experimental import pallas as pl
from jax.experimental.pallas import tpu as pltpu

EPS = 1e-6
LRU_HEADS = 8
LRU_C = 8.0
CONV_W = 4
POOL_WINDOWS = (2, 4, 8, 16)
POOL_BUF = max(POOL_WINDOWS) - 1
PAST_LEN = 16384

SUBLANES = 8
MIB = 1024 * 1024

MIX_TM = 256
MLP_TM = 512
MLP_TF = 2048
NORM_CHUNKS = 4
MIX_VMEM_LIMIT = 60 * MIB
DEC_VMEM_LIMIT = 60 * MIB
MLP_VMEM_LIMIT = 60 * MIB

F32 = jnp.float32
BF16 = jnp.bfloat16


def _rmsnorm(x, g):
    return (x * lax.rsqrt(jnp.mean(x * x, axis=-1, keepdims=True) + EPS)) * g


def _dot(a, b):
    return jnp.dot(a, b, preferred_element_type=F32)


def _gate_update(g, xc, ba, bx, cneg):
    hd = xc.shape[1]
    r = jax.nn.sigmoid(g[:, :hd] + ba)
    i = jax.nn.sigmoid(g[:, hd:] + bx)
    a = jnp.exp(r * cneg)
    y = 1.0 - a * a
    mult = jnp.where(y > 0.0, y * lax.rsqrt(y), 0.0)
    return a, xc * i, mult


def _lru_gates(xc, wg_ref, ba, bx, cneg):
    hd = xc.shape[1] // LRU_HEADS
    xcb = xc.astype(BF16)
    a_parts, m_parts, xi_parts = [], [], []
    for h in range(LRU_HEADS):
        sl = slice(h * hd, (h + 1) * hd)
        g = _dot(xcb[:, sl], wg_ref[h])
        a, xi, mult = _gate_update(g, xc[:, sl], ba[:, sl], bx[:, sl], cneg[:, sl])
        a_parts.append(a)
        m_parts.append(mult)
        xi_parts.append(xi)
    cat = lambda ps: jnp.concatenate(ps, axis=1)
    return cat(a_parts), cat(m_parts), cat(xi_parts)


def _pool_project(pooled, wp_ref, scale):
    n_g = len(POOL_WINDOWS)
    gd = pooled.shape[1] // n_g
    pb = pooled.astype(BF16)
    outs = [_dot(pb[:, g * gd:(g + 1) * gd], wp_ref[g]) for g in range(n_g)]
    return jnp.concatenate(outs, axis=1) * scale


def _neg_c_softplus(lam):
    return -LRU_C * jax.nn.softplus(-lam)


CONV_HALO = CONV_W - 1
POOL_HALO = POOL_BUF


def _rows(k, n=1):
    return pl.ds(k * SUBLANES, n * SUBLANES)


def _tile_copies(hbm, buf, sem, b, t, slot, n_grp, to_hbm):
    tm = n_grp * SUBLANES
    copies = []
    for s in range(SUBLANES):
        far = hbm.at[b, pl.ds(t * tm + s * n_grp, n_grp), :]
        near = buf.at[slot, :, s, :]
        src, dst = (near, far) if to_hbm else (far, near)
        copies.append(pltpu.make_async_copy(src, dst, sem.at[slot]))
    return copies


def _fill_halo(ext_ref, carry_ref, halo, n_grp):
    sub = lax.broadcasted_iota(jnp.int32, (SUBLANES, ext_ref.shape[1]), 0)
    for j in range(1, halo + 1):
        cur = ext_ref[_rows(halo + n_grp - j), :]
        prev = carry_ref[_rows(halo - j), :]
        ext_ref[_rows(halo - j), :] = pltpu.roll(jnp.where(sub == SUBLANES - 1, prev, cur), 1, 0)
    carry_ref[...] = ext_ref[_rows(n_grp, halo), :]


def _seq_mixer_kernel(x_hbm, gmix_ref, w_in_ref, cw_ref, cb_ref, wg_ref, ba_ref, bx_ref,
                      lam_ref, wp_ref, ps_ref, w_out_ref, wup32_ref, wdn32_ref,
                      o_hbm, hlast_ref, nconv_ref, npool_ref, wup16_ref, wdn16_ref,
                      xbuf, obuf, in_sem, out_sem, hn_ref, extx_ref, carryx_ref, extp_ref,
                      carryp_ref, xc_ref, xcb_ref, gate_ref, ug_ref, a_ref, b_ref, ymix_ref, hc_ref,
                      *, n_t):
    n_grp = xbuf.shape[1]
    tm = n_grp * SUBLANES
    d = xbuf.shape[3]
    w_lru = a_ref.shape[1]
    w_pool = extp_ref.shape[1]
    hd = w_lru // LRU_HEADS
    half = w_lru // 2
    b = pl.program_id(0)
    t = pl.program_id(1)
    step = b * n_t + t
    n_steps = pl.num_programs(0) * n_t
    slot = step % 2

    @pl.when(step == 0)
    def _first_fetch():
        for c in _tile_copies(x_hbm, xbuf, in_sem, b, t, slot, n_grp, False):
            c.start()

    @pl.when(step + 1 < n_steps)
    def _prefetch():
        nxt = step + 1
        for c in _tile_copies(x_hbm, xbuf, in_sem, nxt // n_t, nxt % n_t, 1 - slot, n_grp, False):
            c.start()

    for c in _tile_copies(x_hbm, xbuf, in_sem, b, t, slot, n_grp, False):
        c.wait()

    @pl.when(t == 0)
    def _reset_state():
        hc_ref[...] = jnp.zeros_like(hc_ref)
        carryx_ref[...] = jnp.zeros_like(carryx_ref)
        carryp_ref[...] = jnp.zeros_like(carryp_ref)

    x_tile = lambda: xbuf[slot].reshape(tm, d)
    hn_ref[...] = _rmsnorm(x_tile(), gmix_ref[...]).astype(BF16)
    in_proj = lambda c0, c1: _dot(hn_ref[...], w_in_ref[:, c0:c1])
    sub = lax.broadcasted_iota(jnp.int32, (SUBLANES, w_lru), 0)
    first_token = (sub == 0) & (t == 0)
    cneg = _neg_c_softplus(lam_ref[...])

    extx_ref[_rows(CONV_HALO, n_grp), :] = in_proj(0, w_lru)
    _fill_halo(extx_ref, carryx_ref, CONV_HALO, n_grp)

    extp_ref[_rows(POOL_HALO, n_grp), 0:w_pool // 2] = in_proj(2 * w_lru, 2 * w_lru + w_pool // 2)
    for gp in range(n_grp // 2):
        g = 2 * gp
        acc = cb_ref[...] + extx_ref[_rows(CONV_HALO + g, 2), :] * cw_ref[CONV_W - 1:CONV_W, :]
        for k in range(CONV_W - 1):
            acc = acc + extx_ref[_rows(g + k, 2), :] * cw_ref[k:k + 1, :]
        xc_ref[_rows(g, 2), :] = acc
        xcb_ref[_rows(g, 2), :] = acc.astype(BF16)
    for hh in range(LRU_HEADS):
        gate_ref[:, 2 * hd * hh:2 * hd * (hh + 1)] = _dot(xcb_ref[:, hd * hh:hd * (hh + 1)], wg_ref[hh])

    extp_ref[_rows(POOL_HALO, n_grp), w_pool // 2:] = in_proj(2 * w_lru + w_pool // 2, 2 * w_lru + w_pool)
    _fill_halo(extp_ref, carryp_ref, POOL_HALO, n_grp)
    ug_ref[:, 0:half] = in_proj(w_lru, w_lru + half)

    def _scan_groups(g0, g1, a_cum, h_loc):
        for g in range(g0, g1):
            gates = gate_ref[_rows(g), :]
            ga = jnp.concatenate([gates[:, 2 * hd * hh:2 * hd * hh + hd] for hh in range(LRU_HEADS)], axis=1)
            gx = jnp.concatenate([gates[:, 2 * hd * hh + hd:2 * hd * (hh + 1)] for hh in range(LRU_HEADS)], axis=1)
            a, xi, mult = _gate_update(jnp.concatenate([ga, gx], axis=1), xc_ref[_rows(g), :],
                                       ba_ref[...], bx_ref[...], cneg)
            if g == 0:
                mult = jnp.where(first_token, 1.0, mult)
            bt = xi * mult
            a_cum = a if a_cum is None else a * a_cum
            h_loc = bt if h_loc is None else a * h_loc + bt
            a_ref[_rows(g), :] = a_cum
            b_ref[_rows(g), :] = h_loc
        return a_cum, h_loc

    a_tot, h_tot = _scan_groups(0, n_grp // 2, None, None)
    ug_ref[:, half:] = in_proj(w_lru + half, 2 * w_lru)
    a_tot, h_tot = _scan_groups(n_grp // 2, n_grp, a_tot, h_tot)

    gd = w_pool // len(POOL_WINDOWS)
    row = lax.broadcasted_iota(jnp.int32, (tm, 1), 0)
    pos = t * tm + (row & (SUBLANES - 1)) * n_grp + (row >> 3)
    pooled = []
    for gi, w in enumerate(POOL_WINDOWS):
        cols = slice(gi * gd, (gi + 1) * gd)
        lo = POOL_HALO - (w - 1)
        s = extp_ref[_rows(lo, n_grp + w - 1), cols]
        width = 1
        while width < w:
            s = s[width * SUBLANES:, :] + s[:-width * SUBLANES, :]
            width *= 2
        inv = 1.0 / jnp.minimum(pos + 1, w).astype(F32)
        pooled.append(s * inv - extp_ref[_rows(POOL_HALO, n_grp), cols])

    y_pool = _pool_project(jnp.concatenate(pooled, axis=1), wp_ref, ps_ref[...])
    ymix_ref[:, w_lru:] = y_pool.astype(BF16)
    obuf[slot] = (x_tile() + _dot(ymix_ref[:, w_lru:], w_out_ref[w_lru:, :])).reshape(n_grp, SUBLANES, d)

    p_cum, q_cum = a_tot, h_tot
    for dd in (1, 2, 4):
        keep = sub >= dd
        q_cum = jnp.where(keep, p_cum * pltpu.roll(q_cum, dd, 0) + q_cum, q_cum)
        p_cum = jnp.where(keep, p_cum * pltpu.roll(p_cum, dd, 0), p_cum)
    h_in = jnp.broadcast_to(hc_ref[...], (SUBLANES, w_lru))
    seg_end = p_cum * h_in + q_cum
    seg_in = jnp.where(sub == 0, h_in, pltpu.roll(seg_end, 1, 0))
    h_end = seg_end[SUBLANES - 1:SUBLANES, :]
    hc_ref[...] = h_end

    seg_in2 = jnp.concatenate([seg_in, seg_in], axis=0)
    for gp in range(n_grp // 2):
        r2 = _rows(2 * gp, 2)
        hs = b_ref[r2, :] + a_ref[r2, :] * seg_in2
        ymix_ref[r2, 0:w_lru] = (hs * jax.nn.gelu(ug_ref[r2, :], approximate=True)).astype(BF16)

    obuf[slot] += _dot(ymix_ref[:, 0:w_lru], w_out_ref[0:w_lru, :]).reshape(n_grp, SUBLANES, d)

    wup16_ref[...] = wup32_ref[...].astype(BF16)
    wdn16_ref[...] = wdn32_ref[...].astype(BF16)

    for c in _tile_copies(o_hbm, obuf, out_sem, b, t, slot, n_grp, True):
        c.start()

    @pl.when(step > 0)
    def _wait_prev_writeback():
        prv = step - 1
        for c in _tile_copies(o_hbm, obuf, out_sem, prv // n_t, prv % n_t, 1 - slot, n_grp, True):
            c.wait()

    @pl.when(step == n_steps - 1)
    def _wait_last_writeback():
        for c in _tile_copies(o_hbm, obuf, out_sem, b, t, slot, n_grp, True):
            c.wait()

    @pl.when(t == n_t - 1)
    def _emit_state():
        last = SUBLANES - 1
        hlast_ref[...] = h_end
        for i in range(CONV_HALO):
            nconv_ref[i:i + 1, :] = carryx_ref[pl.ds(i * SUBLANES + last, 1), :]
        for i in range(POOL_HALO):
            npool_ref[i:i + 1, :] = carryp_ref[pl.ds(i * SUBLANES + last, 1), :]


def _const_spec(shape):
    nd = len(shape)
    return pl.BlockSpec(shape, lambda *_: (0,) * nd, pipeline_mode=pl.Buffered(1))


def _seq_mixer(x, p, w_up, w_down, tm):
    bsz, seq, d = x.shape
    w_lru = p["conv_b"].shape[1]
    w_pool = p["pool_scale"].shape[1]
    weights = (p["norm_mix_g"], p["w_in"], p["conv_w"], p["conv_b"], p["w_gate"], p["b_rg_a"],
               p["b_rg_x"], p["lru_lambda"], p["w_pool"], p["pool_scale"], p["w_out"])
    n_t = seq // tm
    n_steps = bsz * n_t
    n_grp = tm // SUBLANES
    assert n_grp % 2 == 0 and n_grp >= POOL_HALO
    any_spec = pl.BlockSpec(memory_space=pl.ANY)
    state_spec = lambda rows, w: pl.BlockSpec((None, rows, w), lambda b, t: (b, 0, 0))
    slab_spec = lambda w: pl.BlockSpec((w.shape[0] // n_steps, w.shape[1]),
                                       lambda b, t: (b * n_t + t, 0))
    ext_rows = lambda halo: (halo + n_grp) * SUBLANES
    return pl.pallas_call(
        functools.partial(_seq_mixer_kernel, n_t=n_t),
        grid=(bsz, n_t),
        in_specs=([any_spec] + [_const_spec(w.shape) for w in weights]
                  + [slab_spec(w_up), slab_spec(w_down)]),
        out_specs=[any_spec, state_spec(1, w_lru), state_spec(CONV_W - 1, w_lru),
                   state_spec(POOL_BUF, w_pool), slab_spec(w_up), slab_spec(w_down)],
        out_shape=[jax.ShapeDtypeStruct((bsz, seq, d), F32),
                   jax.ShapeDtypeStruct((bsz, 1, w_lru), F32),
                   jax.ShapeDtypeStruct((bsz, CONV_W - 1, w_lru), F32),
                   jax.ShapeDtypeStruct((bsz, POOL_BUF, w_pool), F32),
                   jax.ShapeDtypeStruct(w_up.shape, BF16),
                   jax.ShapeDtypeStruct(w_down.shape, BF16)],
        scratch_shapes=[
            pltpu.VMEM((2, n_grp, SUBLANES, d), F32),
            pltpu.VMEM((2, n_grp, SUBLANES, d), F32),
            pltpu.SemaphoreType.DMA((2,)),
            pltpu.SemaphoreType.DMA((2,)),
            pltpu.VMEM((tm, d), BF16),
            pltpu.VMEM((ext_rows(CONV_HALO), w_lru), F32),
            pltpu.VMEM((CONV_HALO * SUBLANES, w_lru), F32),
            pltpu.VMEM((ext_rows(POOL_HALO), w_pool), F32),
            pltpu.VMEM((POOL_HALO * SUBLANES, w_pool), F32),
            pltpu.VMEM((tm, w_lru), F32),
            pltpu.VMEM((tm, w_lru), BF16),
            pltpu.VMEM((tm, 2 * w_lru), F32),
            pltpu.VMEM((tm, w_lru), F32),
            pltpu.VMEM((tm, w_lru), F32),
            pltpu.VMEM((tm, w_lru), F32),
            pltpu.VMEM((tm, d), BF16),
            pltpu.VMEM((1, w_lru), F32),
        ],
        compiler_params=pltpu.CompilerParams(
            dimension_semantics=("arbitrary", "arbitrary"),
            vmem_limit_bytes=MIX_VMEM_LIMIT),
        name="seq_mixer",
    )(x, *weights, w_up, w_down)


DEC_CHUNK = 512


def _dec_mixer_kernel(x_ref, h0_ref, cbuf_hbm, pbuf_hbm, gmix_ref, w_in_ref, cw_ref, cb_ref,
                      wg_ref, ba_ref, bx_ref, lam_ref, wp_ref, ps_ref, w_out_ref,
                      o_ref, hnew_ref, nconv_hbm, npool_hbm, w_in16_ref, w_out16_ref,
                      hn_ref, proj_ref, wo16_ref, cbuf_ref, pbuf_ref, nconv_ref, npool_ref, state_sem,
                      *, n_in):
    n_out = wo16_ref.shape[0]
    j = pl.program_id(0)
    w_lru = h0_ref.shape[1]
    w_pool = ps_ref.shape[1]
    state_in = [pltpu.make_async_copy(cbuf_hbm, cbuf_ref, state_sem.at[0]),
                pltpu.make_async_copy(pbuf_hbm, pbuf_ref, state_sem.at[1])]
    state_out = [pltpu.make_async_copy(nconv_ref, nconv_hbm, state_sem.at[2]),
                 pltpu.make_async_copy(npool_ref, npool_hbm, state_sem.at[3])]

    @pl.when(j == 0)
    def _norm():
        for c in state_in:
            c.start()
        hn_ref[...] = _rmsnorm(x_ref[...], gmix_ref[...]).astype(BF16)

    @pl.when(j < n_in)
    def _in_projection():
        w = w_in_ref[...].astype(BF16)
        w_in16_ref[...] = w
        proj_ref[j] = _dot(hn_ref[...], w)

    @pl.when(j < n_out)
    def _round_out_weights():
        w = w_out_ref[...].astype(BF16)
        w_out16_ref[...] = w
        wo16_ref[j] = w

    @pl.when(j == n_in)
    def _mix():
        per = w_lru // DEC_CHUNK
        branch = lambda k: jnp.concatenate([proj_ref[k * per + c] for c in range(per)], axis=1)
        ux, ug, up = branch(0), branch(1), branch(2)
        for c in state_in:
            c.wait()

        xc = cb_ref[...] + ux * cw_ref[CONV_W - 1:CONV_W, :]
        for k in range(CONV_W - 1):
            tap = cbuf_ref[k]
            xc = xc + tap * cw_ref[k:k + 1, :]
            if k > 0:
                nconv_ref[k - 1] = tap
        nconv_ref[CONV_W - 2] = ux

        a, mult, xi = _lru_gates(xc, wg_ref, ba_ref[...], bx_ref[...], _neg_c_softplus(lam_ref[...]))
        h_new = a * h0_ref[...] + xi * mult
        hnew_ref[...] = h_new
        y_lru = (h_new * jax.nn.gelu(ug, approximate=True)).astype(BF16)

        gd = w_pool // len(POOL_WINDOWS)
        s = up
        width = 1
        pooled = []
        for g, w in enumerate(POOL_WINDOWS):
            while width < w:
                k = POOL_BUF - width
                hist = pbuf_ref[k]
                if k > 0:
                    npool_ref[k - 1] = hist
                s = s + hist[:, g * gd:]
                width += 1
            count = float(min(PAST_LEN + 1, w))
            pooled.append(s[:, 0:gd] / count - up[:, g * gd:(g + 1) * gd])
            if g + 1 < len(POOL_WINDOWS):
                s = s[:, gd:]
        npool_ref[POOL_BUF - 1] = up
        y_pool = _pool_project(jnp.concatenate(pooled, axis=1), wp_ref, ps_ref[...])
        for c in state_out:
            c.start()

        y_mix = jnp.concatenate([y_lru, y_pool.astype(BF16)], axis=1)
        for c in range(n_out):
            cols = slice(c * DEC_CHUNK, (c + 1) * DEC_CHUNK)
            o_ref[:, cols] = x_ref[:, cols] + _dot(y_mix, wo16_ref[c])
        for c in state_out:
            c.wait()


def _dec_mixer(x, h0, cbuf, pbuf, p, w_in, w_out):
    rows, d = x.shape
    n_in = w_in.shape[1] // DEC_CHUNK
    n_out = w_out.shape[1] // DEC_CHUNK
    assert n_out <= n_in
    small = (p["conv_w"], p["conv_b"], p["w_gate"], p["b_rg_a"], p["b_rg_x"], p["lru_lambda"],
             p["w_pool"], p["pool_scale"])
    full = lambda a: pl.BlockSpec(a.shape, functools.partial(lambda nd, j: (0,) * nd, a.ndim))
    once = lambda a: _const_spec(a.shape)
    in_hbm = pl.BlockSpec(memory_space=pl.ANY)
    in_chunk = lambda j: (0, jnp.minimum(j, n_in - 1))
    out_chunk = lambda j: (0, jnp.minimum(j, n_out - 1))
    return pl.pallas_call(
        functools.partial(_dec_mixer_kernel, n_in=n_in),
        grid=(n_in + 1,),
        in_specs=([once(x), once(h0), in_hbm, in_hbm,
                   once(p["norm_mix_g"]), pl.BlockSpec((d, DEC_CHUNK), in_chunk)]
                  + [once(a) for a in small]
                  + [pl.BlockSpec((w_out.shape[0], DEC_CHUNK), out_chunk)]),
        out_specs=[full(x), full(h0), in_hbm, in_hbm,
                   pl.BlockSpec((d, DEC_CHUNK), in_chunk),
                   pl.BlockSpec((w_out.shape[0], DEC_CHUNK), out_chunk)],
        out_shape=[jax.ShapeDtypeStruct(s.shape, F32) for s in (x, h0, cbuf, pbuf)]
                  + [jax.ShapeDtypeStruct(w_in.shape, BF16), jax.ShapeDtypeStruct(w_out.shape, BF16)],
        scratch_shapes=[
            pltpu.VMEM((rows, d), BF16),
            pltpu.VMEM((n_in, rows, DEC_CHUNK), F32),
            pltpu.VMEM((n_out, w_out.shape[0], DEC_CHUNK), BF16),
            pltpu.VMEM(cbuf.shape, F32),
            pltpu.VMEM(pbuf.shape, F32),
            pltpu.VMEM(cbuf.shape, F32),
            pltpu.VMEM(pbuf.shape, F32),
            pltpu.SemaphoreType.DMA((4,)),
        ],
        compiler_params=pltpu.CompilerParams(
            dimension_semantics=("arbitrary",), vmem_limit_bytes=DEC_VMEM_LIMIT),
        name="dec_mixer",
    )(x, h0, cbuf, pbuf, p["norm_mix_g"], w_in, *small, w_out)


def _mlp_kernel(x_ref, xs_ref, g_ref, wup_ref, wdn_ref, gfin_ref, o_ref, os_ref, h_ref):
    tm = x_ref.shape[0]
    i = pl.program_id(0)
    j = pl.program_id(1)
    first = j == 0
    last = j == pl.num_programs(1) - 1
    with_decode = i == 0

    def _start(src_ref, rows, acc_ref):
        x = src_ref[...]
        h_ref[rows, :] = _rmsnorm(x, g_ref[...]).astype(BF16)
        acc_ref[...] = x

    def _ffn(rows):
        f = jnp.maximum(_dot(h_ref[rows, :], wup_ref[...]), 0.0)
        return _dot((f * f).astype(BF16), wdn_ref[...])

    pl.when(first & with_decode)(lambda: _start(x_ref, pl.ds(0, tm), o_ref))
    pl.when(first & with_decode)(lambda: _start(xs_ref, pl.ds(tm, xs_ref.shape[0]), os_ref))

    @pl.when(with_decode)
    def _both_streams():
        part = _ffn(pl.ds(0, h_ref.shape[0]))
        o_ref[...] += part[:tm]
        os_ref[...] += part[tm:]

    prompt_only = jnp.logical_not(with_decode)

    @pl.when(prompt_only & first)
    def _first_tile():
        x = x_ref[...]
        o_ref[...] = x
        scale = lax.rsqrt(jnp.mean(x * x, axis=-1, keepdims=True) + EPS)
        d = x.shape[1]
        kc = d // NORM_CHUNKS
        f = None
        for c in range(NORM_CHUNKS):
            cols = slice(c * kc, (c + 1) * kc)
            hc = ((x_ref[:, cols] * scale) * g_ref[:, cols]).astype(BF16)
            h_ref[pl.ds(0, tm), cols] = hc
            part = _dot(hc, wup_ref[cols, :])
            f = part if f is None else f + part
        f = jnp.maximum(f, 0.0)
        o_ref[...] += _dot((f * f).astype(BF16), wdn_ref[...])

    @pl.when(prompt_only & jnp.logical_not(first))
    def _prompt_only():
        o_ref[...] += _ffn(pl.ds(0, tm))

    @pl.when(last)
    def _finish():
        o_ref[...] = _rmsnorm(o_ref[...], gfin_ref[...])

    @pl.when(last & with_decode)
    def _finish_decode():
        os_ref[...] = _rmsnorm(os_ref[...], gfin_ref[...])


def _mlp(x, xs, p, tm, tf):
    n, d = x.shape
    ns = xs.shape[0]
    d_ff = p["w_up"].shape[1]
    vec_spec = pl.BlockSpec((1, d), lambda i, j: (0, 0))
    tok_spec = pl.BlockSpec((tm, d), lambda i, j: (i, 0))
    dec_spec = pl.BlockSpec((ns, d), lambda i, j: (0, 0))
    return pl.pallas_call(
        _mlp_kernel,
        grid=(n // tm, d_ff // tf),
        in_specs=[tok_spec, dec_spec, vec_spec,
                  pl.BlockSpec((d, tf), lambda i, j: (0, j)),
                  pl.BlockSpec((tf, d), lambda i, j: (j, 0)),
                  vec_spec],
        out_specs=[tok_spec, dec_spec],
        out_shape=[jax.ShapeDtypeStruct((n, d), F32), jax.ShapeDtypeStruct((ns, d), F32)],
        scratch_shapes=[pltpu.VMEM((tm + ns, d), BF16)],
        compiler_params=pltpu.CompilerParams(
            dimension_semantics=("arbitrary", "arbitrary"),
            vmem_limit_bytes=MLP_VMEM_LIMIT),
        name="mlp",
    )(x, xs, p["norm_mlp_g"], p["w_up"], p["w_down"], p["norm_final_g"])


def _layer_params(l, norm_mix_g, w_in, conv_w, conv_b, w_rg_a, b_rg_a, w_rg_x, b_rg_x, lru_lambda,
                  w_pool, pool_scale, w_out, norm_mlp_g, w_up, w_down, norm_final_g):
    row = lambda v: v.reshape(1, -1)
    return {
        "norm_mix_g": row(norm_mix_g[l]),
        "w_in": w_in[l],
        "conv_w": conv_w[l],
        "conv_b": row(conv_b[l]),
        "w_gate": jnp.concatenate([w_rg_a[l], w_rg_x[l]], axis=-1).astype(BF16),
        "b_rg_a": row(b_rg_a[l]),
        "b_rg_x": row(b_rg_x[l]),
        "lru_lambda": row(lru_lambda[l]),
        "w_pool": w_pool[l].astype(BF16),
        "pool_scale": row(pool_scale[l]),
        "w_out": w_out[l],
        "norm_mlp_g": row(norm_mlp_g[l]),
        "w_up": w_up[l],
        "w_down": w_down[l],
        "norm_final_g": row(norm_final_g),
    }


def kernel(x_prompt, x_sample, state_lru_h, state_conv, state_pool, norm_mix_g, w_in, conv_w, conv_b, w_rg_a, b_rg_a, w_rg_x, b_rg_x, lru_lambda, w_pool, pool_scale, w_out, norm_mlp_g, w_up, w_down, norm_final_g):
    depth = w_in.shape[0]
    assert depth == 1, "final RMSNorm is fused into the (single) layer's MLP kernel"
    bp, tp, d = x_prompt.shape
    bs, ts, _ = x_sample.shape
    assert ts == 1 and tp % MIX_TM == 0 and (bp * tp) % MLP_TM == 0
    p = _layer_params(0, norm_mix_g, w_in, conv_w, conv_b, w_rg_a, b_rg_a, w_rg_x, b_rg_x,
                      lru_lambda, w_pool, pool_scale, w_out, norm_mlp_g, w_up, w_down, norm_final_g)

    hist_major = lambda s: jnp.transpose(s, (1, 0, 2))
    x1s, hs, cs, ps, p["w_in"], p["w_out"] = _dec_mixer(
        x_sample.reshape(bs, d), state_lru_h[0], hist_major(state_conv[0]), hist_major(state_pool[0]),
        p, p["w_in"], p["w_out"])
    cs, ps = hist_major(cs), hist_major(ps)
    x1p, hp, cp, pp, p["w_up"], p["w_down"] = _seq_mixer(x_prompt, p, p["w_up"], p["w_down"], MIX_TM)
    yp, ys = _mlp(x1p.reshape(bp * tp, d), x1s, p, MLP_TM, MLP_TF)
    yp, ys = yp.reshape(bp, tp, d), ys.reshape(bs, ts, d)

    return (yp, ys, hp.reshape(1, bp, -1), cp[None], pp[None], hs[None], cs[None], ps[None])
```

```python
import functools

import jax
import jax.numpy as jnp
from jax import lax
from jax.experimental import pallas as pl
from jax.experimental.pallas import tpu as pltpu

EPS = 1e-6
LRU_HEADS = 8
LRU_C = 8.0
CONV_W = 4
POOL_WINDOWS = (2, 4, 8, 16)
POOL_BUF = max(POOL_WINDOWS) - 1
PAST_LEN = 16384

SUBLANES = 8
MIB = 1024 * 1024

MIX_TM = 256
MLP_TM = 512
MLP_TF = 2048
NORM_CHUNKS = 4
MIX_VMEM_LIMIT = 60 * MIB
DEC_VMEM_LIMIT = 60 * MIB
MLP_VMEM_LIMIT = 60 * MIB

F32 = jnp.float32
BF16 = jnp.bfloat16


def _rmsnorm(x, g):
    return (x * lax.rsqrt(jnp.mean(x * x, axis=-1, keepdims=True) + EPS)) * g


def _dot(a, b):
    return jnp.dot(a, b, preferred_element_type=F32)


def _gate_update(g, xc, ba, bx, cneg):
    hd = xc.shape[1]
    r = jax.nn.sigmoid(g[:, :hd] + ba)
    i = jax.nn.sigmoid(g[:, hd:] + bx)
    a = jnp.exp(r * cneg)
    y = 1.0 - a * a
    mult = jnp.where(y > 0.0, y * lax.rsqrt(y), 0.0)
    return a, xc * i, mult


def _lru_gates(xc, wg_ref, ba, bx, cneg):
    hd = xc.shape[1] // LRU_HEADS
    xcb = xc.astype(BF16)
    a_parts, m_parts, xi_parts = [], [], []
    for h in range(LRU_HEADS):
        sl = slice(h * hd, (h + 1) * hd)
        g = _dot(xcb[:, sl], wg_ref[h])
        a, xi, mult = _gate_update(g, xc[:, sl], ba[:, sl], bx[:, sl], cneg[:, sl])
        a_parts.append(a)
        m_parts.append(mult)
        xi_parts.append(xi)
    cat = lambda ps: jnp.concatenate(ps, axis=1)
    return cat(a_parts), cat(m_parts), cat(xi_parts)


def _pool_project(pooled, wp_ref, scale):
    n_g = len(POOL_WINDOWS)
    gd = pooled.shape[1] // n_g
    pb = pooled.astype(BF16)
    outs = [_dot(pb[:, g * gd:(g + 1) * gd], wp_ref[g]) for g in range(n_g)]
    return jnp.concatenate(outs, axis=1) * scale


def _neg_c_softplus(lam):
    return -LRU_C * jax.nn.softplus(-lam)


CONV_HALO = CONV_W - 1
POOL_HALO = POOL_BUF


def _rows(k, n=1):
    return pl.ds(k * SUBLANES, n * SUBLANES)


def _tile_copies(hbm, buf, sem, b, t, slot, n_grp, to_hbm):
    tm = n_grp * SUBLANES
    copies = []
    for s in range(SUBLANES):
        far = hbm.at[b, pl.ds(t * tm + s * n_grp, n_grp), :]
        near = buf.at[slot, :, s, :]
        src, dst = (near, far) if to_hbm else (far, near)
        copies.append(pltpu.make_async_copy(src, dst, sem.at[slot]))
    return copies


def _fill_halo(ext_ref, carry_ref, halo, n_grp):
    sub = lax.broadcasted_iota(jnp.int32, (SUBLANES, ext_ref.shape[1]), 0)
    for j in range(1, halo + 1):
        cur = ext_ref[_rows(halo + n_grp - j), :]
        prev = carry_ref[_rows(halo - j), :]
        ext_ref[_rows(halo - j), :] = pltpu.roll(jnp.where(sub == SUBLANES - 1, prev, cur), 1, 0)
    carry_ref[...] = ext_ref[_rows(n_grp, halo), :]


def _seq_mixer_kernel(x_hbm, gmix_ref, w_in_ref, cw_ref, cb_ref, wg_ref, ba_ref, bx_ref,
                      lam_ref, wp_ref, ps_ref, w_out_ref, wup32_ref, wdn32_ref,
                      o_hbm, hlast_ref, nconv_ref, npool_ref, wup16_ref, wdn16_ref,
                      xbuf, obuf, in_sem, out_sem, hn_ref, extx_ref, carryx_ref, extp_ref,
                      carryp_ref, xc_ref, xcb_ref, gate_ref, ug_ref, a_ref, b_ref, ymix_ref, hc_ref,
                      *, n_t):
    n_grp = xbuf.shape[1]
    tm = n_grp * SUBLANES
    d = xbuf.shape[3]
    w_lru = a_ref.shape[1]
    w_pool = extp_ref.shape[1]
    hd = w_lru // LRU_HEADS
    half = w_lru // 2
    b = pl.program_id(0)
    t = pl.program_id(1)
    step = b * n_t + t
    n_steps = pl.num_programs(0) * n_t
    slot = step % 2

    @pl.when(step == 0)
    def _first_fetch():
        for c in _tile_copies(x_hbm, xbuf, in_sem, b, t, slot, n_grp, False):
            c.start()

    @pl.when(step + 1 < n_steps)
    def _prefetch():
        nxt = step + 1
        for c in _tile_copies(x_hbm, xbuf, in_sem, nxt // n_t, nxt % n_t, 1 - slot, n_grp, False):
            c.start()

    for c in _tile_copies(x_hbm, xbuf, in_sem, b, t, slot, n_grp, False):
        c.wait()

    @pl.when(t == 0)
    def _reset_state():
        hc_ref[...] = jnp.zeros_like(hc_ref)
        carryx_ref[...] = jnp.zeros_like(carryx_ref)
        carryp_ref[...] = jnp.zeros_like(carryp_ref)

    x_tile = lambda: xbuf[slot].reshape(tm, d)
    hn_ref[...] = _rmsnorm(x_tile(), gmix_ref[...]).astype(BF16)
    in_proj = lambda c0, c1: _dot(hn_ref[...], w_in_ref[:, c0:c1])
    sub = lax.broadcasted_iota(jnp.int32, (SUBLANES, w_lru), 0)
    first_token = (sub == 0) & (t == 0)
    cneg = _neg_c_softplus(lam_ref[...])

    extx_ref[_rows(CONV_HALO, n_grp), :] = in_proj(0, w_lru)
    _fill_halo(extx_ref, carryx_ref, CONV_HALO, n_grp)

    extp_ref[_rows(POOL_HALO, n_grp), 0:w_pool // 2] = in_proj(2 * w_lru, 2 * w_lru + w_pool // 2)
    for gp in range(n_grp // 2):
        g = 2 * gp
        acc = cb_ref[...] + extx_ref[_rows(CONV_HALO + g, 2), :] * cw_ref[CONV_W - 1:CONV_W, :]
        for k in range(CONV_W - 1):
            acc = acc + extx_ref[_rows(g + k, 2), :] * cw_ref[k:k + 1, :]
        xc_ref[_rows(g, 2), :] = acc
        xcb_ref[_rows(g, 2), :] = acc.astype(BF16)
    for hh in range(LRU_HEADS):
        gate_ref[:, 2 * hd * hh:2 * hd * (hh + 1)] = _dot(xcb_ref[:, hd * hh:hd * (hh + 1)], wg_ref[hh])

    extp_ref[_rows(POOL_HALO, n_grp), w_pool // 2:] = in_proj(2 * w_lru + w_pool // 2, 2 * w_lru + w_pool)
    _fill_halo(extp_ref, carryp_ref, POOL_HALO, n_grp)
    ug_ref[:, 0:half] = in_proj(w_lru, w_lru + half)

    def _scan_groups(g0, g1, a_cum, h_loc):
        for g in range(g0, g1):
            gates = gate_ref[_rows(g), :]
            ga = jnp.concatenate([gates[:, 2 * hd * hh:2 * hd * hh + hd] for hh in range(LRU_HEADS)], axis=1)
            gx = jnp.concatenate([gates[:, 2 * hd * hh + hd:2 * hd * (hh + 1)] for hh in range(LRU_HEADS)], axis=1)
            a, xi, mult = _gate_update(jnp.concatenate([ga, gx], axis=1), xc_ref[_rows(g), :],
                                       ba_ref[...], bx_ref[...], cneg)
            if g == 0:
                mult = jnp.where(first_token, 1.0, mult)
            bt = xi * mult
            a_cum = a if a_cum is None else a * a_cum
            h_loc = bt if h_loc is None else a * h_loc + bt
            a_ref[_rows(g), :] = a_cum
            b_ref[_rows(g), :] = h_loc
        return a_cum, h_loc

    a_tot, h_tot = _scan_groups(0, n_grp // 2, None, None)
    ug_ref[:, half:] = in_proj(w_lru + half, 2 * w_lru)
    a_tot, h_tot = _scan_groups(n_grp // 2, n_grp, a_tot, h_tot)

    gd = w_pool // len(POOL_WINDOWS)
    row = lax.broadcasted_iota(jnp.int32, (tm, 1), 0)
    pos = t * tm + (row & (SUBLANES - 1)) * n_grp + (row >> 3)
    pooled = []
    for gi, w in enumerate(POOL_WINDOWS):
        cols = slice(gi * gd, (gi + 1) * gd)
        lo = POOL_HALO - (w - 1)
        s = extp_ref[_rows(lo, n_grp + w - 1), cols]
        width = 1
        while width < w:
            s = s[width * SUBLANES:, :] + s[:-width * SUBLANES, :]
            width *= 2
        inv = 1.0 / jnp.minimum(pos + 1, w).astype(F32)
        pooled.append(s * inv - extp_ref[_rows(POOL_HALO, n_grp), cols])

    y_pool = _pool_project(jnp.concatenate(pooled, axis=1), wp_ref, ps_ref[...])
    ymix_ref[:, w_lru:] = y_pool.astype(BF16)
    obuf[slot] = (x_tile() + _dot(ymix_ref[:, w_lru:], w_out_ref[w_lru:, :])).reshape(n_grp, SUBLANES, d)

    p_cum, q_cum = a_tot, h_tot
    for dd in (1, 2, 4):
        keep = sub >= dd
        q_cum = jnp.where(keep, p_cum * pltpu.roll(q_cum, dd, 0) + q_cum, q_cum)
        p_cum = jnp.where(keep, p_cum * pltpu.roll(p_cum, dd, 0), p_cum)
    h_in = jnp.broadcast_to(hc_ref[...], (SUBLANES, w_lru))
    seg_end = p_cum * h_in + q_cum
    seg_in = jnp.where(sub == 0, h_in, pltpu.roll(seg_end, 1, 0))
    h_end = seg_end[SUBLANES - 1:SUBLANES, :]
    hc_ref[...] = h_end

    seg_in2 = jnp.concatenate([seg_in, seg_in], axis=0)
    for gp in range(n_grp // 2):
        r2 = _rows(2 * gp, 2)
        hs = b_ref[r2, :] + a_ref[r2, :] * seg_in2
        ymix_ref[r2, 0:w_lru] = (hs * jax.nn.gelu(ug_ref[r2, :], approximate=True)).astype(BF16)

    obuf[slot] += _dot(ymix_ref[:, 0:w_lru], w_out_ref[0:w_lru, :]).reshape(n_grp, SUBLANES, d)

    wup16_ref[...] = wup32_ref[...].astype(BF16)
    wdn16_ref[...] = wdn32_ref[...].astype(BF16)

    for c in _tile_copies(o_hbm, obuf, out_sem, b, t, slot, n_grp, True):
        c.start()

    @pl.when(step > 0)
    def _wait_prev_writeback():
        prv = step - 1
        for c in _tile_copies(o_hbm, obuf, out_sem, prv // n_t, prv % n_t, 1 - slot, n_grp, True):
            c.wait()

    @pl.when(step == n_steps - 1)
    def _wait_last_writeback():
        for c in _tile_copies(o_hbm, obuf, out_sem, b, t, slot, n_grp, True):
            c.wait()

    @pl.when(t == n_t - 1)
    def _emit_state():
        last = SUBLANES - 1
        hlast_ref[...] = h_end
        for i in range(CONV_HALO):
            nconv_ref[i:i + 1, :] = carryx_ref[pl.ds(i * SUBLANES + last, 1), :]
        for i in range(POOL_HALO):
            npool_ref[i:i + 1, :] = carryp_ref[pl.ds(i * SUBLANES + last, 1), :]


def _const_spec(shape):
    nd = len(shape)
    return pl.BlockSpec(shape, lambda *_: (0,) * nd, pipeline_mode=pl.Buffered(1))


def _seq_mixer(x, p, w_up, w_down, tm):
    bsz, seq, d = x.shape
    w_lru = p["conv_b"].shape[1]
    w_pool = p["pool_scale"].shape[1]
    weights = (p["norm_mix_g"], p["w_in"], p["conv_w"], p["conv_b"], p["w_gate"], p["b_rg_a"],
               p["b_rg_x"], p["lru_lambda"], p["w_pool"], p["pool_scale"], p["w_out"])
    n_t = seq // tm
    n_steps = bsz * n_t
    n_grp = tm // SUBLANES
    assert n_grp % 2 == 0 and n_grp >= POOL_HALO
    any_spec = pl.BlockSpec(memory_space=pl.ANY)
    state_spec = lambda rows, w: pl.BlockSpec((None, rows, w), lambda b, t: (b, 0, 0))
    slab_spec = lambda w: pl.BlockSpec((w.shape[0] // n_steps, w.shape[1]),
                                       lambda b, t: (b * n_t + t, 0))
    ext_rows = lambda halo: (halo + n_grp) * SUBLANES
    return pl.pallas_call(
        functools.partial(_seq_mixer_kernel, n_t=n_t),
        grid=(bsz, n_t),
        in_specs=([any_spec] + [_const_spec(w.shape) for w in weights]
                  + [slab_spec(w_up), slab_spec(w_down)]),
        out_specs=[any_spec, state_spec(1, w_lru), state_spec(CONV_W - 1, w_lru),
                   state_spec(POOL_BUF, w_pool), slab_spec(w_up), slab_spec(w_down)],
        out_shape=[jax.ShapeDtypeStruct((bsz, seq, d), F32),
                   jax.ShapeDtypeStruct((bsz, 1, w_lru), F32),
                   jax.ShapeDtypeStruct((bsz, CONV_W - 1, w_lru), F32),
                   jax.ShapeDtypeStruct((bsz, POOL_BUF, w_pool), F32),
                   jax.ShapeDtypeStruct(w_up.shape, BF16),
                   jax.ShapeDtypeStruct(w_down.shape, BF16)],
        scratch_shapes=[
            pltpu.VMEM((2, n_grp, SUBLANES, d), F32),
            pltpu.VMEM((2, n_grp, SUBLANES, d), F32),
            pltpu.SemaphoreType.DMA((2,)),
            pltpu.SemaphoreType.DMA((2,)),
            pltpu.VMEM((tm, d), BF16),
            pltpu.VMEM((ext_rows(CONV_HALO), w_lru), F32),
            pltpu.VMEM((CONV_HALO * SUBLANES, w_lru), F32),
            pltpu.VMEM((ext_rows(POOL_HALO), w_pool), F32),
            pltpu.VMEM((POOL_HALO * SUBLANES, w_pool), F32),
            pltpu.VMEM((tm, w_lru), F32),
            pltpu.VMEM((tm, w_lru), BF16),
            pltpu.VMEM((tm, 2 * w_lru), F32),
            pltpu.VMEM((tm, w_lru), F32),
            pltpu.VMEM((tm, w_lru), F32),
            pltpu.VMEM((tm, w_lru), F32),
            pltpu.VMEM((tm, d), BF16),
            pltpu.VMEM((1, w_lru), F32),
        ],
        compiler_params=pltpu.CompilerParams(
            dimension_semantics=("arbitrary", "arbitrary"),
            vmem_limit_bytes=MIX_VMEM_LIMIT),
        name="seq_mixer",
    )(x, *weights, w_up, w_down)


DEC_CHUNK = 256


def _dec_mixer_kernel(x_ref, h0_ref, cbuf_hbm, pbuf_hbm, gmix_ref, w_in_ref, cw_ref, cb_ref,
                      wg_ref, ba_ref, bx_ref, lam_ref, wp_ref, ps_ref, w_out_ref,
                      o_ref, hnew_ref, nconv_hbm, npool_hbm, w_in16_ref, w_out16_ref,
                      hn_ref, proj_ref, wo16_ref, cbuf_ref, pbuf_ref, nconv_ref, npool_ref, state_sem):
    n_k = hn_ref.shape[0]
    j = pl.program_id(0)
    w_lru = h0_ref.shape[1]
    w_pool = ps_ref.shape[1]
    state_in = [pltpu.make_async_copy(cbuf_hbm, cbuf_ref, state_sem.at[0]),
                pltpu.make_async_copy(pbuf_hbm, pbuf_ref, state_sem.at[1])]
    state_out = [pltpu.make_async_copy(nconv_ref, nconv_hbm, state_sem.at[2]),
                 pltpu.make_async_copy(npool_ref, npool_hbm, state_sem.at[3])]

    @pl.when(j == 0)
    def _norm():
        for c in state_in:
            c.start()
        hn = _rmsnorm(x_ref[...], gmix_ref[...]).astype(BF16)
        for c in range(n_k):
            hn_ref[c] = hn[:, c * DEC_CHUNK:(c + 1) * DEC_CHUNK]
        proj_ref[...] = jnp.zeros_like(proj_ref)

    @pl.when(j < n_k)
    def _stream_weights():
        w = w_in_ref[...].astype(BF16)
        w_in16_ref[...] = w
        proj_ref[...] += _dot(hn_ref[j], w)
        w = w_out_ref[...].astype(BF16)
        w_out16_ref[...] = w
        wo16_ref[j] = w

    @pl.when(j == n_k)
    def _mix():
        ux = proj_ref[:, 0:w_lru]
        ug = proj_ref[:, w_lru:2 * w_lru]
        up = proj_ref[:, 2 * w_lru:]
        for c in state_in:
            c.wait()

        xc = cb_ref[...] + ux * cw_ref[CONV_W - 1:CONV_W, :]
        for k in range(CONV_W - 1):
            tap = cbuf_ref[k]
            xc = xc + tap * cw_ref[k:k + 1, :]
            if k > 0:
                nconv_ref[k - 1] = tap
        nconv_ref[CONV_W - 2] = ux

        a, mult, xi = _lru_gates(xc, wg_ref, ba_ref[...], bx_ref[...], _neg_c_softplus(lam_ref[...]))
        h_new = a * h0_ref[...] + xi * mult
        hnew_ref[...] = h_new
        y_lru = (h_new * jax.nn.gelu(ug, approximate=True)).astype(BF16)

        gd = w_pool // len(POOL_WINDOWS)
        s = up
        width = 1
        pooled = []
        for g, w in enumerate(POOL_WINDOWS):
            while width < w:
                k = POOL_BUF - width
                hist = pbuf_ref[k]
                if k > 0:
                    npool_ref[k - 1] = hist
                s = s + hist[:, g * gd:]
                width += 1
            count = float(min(PAST_LEN + 1, w))
            pooled.append(s[:, 0:gd] / count - up[:, g * gd:(g + 1) * gd])
            if g + 1 < len(POOL_WINDOWS):
                s = s[:, gd:]
        npool_ref[POOL_BUF - 1] = up
        y_pool = _pool_project(jnp.concatenate(pooled, axis=1), wp_ref, ps_ref[...])
        for c in state_out:
            c.start()

        y_mix = jnp.concatenate([y_lru, y_pool.astype(BF16)], axis=1)
        w_out16 = wo16_ref[...].reshape(n_k * DEC_CHUNK, o_ref.shape[1])
        o_ref[...] = x_ref[...] + _dot(y_mix, w_out16)
        for c in state_out:
            c.wait()


def _dec_mixer(x, h0, cbuf, pbuf, p, w_in, w_out):
    rows, d = x.shape
    n_k = d // DEC_CHUNK
    assert w_in.shape[0] == d and w_out.shape[0] == d
    small = (p["conv_w"], p["conv_b"], p["w_gate"], p["b_rg_a"], p["b_rg_x"], p["lru_lambda"],
             p["w_pool"], p["pool_scale"])
    full = lambda a: pl.BlockSpec(a.shape, functools.partial(lambda nd, j: (0,) * nd, a.ndim))
    once = lambda a: _const_spec(a.shape)
    in_hbm = pl.BlockSpec(memory_space=pl.ANY)
    row_chunk = lambda j: (jnp.minimum(j, n_k - 1), 0)
    return pl.pallas_call(
        _dec_mixer_kernel,
        grid=(n_k + 1,),
        in_specs=([once(x), once(h0), in_hbm, in_hbm,
                   once(p["norm_mix_g"]), pl.BlockSpec((DEC_CHUNK, w_in.shape[1]), row_chunk)]
                  + [once(a) for a in small]
                  + [pl.BlockSpec((DEC_CHUNK, w_out.shape[1]), row_chunk)]),
        out_specs=[full(x), full(h0), in_hbm, in_hbm,
                   pl.BlockSpec((DEC_CHUNK, w_in.shape[1]), row_chunk),
                   pl.BlockSpec((DEC_CHUNK, w_out.shape[1]), row_chunk)],
        out_shape=[jax.ShapeDtypeStruct(s.shape, F32) for s in (x, h0, cbuf, pbuf)]
                  + [jax.ShapeDtypeStruct(w_in.shape, BF16), jax.ShapeDtypeStruct(w_out.shape, BF16)],
        scratch_shapes=[
            pltpu.VMEM((n_k, rows, DEC_CHUNK), BF16),
            pltpu.VMEM((rows, w_in.shape[1]), F32),
            pltpu.VMEM((n_k, DEC_CHUNK, w_out.shape[1]), BF16),
            pltpu.VMEM(cbuf.shape, F32),
            pltpu.VMEM(pbuf.shape, F32),
            pltpu.VMEM(cbuf.shape, F32),
            pltpu.VMEM(pbuf.shape, F32),
            pltpu.SemaphoreType.DMA((4,)),
        ],
        compiler_params=pltpu.CompilerParams(
            dimension_semantics=("arbitrary",), vmem_limit_bytes=DEC_VMEM_LIMIT),
        name="dec_mixer",
    )(x, h0, cbuf, pbuf, p["norm_mix_g"], w_in, *small, w_out)


def _mlp_kernel(x_ref, xs_ref, g_ref, wup_ref, wdn_ref, gfin_ref, o_ref, os_ref, h_ref):
    tm = x_ref.shape[0]
    i = pl.program_id(0)
    j = pl.program_id(1)
    first = j == 0
    last = j == pl.num_programs(1) - 1
    with_decode = i == 0

    def _start(src_ref, rows, acc_ref):
        x = src_ref[...]
        h_ref[rows, :] = _rmsnorm(x, g_ref[...]).astype(BF16)
        acc_ref[...] = x

    def _ffn(rows):
        f = jnp.maximum(_dot(h_ref[rows, :], wup_ref[...]), 0.0)
        return _dot((f * f).astype(BF16), wdn_ref[...])

    pl.when(first & with_decode)(lambda: _start(x_ref, pl.ds(0, tm), o_ref))
    pl.when(first & with_decode)(lambda: _start(xs_ref, pl.ds(tm, xs_ref.shape[0]), os_ref))

    @pl.when(with_decode)
    def _both_streams():
        part = _ffn(pl.ds(0, h_ref.shape[0]))
        o_ref[...] += part[:tm]
        os_ref[...] += part[tm:]

    prompt_only = jnp.logical_not(with_decode)

    @pl.when(prompt_only & first)
    def _first_tile():
        x = x_ref[...]
        o_ref[...] = x
        scale = lax.rsqrt(jnp.mean(x * x, axis=-1, keepdims=True) + EPS)
        d = x.shape[1]
        kc = d // NORM_CHUNKS
        f = None
        for c in range(NORM_CHUNKS):
            cols = slice(c * kc, (c + 1) * kc)
            hc = ((x_ref[:, cols] * scale) * g_ref[:, cols]).astype(BF16)
            h_ref[pl.ds(0, tm), cols] = hc
            part = _dot(hc, wup_ref[cols, :])
            f = part if f is None else f + part
        f = jnp.maximum(f, 0.0)
        o_ref[...] += _dot((f * f).astype(BF16), wdn_ref[...])

    @pl.when(prompt_only & jnp.logical_not(first))
    def _prompt_only():
        o_ref[...] += _ffn(pl.ds(0, tm))

    @pl.when(last)
    def _finish():
        o_ref[...] = _rmsnorm(o_ref[...], gfin_ref[...])

    @pl.when(last & with_decode)
    def _finish_decode():
        os_ref[...] = _rmsnorm(os_ref[...], gfin_ref[...])


def _mlp(x, xs, p, tm, tf):
    n, d = x.shape
    ns = xs.shape[0]
    d_ff = p["w_up"].shape[1]
    vec_spec = pl.BlockSpec((1, d), lambda i, j: (0, 0))
    tok_spec = pl.BlockSpec((tm, d), lambda i, j: (i, 0))
    dec_spec = pl.BlockSpec((ns, d), lambda i, j: (0, 0))
    return pl.pallas_call(
        _mlp_kernel,
        grid=(n // tm, d_ff // tf),
        in_specs=[tok_spec, dec_spec, vec_spec,
                  pl.BlockSpec((d, tf), lambda i, j: (0, j)),
                  pl.BlockSpec((tf, d), lambda i, j: (j, 0)),
                  vec_spec],
        out_specs=[tok_spec, dec_spec],
        out_shape=[jax.ShapeDtypeStruct((n, d), F32), jax.ShapeDtypeStruct((ns, d), F32)],
        scratch_shapes=[pltpu.VMEM((tm + ns, d), BF16)],
        compiler_params=pltpu.CompilerParams(
            dimension_semantics=("arbitrary", "arbitrary"),
            vmem_limit_bytes=MLP_VMEM_LIMIT),
        name="mlp",
    )(x, xs, p["norm_mlp_g"], p["w_up"], p["w_down"], p["norm_final_g"])


def _layer_params(l, norm_mix_g, w_in, conv_w, conv_b, w_rg_a, b_rg_a, w_rg_x, b_rg_x, lru_lambda,
                  w_pool, pool_scale, w_out, norm_mlp_g, w_up, w_down, norm_final_g):
    row = lambda v: v.reshape(1, -1)
    return {
        "norm_mix_g": row(norm_mix_g[l]),
        "w_in": w_in[l],
        "conv_w": conv_w[l],
        "conv_b": row(conv_b[l]),
        "w_gate": jnp.concatenate([w_rg_a[l], w_rg_x[l]], axis=-1).astype(BF16),
        "b_rg_a": row(b_rg_a[l]),
        "b_rg_x": row(b_rg_x[l]),
        "lru_lambda": row(lru_lambda[l]),
        "w_pool": w_pool[l].astype(BF16),
        "pool_scale": row(pool_scale[l]),
        "w_out": w_out[l],
        "norm_mlp_g": row(norm_mlp_g[l]),
        "w_up": w_up[l],
        "w_down": w_down[l],
        "norm_final_g": row(norm_final_g),
    }


def kernel(x_prompt, x_sample, state_lru_h, state_conv, state_pool, norm_mix_g, w_in, conv_w, conv_b, w_rg_a, b_rg_a, w_rg_x, b_rg_x, lru_lambda, w_pool, pool_scale, w_out, norm_mlp_g, w_up, w_down, norm_final_g):
    depth = w_in.shape[0]
    assert depth == 1, "final RMSNorm is fused into the (single) layer's MLP kernel"
    bp, tp, d = x_prompt.shape
    bs, ts, _ = x_sample.shape
    assert ts == 1 and tp % MIX_TM == 0 and (bp * tp) % MLP_TM == 0
    p = _layer_params(0, norm_mix_g, w_in, conv_w, conv_b, w_rg_a, b_rg_a, w_rg_x, b_rg_x,
                      lru_lambda, w_pool, pool_scale, w_out, norm_mlp_g, w_up, w_down, norm_final_g)

    hist_major = lambda s: jnp.transpose(s, (1, 0, 2))
    x1s, hs, cs, ps, p["w_in"], p["w_out"] = _dec_mixer(
        x_sample.reshape(bs, d), state_lru_h[0], hist_major(state_conv[0]), hist_major(state_pool[0]),
        p, p["w_in"], p["w_out"])
    cs, ps = hist_major(cs), hist_major(ps)
    x1p, hp, cp, pp, p["w_up"], p["w_down"] = _seq_mixer(x_prompt, p, p["w_up"], p["w_down"], MIX_TM)
    yp, ys = _mlp(x1p.reshape(bp * tp, d), x1s, p, MLP_TM, MLP_TF)
    yp, ys = yp.reshape(bp, tp, d), ys.reshape(bs, ts, d)

    return (yp, ys, hp.reshape(1, bp, -1), cp[None], pp[None], hs[None], cs[None], ps[None])
```

```python
import functools

import jax
import jax.numpy as jnp
from jax import lax
from jax.experimental import pallas as pl
from jax.experimental.pallas import tpu as pltpu

EPS = 1e-6
LRU_HEADS = 8
LRU_C = 8.0
CONV_W = 4
POOL_WINDOWS = (2, 4, 8, 16)
POOL_BUF = max(POOL_WINDOWS) - 1
PAST_LEN = 16384

SUBLANES = 8
MIB = 1024 * 1024

MIX_TM = 256
MLP_TM = 512
MLP_TF = 2048
NORM_CHUNKS = 4
MIX_VMEM_LIMIT = 60 * MIB
DEC_VMEM_LIMIT = 60 * MIB
MLP_VMEM_LIMIT = 60 * MIB

F32 = jnp.float32
BF16 = jnp.bfloat16


def _rmsnorm(x, g):
    return (x * lax.rsqrt(jnp.mean(x * x, axis=-1, keepdims=True) + EPS)) * g


def _dot(a, b):
    return jnp.dot(a, b, preferred_element_type=F32)


def _gate_update(g, xc, ba, bx, cneg):
    hd = xc.shape[1]
    r = jax.nn.sigmoid(g[:, :hd] + ba)
    i = jax.nn.sigmoid(g[:, hd:] + bx)
    a = jnp.exp(r * cneg)
    y = 1.0 - a * a
    mult = jnp.where(y > 0.0, y * lax.rsqrt(y), 0.0)
    return a, xc * i, mult


def _lru_gates(xc, wg_ref, ba, bx, cneg):
    hd = xc.shape[1] // LRU_HEADS
    xcb = xc.astype(BF16)
    a_parts, m_parts, xi_parts = [], [], []
    for h in range(LRU_HEADS):
        sl = slice(h * hd, (h + 1) * hd)
        g = _dot(xcb[:, sl], wg_ref[h])
        a, xi, mult = _gate_update(g, xc[:, sl], ba[:, sl], bx[:, sl], cneg[:, sl])
        a_parts.append(a)
        m_parts.append(mult)
        xi_parts.append(xi)
    cat = lambda ps: jnp.concatenate(ps, axis=1)
    return cat(a_parts), cat(m_parts), cat(xi_parts)


def _pool_project(pooled, wp_ref, scale):
    n_g = len(POOL_WINDOWS)
    gd = pooled.shape[1] // n_g
    pb = pooled.astype(BF16)
    outs = [_dot(pb[:, g * gd:(g + 1) * gd], wp_ref[g]) for g in range(n_g)]
    return jnp.concatenate(outs, axis=1) * scale


def _neg_c_softplus(lam):
    return -LRU_C * jax.nn.softplus(-lam)


CONV_HALO = CONV_W - 1
POOL_HALO = POOL_BUF


def _rows(k, n=1):
    return pl.ds(k * SUBLANES, n * SUBLANES)


def _tile_copies(hbm, buf, sem, b, t, slot, n_grp, to_hbm):
    tm = n_grp * SUBLANES
    copies = []
    for s in range(SUBLANES):
        far = hbm.at[b, pl.ds(t * tm + s * n_grp, n_grp), :]
        near = buf.at[slot, :, s, :]
        src, dst = (near, far) if to_hbm else (far, near)
        copies.append(pltpu.make_async_copy(src, dst, sem.at[slot]))
    return copies


def _fill_halo(ext_ref, carry_ref, halo, n_grp):
    sub = lax.broadcasted_iota(jnp.int32, (SUBLANES, ext_ref.shape[1]), 0)
    for j in range(1, halo + 1):
        cur = ext_ref[_rows(halo + n_grp - j), :]
        prev = carry_ref[_rows(halo - j), :]
        ext_ref[_rows(halo - j), :] = pltpu.roll(jnp.where(sub == SUBLANES - 1, prev, cur), 1, 0)
    carry_ref[...] = ext_ref[_rows(n_grp, halo), :]


def _seq_mixer_kernel(x_hbm, gmix_ref, w_in_ref, cw_ref, cb_ref, wg_ref, ba_ref, bx_ref,
                      lam_ref, wp_ref, ps_ref, w_out_ref, wup32_ref, wdn32_ref,
                      o_hbm, hlast_ref, nconv_ref, npool_ref, wup16_ref, wdn16_ref,
                      xbuf, obuf, in_sem, out_sem, hn_ref, extx_ref, carryx_ref, extp_ref,
                      carryp_ref, xc_ref, xcb_ref, gate_ref, ug_ref, a_ref, b_ref, ymix_ref, hc_ref,
                      *, n_t):
    n_grp = xbuf.shape[1]
    tm = n_grp * SUBLANES
    d = xbuf.shape[3]
    w_lru = a_ref.shape[1]
    w_pool = extp_ref.shape[1]
    hd = w_lru // LRU_HEADS
    half = w_lru // 2
    b = pl.program_id(0)
    t = pl.program_id(1)
    step = b * n_t + t
    n_steps = pl.num_programs(0) * n_t
    slot = step % 2

    @pl.when(step == 0)
    def _first_fetch():
        for c in _tile_copies(x_hbm, xbuf, in_sem, b, t, slot, n_grp, False):
            c.start()

    @pl.when(step + 1 < n_steps)
    def _prefetch():
        nxt = step + 1
        for c in _tile_copies(x_hbm, xbuf, in_sem, nxt // n_t, nxt % n_t, 1 - slot, n_grp, False):
            c.start()

    for c in _tile_copies(x_hbm, xbuf, in_sem, b, t, slot, n_grp, False):
        c.wait()

    @pl.when(t == 0)
    def _reset_state():
        hc_ref[...] = jnp.zeros_like(hc_ref)
        carryx_ref[...] = jnp.zeros_like(carryx_ref)
        carryp_ref[...] = jnp.zeros_like(carryp_ref)

    x_tile = lambda: xbuf[slot].reshape(tm, d)
    hn_ref[...] = _rmsnorm(x_tile(), gmix_ref[...]).astype(BF16)
    in_proj = lambda c0, c1: _dot(hn_ref[...], w_in_ref[:, c0:c1])
    sub = lax.broadcasted_iota(jnp.int32, (SUBLANES, w_lru), 0)
    first_token = (sub == 0) & (t == 0)
    cneg = _neg_c_softplus(lam_ref[...])

    extx_ref[_rows(CONV_HALO, n_grp), :] = in_proj(0, w_lru)
    _fill_halo(extx_ref, carryx_ref, CONV_HALO, n_grp)

    extp_ref[_rows(POOL_HALO, n_grp), 0:w_pool // 2] = in_proj(2 * w_lru, 2 * w_lru + w_pool // 2)
    for gp in range(n_grp // 2):
        g = 2 * gp
        acc = cb_ref[...] + extx_ref[_rows(CONV_HALO + g, 2), :] * cw_ref[CONV_W - 1:CONV_W, :]
        for k in range(CONV_W - 1):
            acc = acc + extx_ref[_rows(g + k, 2), :] * cw_ref[k:k + 1, :]
        xc_ref[_rows(g, 2), :] = acc
        xcb_ref[_rows(g, 2), :] = acc.astype(BF16)
    for hh in range(LRU_HEADS):
        gate_ref[:, 2 * hd * hh:2 * hd * (hh + 1)] = _dot(xcb_ref[:, hd * hh:hd * (hh + 1)], wg_ref[hh])

    extp_ref[_rows(POOL_HALO, n_grp), w_pool // 2:] = in_proj(2 * w_lru + w_pool // 2, 2 * w_lru + w_pool)
    _fill_halo(extp_ref, carryp_ref, POOL_HALO, n_grp)
    ug_ref[:, 0:half] = in_proj(w_lru, w_lru + half)

    def _scan_groups(g0, g1, a_cum, h_loc):
        for g in range(g0, g1):
            gates = gate_ref[_rows(g), :]
            ga = jnp.concatenate([gates[:, 2 * hd * hh:2 * hd * hh + hd] for hh in range(LRU_HEADS)], axis=1)
            gx = jnp.concatenate([gates[:, 2 * hd * hh + hd:2 * hd * (hh + 1)] for hh in range(LRU_HEADS)], axis=1)
            a, xi, mult = _gate_update(jnp.concatenate([ga, gx], axis=1), xc_ref[_rows(g), :],
                                       ba_ref[...], bx_ref[...], cneg)
            if g == 0:
                mult = jnp.where(first_token, 1.0, mult)
            bt = xi * mult
            a_cum = a if a_cum is None else a * a_cum
            h_loc = bt if h_loc is None else a * h_loc + bt
            a_ref[_rows(g), :] = a_cum
            b_ref[_rows(g), :] = h_loc
        return a_cum, h_loc

    a_tot, h_tot = _scan_groups(0, n_grp // 2, None, None)
    ug_ref[:, half:] = in_proj(w_lru + half, 2 * w_lru)
    a_tot, h_tot = _scan_groups(n_grp // 2, n_grp, a_tot, h_tot)

    gd = w_pool // len(POOL_WINDOWS)
    row = lax.broadcasted_iota(jnp.int32, (tm, 1), 0)
    pos = t * tm + (row & (SUBLANES - 1)) * n_grp + (row >> 3)
    pooled = []
    for gi, w in enumerate(POOL_WINDOWS):
        cols = slice(gi * gd, (gi + 1) * gd)
        lo = POOL_HALO - (w - 1)
        s = extp_ref[_rows(lo, n_grp + w - 1), cols]
        width = 1
        while width < w:
            s = s[width * SUBLANES:, :] + s[:-width * SUBLANES, :]
            width *= 2
        inv = 1.0 / jnp.minimum(pos + 1, w).astype(F32)
        pooled.append(s * inv - extp_ref[_rows(POOL_HALO, n_grp), cols])

    y_pool = _pool_project(jnp.concatenate(pooled, axis=1), wp_ref, ps_ref[...])
    ymix_ref[:, w_lru:] = y_pool.astype(BF16)
    obuf[slot] = (x_tile() + _dot(ymix_ref[:, w_lru:], w_out_ref[w_lru:, :])).reshape(n_grp, SUBLANES, d)

    p_cum, q_cum = a_tot, h_tot
    for dd in (1, 2, 4):
        keep = sub >= dd
        q_cum = jnp.where(keep, p_cum * pltpu.roll(q_cum, dd, 0) + q_cum, q_cum)
        p_cum = jnp.where(keep, p_cum * pltpu.roll(p_cum, dd, 0), p_cum)
    h_in = jnp.broadcast_to(hc_ref[...], (SUBLANES, w_lru))
    seg_end = p_cum * h_in + q_cum
    seg_in = jnp.where(sub == 0, h_in, pltpu.roll(seg_end, 1, 0))
    h_end = seg_end[SUBLANES - 1:SUBLANES, :]
    hc_ref[...] = h_end

    seg_in2 = jnp.concatenate([seg_in, seg_in], axis=0)
    for gp in range(n_grp // 2):
        r2 = _rows(2 * gp, 2)
        hs = b_ref[r2, :] + a_ref[r2, :] * seg_in2
        ymix_ref[r2, 0:w_lru] = (hs * jax.nn.gelu(ug_ref[r2, :], approximate=True)).astype(BF16)

    obuf[slot] += _dot(ymix_ref[:, 0:w_lru], w_out_ref[0:w_lru, :]).reshape(n_grp, SUBLANES, d)

    wup16_ref[...] = wup32_ref[...].astype(BF16)
    wdn16_ref[...] = wdn32_ref[...].astype(BF16)

    for c in _tile_copies(o_hbm, obuf, out_sem, b, t, slot, n_grp, True):
        c.start()

    @pl.when(step > 0)
    def _wait_prev_writeback():
        prv = step - 1
        for c in _tile_copies(o_hbm, obuf, out_sem, prv // n_t, prv % n_t, 1 - slot, n_grp, True):
            c.wait()

    @pl.when(step == n_steps - 1)
    def _wait_last_writeback():
        for c in _tile_copies(o_hbm, obuf, out_sem, b, t, slot, n_grp, True):
            c.wait()

    @pl.when(t == n_t - 1)
    def _emit_state():
        last = SUBLANES - 1
        hlast_ref[...] = h_end
        for i in range(CONV_HALO):
            nconv_ref[i:i + 1, :] = carryx_ref[pl.ds(i * SUBLANES + last, 1), :]
        for i in range(POOL_HALO):
            npool_ref[i:i + 1, :] = carryp_ref[pl.ds(i * SUBLANES + last, 1), :]


def _const_spec(shape):
    nd = len(shape)
    return pl.BlockSpec(shape, lambda *_: (0,) * nd, pipeline_mode=pl.Buffered(1))


def _seq_mixer(x, p, w_up, w_down, tm):
    bsz, seq, d = x.shape
    w_lru = p["conv_b"].shape[1]
    w_pool = p["pool_scale"].shape[1]
    weights = (p["norm_mix_g"], p["w_in"], p["conv_w"], p["conv_b"], p["w_gate"], p["b_rg_a"],
               p["b_rg_x"], p["lru_lambda"], p["w_pool"], p["pool_scale"], p["w_out"])
    n_t = seq // tm
    n_steps = bsz * n_t
    n_grp = tm // SUBLANES
    assert n_grp % 2 == 0 and n_grp >= POOL_HALO
    any_spec = pl.BlockSpec(memory_space=pl.ANY)
    state_spec = lambda rows, w: pl.BlockSpec((None, rows, w), lambda b, t: (b, 0, 0))
    slab_spec = lambda w: pl.BlockSpec((w.shape[0] // n_steps, w.shape[1]),
                                       lambda b, t: (b * n_t + t, 0))
    ext_rows = lambda halo: (halo + n_grp) * SUBLANES
    return pl.pallas_call(
        functools.partial(_seq_mixer_kernel, n_t=n_t),
        grid=(bsz, n_t),
        in_specs=([any_spec] + [_const_spec(w.shape) for w in weights]
                  + [slab_spec(w_up), slab_spec(w_down)]),
        out_specs=[any_spec, state_spec(1, w_lru), state_spec(CONV_W - 1, w_lru),
                   state_spec(POOL_BUF, w_pool), slab_spec(w_up), slab_spec(w_down)],
        out_shape=[jax.ShapeDtypeStruct((bsz, seq, d), F32),
                   jax.ShapeDtypeStruct((bsz, 1, w_lru), F32),
                   jax.ShapeDtypeStruct((bsz, CONV_W - 1, w_lru), F32),
                   jax.ShapeDtypeStruct((bsz, POOL_BUF, w_pool), F32),
                   jax.ShapeDtypeStruct(w_up.shape, BF16),
                   jax.ShapeDtypeStruct(w_down.shape, BF16)],
        scratch_shapes=[
            pltpu.VMEM((2, n_grp, SUBLANES, d), F32),
            pltpu.VMEM((2, n_grp, SUBLANES, d), F32),
            pltpu.SemaphoreType.DMA((2,)),
            pltpu.SemaphoreType.DMA((2,)),
            pltpu.VMEM((tm, d), BF16),
            pltpu.VMEM((ext_rows(CONV_HALO), w_lru), F32),
            pltpu.VMEM((CONV_HALO * SUBLANES, w_lru), F32),
            pltpu.VMEM((ext_rows(POOL_HALO), w_pool), F32),
            pltpu.VMEM((POOL_HALO * SUBLANES, w_pool), F32),
            pltpu.VMEM((tm, w_lru), F32),
            pltpu.VMEM((tm, w_lru), BF16),
            pltpu.VMEM((tm, 2 * w_lru), F32),
            pltpu.VMEM((tm, w_lru), F32),
            pltpu.VMEM((tm, w_lru), F32),
            pltpu.VMEM((tm, w_lru), F32),
            pltpu.VMEM((tm, d), BF16),
            pltpu.VMEM((1, w_lru), F32),
        ],
        compiler_params=pltpu.CompilerParams(
            dimension_semantics=("arbitrary", "arbitrary"),
            vmem_limit_bytes=MIX_VMEM_LIMIT),
        name="seq_mixer",
    )(x, *weights, w_up, w_down)


DEC_CHUNK = 256


def _dec_mixer_kernel(x_ref, h0_ref, cbuf_hbm, pbuf_hbm, gmix_ref, w_in_ref, cw_ref, cb_ref,
                      wg_ref, ba_ref, bx_ref, lam_ref, wp_ref, ps_ref, w_out_ref,
                      o_ref, hnew_ref, nconv_hbm, npool_hbm, w_in16_ref, w_out16_ref,
                      hn_ref, proj_ref, wo16_ref, cbuf_ref, pbuf_ref, nconv_ref, npool_ref, state_sem):
    n_k = hn_ref.shape[0]
    j = pl.program_id(0)
    w_lru = h0_ref.shape[1]
    w_pool = ps_ref.shape[1]
    state_in = [pltpu.make_async_copy(cbuf_hbm, cbuf_ref, state_sem.at[0]),
                pltpu.make_async_copy(pbuf_hbm, pbuf_ref, state_sem.at[1])]
    state_out = [pltpu.make_async_copy(nconv_ref, nconv_hbm, state_sem.at[2]),
                 pltpu.make_async_copy(npool_ref, npool_hbm, state_sem.at[3])]

    @pl.when(j == 0)
    def _norm():
        for c in state_in:
            c.start()
        hn = _rmsnorm(x_ref[...], gmix_ref[...]).astype(BF16)
        for c in range(n_k):
            hn_ref[c] = hn[:, c * DEC_CHUNK:(c + 1) * DEC_CHUNK]
        proj_ref[...] = jnp.zeros_like(proj_ref)

    @pl.when(j < n_k)
    def _stream_weights():
        w = w_in_ref[...].astype(BF16)
        w_in16_ref[...] = w
        proj_ref[...] += _dot(hn_ref[j], w)
        w = w_out_ref[...].astype(BF16)
        w_out16_ref[...] = w
        wo16_ref[j] = w

    @pl.when(j == n_k)
    def _mix():
        ux = proj_ref[:, 0:w_lru]
        ug = proj_ref[:, w_lru:2 * w_lru]
        up = proj_ref[:, 2 * w_lru:]
        for c in state_in:
            c.wait()

        xc = cb_ref[...] + ux * cw_ref[CONV_W - 1:CONV_W, :]
        for k in range(CONV_W - 1):
            tap = cbuf_ref[k]
            xc = xc + tap * cw_ref[k:k + 1, :]
            if k > 0:
                nconv_ref[k - 1] = tap
        nconv_ref[CONV_W - 2] = ux

        a, mult, xi = _lru_gates(xc, wg_ref, ba_ref[...], bx_ref[...], _neg_c_softplus(lam_ref[...]))
        h_new = a * h0_ref[...] + xi * mult
        hnew_ref[...] = h_new
        y_lru = (h_new * jax.nn.gelu(ug, approximate=True)).astype(BF16)

        gd = w_pool // len(POOL_WINDOWS)
        s = up
        width = 1
        pooled = []
        for g, w in enumerate(POOL_WINDOWS):
            while width < w:
                k = POOL_BUF - width
                hist = pbuf_ref[k]
                if k > 0:
                    npool_ref[k - 1] = hist
                s = s + hist[:, g * gd:]
                width += 1
            count = float(min(PAST_LEN + 1, w))
            pooled.append(s[:, 0:gd] / count - up[:, g * gd:(g + 1) * gd])
            if g + 1 < len(POOL_WINDOWS):
                s = s[:, gd:]
        npool_ref[POOL_BUF - 1] = up
        y_pool = _pool_project(jnp.concatenate(pooled, axis=1), wp_ref, ps_ref[...])
        for c in state_out:
            c.start()

        y_mix = jnp.concatenate([y_lru, y_pool.astype(BF16)], axis=1)
        w_out16 = wo16_ref[...].reshape(n_k * DEC_CHUNK, o_ref.shape[1])
        o_ref[...] = x_ref[...] + _dot(y_mix, w_out16)
        for c in state_out:
            c.wait()


def _dec_mixer(x, h0, cbuf, pbuf, p, w_in, w_out):
    rows, d = x.shape
    n_k = d // DEC_CHUNK
    assert w_in.shape[0] == d and w_out.shape[0] == d
    small = (p["conv_w"], p["conv_b"], p["w_gate"], p["b_rg_a"], p["b_rg_x"], p["lru_lambda"],
             p["w_pool"], p["pool_scale"])
    full = lambda a: pl.BlockSpec(a.shape, functools.partial(lambda nd, j: (0,) * nd, a.ndim))
    once = lambda a: _const_spec(a.shape)
    in_hbm = pl.BlockSpec(memory_space=pl.ANY)
    row_chunk = lambda j: (jnp.minimum(j, n_k - 1), 0)
    return pl.pallas_call(
        _dec_mixer_kernel,
        grid=(n_k + 1,),
        in_specs=([once(x), once(h0), in_hbm, in_hbm,
                   once(p["norm_mix_g"]), pl.BlockSpec((DEC_CHUNK, w_in.shape[1]), row_chunk)]
                  + [once(a) for a in small]
                  + [pl.BlockSpec((DEC_CHUNK, w_out.shape[1]), row_chunk)]),
        out_specs=[full(x), full(h0), in_hbm, in_hbm,
                   pl.BlockSpec((DEC_CHUNK, w_in.shape[1]), row_chunk),
                   pl.BlockSpec((DEC_CHUNK, w_out.shape[1]), row_chunk)],
        out_shape=[jax.ShapeDtypeStruct(s.shape, F32) for s in (x, h0, cbuf, pbuf)]
                  + [jax.ShapeDtypeStruct(w_in.shape, BF16), jax.ShapeDtypeStruct(w_out.shape, BF16)],
        scratch_shapes=[
            pltpu.VMEM((n_k, rows, DEC_CHUNK), BF16),
            pltpu.VMEM((rows, w_in.shape[1]), F32),
            pltpu.VMEM((n_k, DEC_CHUNK, w_out.shape[1]), BF16),
            pltpu.VMEM(cbuf.shape, F32),
            pltpu.VMEM(pbuf.shape, F32),
            pltpu.VMEM(cbuf.shape, F32),
            pltpu.VMEM(pbuf.shape, F32),
            pltpu.SemaphoreType.DMA((4,)),
        ],
        compiler_params=pltpu.CompilerParams(
            dimension_semantics=("arbitrary",), vmem_limit_bytes=DEC_VMEM_LIMIT),
        name="dec_mixer",
    )(x, h0, cbuf, pbuf, p["norm_mix_g"], w_in, *small, w_out)


def _mlp_kernel(x_ref, xs_ref, g_ref, wup_ref, wdn_ref, gfin_ref, o_ref, os_ref, h_ref):
    tm = x_ref.shape[0]
    i = pl.program_id(0)
    j = pl.program_id(1)
    first = j == 0
    last = j == pl.num_programs(1) - 1
    with_decode = i == 0

    def _start(src_ref, rows, acc_ref):
        x = src_ref[...]
        h_ref[rows, :] = _rmsnorm(x, g_ref[...]).astype(BF16)
        acc_ref[...] = x

    def _ffn(rows):
        f = jnp.maximum(_dot(h_ref[rows, :], wup_ref[...]), 0.0)
        return _dot((f * f).astype(BF16), wdn_ref[...])

    pl.when(first & with_decode)(lambda: _start(x_ref, pl.ds(0, tm), o_ref))
    pl.when(first & with_decode)(lambda: _start(xs_ref, pl.ds(tm, xs_ref.shape[0]), os_ref))

    @pl.when(with_decode)
    def _both_streams():
        part = _ffn(pl.ds(0, h_ref.shape[0]))
        o_ref[...] += part[:tm]
        os_ref[...] += part[tm:]

    prompt_only = jnp.logical_not(with_decode)

    @pl.when(prompt_only & first)
    def _first_tile():
        x = x_ref[...]
        o_ref[...] = x
        scale = lax.rsqrt(jnp.mean(x * x, axis=-1, keepdims=True) + EPS)
        d = x.shape[1]
        kc = d // NORM_CHUNKS
        f = None
        for c in range(NORM_CHUNKS):
            cols = slice(c * kc, (c + 1) * kc)
            hc = ((x_ref[:, cols] * scale) * g_ref[:, cols]).astype(BF16)
            h_ref[pl.ds(0, tm), cols] = hc
            part = _dot(hc, wup_ref[cols, :])
            f = part if f is None else f + part
        f = jnp.maximum(f, 0.0)
        o_ref[...] += _dot((f * f).astype(BF16), wdn_ref[...])

    @pl.when(prompt_only & jnp.logical_not(first) & jnp.logical_not(last))
    def _prompt_only():
        o_ref[...] += _ffn(pl.ds(0, tm))

    @pl.when(prompt_only & last)
    def _last_tile():
        o_ref[...] = _rmsnorm(o_ref[...] + _ffn(pl.ds(0, tm)), gfin_ref[...])

    @pl.when(last & with_decode)
    def _finish():
        o_ref[...] = _rmsnorm(o_ref[...], gfin_ref[...])

    @pl.when(last & with_decode)
    def _finish_decode():
        os_ref[...] = _rmsnorm(os_ref[...], gfin_ref[...])


def _mlp(x, xs, p, tm, tf):
    n, d = x.shape
    ns = xs.shape[0]
    d_ff = p["w_up"].shape[1]
    vec_spec = pl.BlockSpec((1, d), lambda i, j: (0, 0))
    tok_spec = pl.BlockSpec((tm, d), lambda i, j: (i, 0))
    dec_spec = pl.BlockSpec((ns, d), lambda i, j: (0, 0))
    return pl.pallas_call(
        _mlp_kernel,
        grid=(n // tm, d_ff // tf),
        in_specs=[tok_spec, dec_spec, vec_spec,
                  pl.BlockSpec((d, tf), lambda i, j: (0, j)),
                  pl.BlockSpec((tf, d), lambda i, j: (j, 0)),
                  vec_spec],
        out_specs=[tok_spec, dec_spec],
        out_shape=[jax.ShapeDtypeStruct((n, d), F32), jax.ShapeDtypeStruct((ns, d), F32)],
        scratch_shapes=[pltpu.VMEM((tm + ns, d), BF16)],
        compiler_params=pltpu.CompilerParams(
            dimension_semantics=("arbitrary", "arbitrary"),
            vmem_limit_bytes=MLP_VMEM_LIMIT),
        name="mlp",
    )(x, xs, p["norm_mlp_g"], p["w_up"], p["w_down"], p["norm_final_g"])


def _layer_params(l, norm_mix_g, w_in, conv_w, conv_b, w_rg_a, b_rg_a, w_rg_x, b_rg_x, lru_lambda,
                  w_pool, pool_scale, w_out, norm_mlp_g, w_up, w_down, norm_final_g):
    row = lambda v: v.reshape(1, -1)
    return {
        "norm_mix_g": row(norm_mix_g[l]),
        "w_in": w_in[l],
        "conv_w": conv_w[l],
        "conv_b": row(conv_b[l]),
        "w_gate": jnp.concatenate([w_rg_a[l], w_rg_x[l]], axis=-1).astype(BF16),
        "b_rg_a": row(b_rg_a[l]),
        "b_rg_x": row(b_rg_x[l]),
        "lru_lambda": row(lru_lambda[l]),
        "w_pool": w_pool[l].astype(BF16),
        "pool_scale": row(pool_scale[l]),
        "w_out": w_out[l],
        "norm_mlp_g": row(norm_mlp_g[l]),
        "w_up": w_up[l],
        "w_down": w_down[l],
        "norm_final_g": row(norm_final_g),
    }


def kernel(x_prompt, x_sample, state_lru_h, state_conv, state_pool, norm_mix_g, w_in, conv_w, conv_b, w_rg_a, b_rg_a, w_rg_x, b_rg_x, lru_lambda, w_pool, pool_scale, w_out, norm_mlp_g, w_up, w_down, norm_final_g):
    depth = w_in.shape[0]
    assert depth == 1, "final RMSNorm is fused into the (single) layer's MLP kernel"
    bp, tp, d = x_prompt.shape
    bs, ts, _ = x_sample.shape
    assert ts == 1 and tp % MIX_TM == 0 and (bp * tp) % MLP_TM == 0
    p = _layer_params(0, norm_mix_g, w_in, conv_w, conv_b, w_rg_a, b_rg_a, w_rg_x, b_rg_x,
                      lru_lambda, w_pool, pool_scale, w_out, norm_mlp_g, w_up, w_down, norm_final_g)

    hist_major = lambda s: jnp.transpose(s, (1, 0, 2))
    x1s, hs, cs, ps, p["w_in"], p["w_out"] = _dec_mixer(
        x_sample.reshape(bs, d), state_lru_h[0], hist_major(state_conv[0]), hist_major(state_pool[0]),
        p, p["w_in"], p["w_out"])
    cs, ps = hist_major(cs), hist_major(ps)
    x1p, hp, cp, pp, p["w_up"], p["w_down"] = _seq_mixer(x_prompt, p, p["w_up"], p["w_down"], MIX_TM)
    yp, ys = _mlp(x1p.reshape(bp * tp, d), x1s, p, MLP_TM, MLP_TF)
    yp, ys = yp.reshape(bp, tp, d), ys.reshape(bs, ts, d)

    return (yp, ys, hp.reshape(1, bp, -1), cp[None], pp[None], hs[None], cs[None], ps[None])
```
